```python
import jax, jax.numpy as jnp
from jax import lax
import numpy as np

D_MODEL = 1024
BATCH = 8
SEQ = 2048
DEPTH = 2

CHUNK = 64
Q_BLOCK = 128
EPS = 1e-6
GLA_HEADS = 4
GLA_DK = 48
GLA_DV = 96
GLA_RANK = 16
GLA_TAU = 16.0
SB_HEADS = 6
SB_DH = 64
CONV_CH = 256
CONV_WIDTH = 31
D_FF = 2816
N_MOD = 9

GLA_QK = GLA_HEADS * GLA_DK
GLA_V = GLA_HEADS * GLA_DV
SB_W = SB_HEADS * SB_DH
MIX_W = GLA_V + SB_W + CONV_CH
IN_COLS = 2 * GLA_QK + 2 * GLA_V + GLA_RANK + 3 * SB_W + 2 * CONV_CH

kernel_name = "hybrid_gla_stickbreak_conformer_block"


def rms_norm(x, g):
    xf = x.astype(jnp.float32)
    y = xf * lax.rsqrt(jnp.mean(xf * xf, axis=-1, keepdims=True) + EPS)
    return (y * g.astype(jnp.float32)).astype(x.dtype)


def layer_norm(x, g, b):
    xf = x.astype(jnp.float32)
    mu = jnp.mean(xf, axis=-1, keepdims=True)
    xc = xf - mu
    y = xc * lax.rsqrt(jnp.mean(xc * xc, axis=-1, keepdims=True) + EPS)
    return (y * g.astype(jnp.float32) + b.astype(jnp.float32)).astype(x.dtype)


def modulate(x, g, shift, scale):
    return rms_norm(x, g) * (1.0 + scale[:, None, :]) + shift[:, None, :]


def swiglu(h, w_in, w_out):
    a, b = jnp.split(h @ w_in, 2, axis=-1)
    return (jax.nn.silu(a) * b) @ w_out


def split_heads(t, n_heads):
    B, T, _ = t.shape
    return t.reshape(B, T, n_heads, -1).transpose(0, 2, 1, 3)


def merge_heads(t):
    B, H, T, d = t.shape
    return t.transpose(0, 2, 1, 3).reshape(B, T, H * d)


def gla_chunked(q, k, v, log_a):
    B, H, T, DK = q.shape
    DV = v.shape[-1]
    NC = T // CHUNK

    def to_chunks(t):
        return jnp.moveaxis(t.reshape(B, H, NC, CHUNK, t.shape[-1]), 2, 0)

    qc, kc, vc, gc = to_chunks(q), to_chunks(k), to_chunks(v), to_chunks(log_a)
    causal = jnp.tril(jnp.ones((CHUNK, CHUNK), dtype=bool))

    def step(S, inp):
        qi, ki, vi, gi = inp
        b = jnp.cumsum(gi, axis=2)
        o_inter = jnp.einsum('bhtk,bhkv->bhtv', qi * jnp.exp(b), S)
        diff = b[:, :, :, None, :] - b[:, :, None, :, :]
        decay = jnp.exp(jnp.where(causal[:, :, None], diff, -jnp.inf))
        scores = jnp.einsum('bhtk,bhsk,bhtsk->bhts', qi, ki, decay)
        o_intra = jnp.einsum('bhts,bhsv->bhtv', scores, vi)
        b_last = b[:, :, -1:, :]
        k_dec = ki * jnp.exp(b_last - b)
        S_new = jnp.exp(b_last[:, :, 0, :])[..., None] * S + jnp.einsum('bhsk,bhsv->bhkv', k_dec, vi)
        return S_new, o_inter + o_intra

    S0 = jnp.zeros((B, H, DK, DV), jnp.float32)
    _, o = lax.scan(step, S0, (qc, kc, vc, gc))
    return jnp.moveaxis(o, 0, 2).reshape(B, H, T, DV)


def stick_breaking(q, k, v):
    T = q.shape[2]
    scale = SB_DH ** -0.5
    outs = []
    for i in range(T // Q_BLOCK):
        q0 = i * Q_BLOCK
        kv_len = q0 + Q_BLOCK
        qb = q[:, :, q0:kv_len]
        kb = k[:, :, :kv_len]
        vb = v[:, :, :kv_len]
        z = jnp.einsum('bhqd,bhkd->bhqk', qb, kb) * scale
        t_idx = q0 + jnp.arange(Q_BLOCK)[:, None]
        s_idx = jnp.arange(kv_len)[None, :]
        mask = s_idx < t_idx
        log_keep = jnp.where(mask, jax.nn.log_sigmoid(-z), 0.0)
        cum = jnp.cumsum(log_keep, axis=-1)
        log_w = jax.nn.log_sigmoid(z) + (cum[..., -1:] - cum)
        w = jnp.where(mask, jnp.exp(log_w), 0.0)
        outs.append(jnp.einsum('bhqk,bhkd->bhqd', w, vb))
    return jnp.concatenate(outs, axis=2)


def causal_depthwise_conv(u, w, b):
    out = lax.conv_general_dilated(
        u, w[:, None, :], window_strides=(1,), padding=[(CONV_WIDTH - 1, 0)],
        dimension_numbers=('NWC', 'WIO', 'NWC'), feature_group_count=u.shape[-1])
    return out + b


def hybrid_mixer(h, w_in, w_out, gla_w_gate_up, gla_b_gate, gla_out_norm,
                 sb_q_norm, sb_k_norm, sb_out_norm, conv_w, conv_b, conv_ln_g, conv_ln_b):
    dt = h.dtype
    proj = h @ w_in
    idx = [GLA_QK, 2 * GLA_QK, 2 * GLA_QK + GLA_V, 2 * GLA_QK + 2 * GLA_V,
           2 * GLA_QK + 2 * GLA_V + GLA_RANK]
    idx = idx + [idx[-1] + SB_W, idx[-1] + 2 * SB_W, idx[-1] + 3 * SB_W]
    g_q, g_k, g_v, g_g, g_r, s_q, s_k, s_v, c_u = jnp.split(proj, idx, axis=-1)

    log_a = jax.nn.log_sigmoid(g_r @ gla_w_gate_up + gla_b_gate) / GLA_TAU
    qa = split_heads(g_q, GLA_HEADS).astype(jnp.float32) * (GLA_DK ** -0.5)
    ka = split_heads(g_k, GLA_HEADS).astype(jnp.float32)
    va = split_heads(g_v, GLA_HEADS).astype(jnp.float32)
    la = split_heads(log_a, GLA_HEADS).astype(jnp.float32)
    oa = rms_norm(gla_chunked(qa, ka, va, la), gla_out_norm)
    out_a = merge_heads(oa).astype(dt) * jax.nn.silu(g_g)

    qb = rms_norm(split_heads(s_q, SB_HEADS), sb_q_norm).astype(jnp.float32)
    kb = rms_norm(split_heads(s_k, SB_HEADS), sb_k_norm).astype(jnp.float32)
    vb = split_heads(s_v, SB_HEADS).astype(jnp.float32)
    ob = rms_norm(stick_breaking(qb, kb, vb), sb_out_norm)
    out_b = merge_heads(ob).astype(dt)

    cu_a, cu_g = jnp.split(c_u, 2, axis=-1)
    u = cu_a * jax.nn.sigmoid(cu_g)
    u = causal_depthwise_conv(u, conv_w, conv_b)
    out_c = jax.nn.silu(layer_norm(u, conv_ln_g, conv_ln_b))

    return jnp.concatenate([out_a, out_b, out_c], axis=-1) @ w_out


def setup_inputs(seed: int = 0) -> dict:
    key = jax.random.key(seed)
    ks = jax.random.split(key, 24)
    L, D = DEPTH, D_MODEL

    def nrm(k, shape, s):
        return jax.random.normal(k, shape, jnp.float32) * s

    def gain(k, shape):
        return 1.0 + 0.02 * jax.random.normal(k, shape, jnp.float32)

    return {
        "x": nrm(ks[0], (BATCH, SEQ, D), 1.0),
        "c": nrm(ks[1], (BATCH, D), 1.0),
        "w_ada": nrm(ks[2], (L, D, N_MOD * D), D ** -0.5),
        "b_ada": nrm(ks[3], (L, N_MOD * D), 0.02),
        "norm_ffn1": gain(ks[4], (L, D)),
        "ffn1_w_in": nrm(ks[5], (L, D, 2 * D_FF), D ** -0.5),
        "ffn1_w_out": nrm(ks[6], (L, D_FF, D), D_FF ** -0.5),
        "norm_mix": gain(ks[7], (L, D)),
        "w_in": nrm(ks[8], (L, D, IN_COLS), D ** -0.5),
        "w_out": nrm(ks[9], (L, MIX_W, D), MIX_W ** -0.5),
        "gla_w_gate_up": nrm(ks[10], (L, GLA_RANK, GLA_QK), GLA_RANK ** -0.5),
        "gla_b_gate": nrm(ks[11], (L, GLA_QK), 0.1),
        "gla_out_norm": gain(ks[12], (L, GLA_DV)),
        "sb_q_norm": gain(ks[13], (L, SB_DH)),
        "sb_k_norm": gain(ks[14], (L, SB_DH)),
        "sb_out_norm": gain(ks[15], (L, SB_DH)),
        "conv_w": nrm(ks[16], (L, CONV_WIDTH, CONV_CH), CONV_WIDTH ** -0.5),
        "conv_b": nrm(ks[17], (L, CONV_CH), 0.02),
        "conv_ln_g": gain(ks[18], (L, CONV_CH)),
        "conv_ln_b": nrm(ks[19], (L, CONV_CH), 0.02),
        "norm_ffn2": gain(ks[20], (L, D)),
        "ffn2_w_in": nrm(ks[21], (L, D, 2 * D_FF), D ** -0.5),
        "ffn2_w_out": nrm(ks[22], (L, D_FF, D), D_FF ** -0.5),
    }


def reference(x, c, w_ada, b_ada, norm_ffn1, ffn1_w_in, ffn1_w_out, norm_mix, w_in, w_out,
              gla_w_gate_up, gla_b_gate, gla_out_norm, sb_q_norm, sb_k_norm, sb_out_norm,
              conv_w, conv_b, conv_ln_g, conv_ln_b, norm_ffn2, ffn2_w_in, ffn2_w_out):
    c_act = jax.nn.silu(c)
    for l in range(DEPTH):
        mod = c_act @ w_ada[l] + b_ada[l]
        sh1, sc1, gt1, sh2, sc2, gt2, sh3, sc3, gt3 = jnp.split(mod, N_MOD, axis=-1)
        h = modulate(x, norm_ffn1[l], sh1, sc1)
        x = x + 0.5 * gt1[:, None, :] * swiglu(h, ffn1_w_in[l], ffn1_w_out[l])
        h = modulate(x, norm_mix[l], sh2, sc2)
        y = hybrid_mixer(h, w_in[l], w_out[l], gla_w_gate_up[l], gla_b_gate[l], gla_out_norm[l],
                         sb_q_norm[l], sb_k_norm[l], sb_out_norm[l],
                         conv_w[l], conv_b[l], conv_ln_g[l], conv_ln_b[l])
        x = x + gt2[:, None, :] * y
        h = modulate(x, norm_ffn2[l], sh3, sc3)
        x = x + 0.5 * gt3[:, None, :] * swiglu(h, ffn2_w_in[l], ffn2_w_out[l])
    return x
```

```python
import functools

import numpy as np
import jax
import jax.numpy as jnp
from jax import lax
from jax.experimental import pallas as pl
from jax.experimental.pallas import tpu as pltpu

F32 = jnp.float32
BF16 = jnp.bfloat16

EPS = 1e-6
CHUNK = 64
GLA_HEADS, GLA_DK, GLA_DV, GLA_RANK, GLA_TAU = 4, 48, 96, 16, 16.0
SB_HEADS, SB_DH = 6, 64
CONV_CH, CONV_WIDTH = 256, 31
D_FF = 2816
N_MOD = 9
GLA_QK = GLA_HEADS * GLA_DK
GLA_V = GLA_HEADS * GLA_DV
SB_W = SB_HEADS * SB_DH

LANES = 128
QK_PAD = 256
RANK_PAD = 128
FF_CHUNK = 256
TOKEN_TILE = 512
SB_BLOCK = 128
SB_SKIP = 110.0
GLA_TILE = 512
CONV_TILE = 256
CONV_PAD = 32
VMEM_LIMIT = 52 * 1024 * 1024

_OFF_GQ, _OFF_GK, _OFF_GV, _OFF_GG, _OFF_GR = 0, 256, 512, 896, 1280
_OFF_SQ, _OFF_SK, _OFF_SV, _OFF_CA, _OFF_CG = 1408, 1792, 2176, 2560, 2816
PROJ_W = 3072


def _dot(a, b):
    return jnp.dot(a, b, preferred_element_type=F32)


def _dot_nt(a, b):
    return lax.dot_general(a, b, (((1,), (1,)), ((), ())), preferred_element_type=F32)


def _dot_tn(a, b):
    return lax.dot_general(a, b, (((0,), (0,)), ((), ())), preferred_element_type=F32)


def _split_dot(x, m):
    hi = x.astype(BF16)
    lo = (x - hi.astype(F32)).astype(BF16)
    return _dot(hi, m) + _dot(lo, m)


def _split_dot_left(m, x):
    hi = x.astype(BF16)
    lo = (x - hi.astype(F32)).astype(BF16)
    return _dot(m, hi) + _dot(m, lo)


def _sigmoid(x):
    return 1.0 / (1.0 + jnp.exp(-x))


def _silu(x):
    return x * _sigmoid(x)


def _log_sigmoid(x):
    return jnp.minimum(x, 0.0) - jnp.log(1.0 + jnp.exp(-jnp.abs(x)))


def _modulated_norm(x, g, scale, shift):
    ms = jnp.mean(x * x, axis=-1, keepdims=True)
    return (x * lax.rsqrt(ms + EPS) * g) * (1.0 + scale) + shift


def _adaln_kernel(c_ref, w_ref, b_ref, o_ref):
    ca = _silu(c_ref[...]).astype(BF16)
    o_ref[0] = _dot(ca, w_ref[0].astype(BF16)) + b_ref[0]


def _adaln(c, w_ada, b_ada):
    L, D, W = w_ada.shape
    B = c.shape[0]
    tn = 1536
    return pl.pallas_call(
        _adaln_kernel,
        out_shape=jax.ShapeDtypeStruct((L, B, W), F32),
        grid=(L, W // tn),
        in_specs=[
            pl.BlockSpec((B, D), lambda l, j: (0, 0)),
            pl.BlockSpec((1, D, tn), lambda l, j: (l, 0, j)),
            pl.BlockSpec((1, 1, tn), lambda l, j: (l, 0, j)),
        ],
        out_specs=pl.BlockSpec((1, B, tn), lambda l, j: (l, 0, j)),
        compiler_params=pltpu.CompilerParams(
            dimension_semantics=("parallel", "parallel"), vmem_limit_bytes=VMEM_LIMIT),
        name="adaln",
    )(c, w_ada, b_ada.reshape(L, 1, W))


def _ffn_kernel(x_ref, g_ref, sh_ref, sc_ref, gt_ref, win_ref, wout_ref, o_ref, acc_ref):
    x = x_ref[...]
    hb = _modulated_norm(x, g_ref[...], sc_ref[0], sh_ref[0]).astype(BF16)
    n = wout_ref.shape[0]
    for j in range(n):
        a = _dot(hb, win_ref[j])
        b = _dot(hb, win_ref[n + j])
        y = _dot((_silu(a) * b).astype(BF16), wout_ref[j])
        if j == 0:
            acc_ref[...] = y
        else:
            acc_ref[...] += y
    o_ref[...] = x + (0.5 * gt_ref[0]) * acc_ref[...]


def _const_spec(shape):
    nd = len(shape)
    return pl.BlockSpec(shape, lambda *_: (0,) * nd, pipeline_mode=pl.Buffered(1))


def _row_spec(tm, w):
    return pl.BlockSpec((tm, w), lambda i: (i, 0))


def _batch_vec_spec(tiles_per_batch, w):
    return pl.BlockSpec((1, 1, w), lambda i: (i // tiles_per_batch, 0, 0))


def _ffn(x2, g, shift, scale, gate, w_in, w_out, seq):
    N, D = x2.shape
    nchunk = D_FF // FF_CHUNK
    win = w_in.reshape(D, 2 * nchunk, FF_CHUNK).transpose(1, 0, 2).astype(BF16)
    wout = w_out.reshape(nchunk, FF_CHUNK, D).astype(BF16)
    tm = TOKEN_TILE
    tpb = seq // tm
    return pl.pallas_call(
        _ffn_kernel,
        out_shape=jax.ShapeDtypeStruct((N, D), F32),
        grid=(N // tm,),
        in_specs=[
            _row_spec(tm, D),
            _const_spec((1, D)),
            _batch_vec_spec(tpb, D), _batch_vec_spec(tpb, D), _batch_vec_spec(tpb, D),
            _const_spec(win.shape), _const_spec(wout.shape),
        ],
        out_specs=_row_spec(tm, D),
        scratch_shapes=[pltpu.VMEM((tm, D), F32)],
        compiler_params=pltpu.CompilerParams(
            dimension_semantics=("parallel",), vmem_limit_bytes=VMEM_LIMIT),
        name="ffn",
    )(x2, g.reshape(1, D), shift, scale, gate, win, wout)


def _inproj_kernel(x_ref, g_ref, sh_ref, sc_ref, w_ref, wup_ref, bg_ref,
                   gq_ref, gk_ref, ga_ref, gv_ref, gg_ref, sq_ref, sk_ref, sv_ref, cu_ref):
    hb = _modulated_norm(x_ref[...], g_ref[...], sc_ref[0], sh_ref[0]).astype(BF16)
    p = _dot(hb, w_ref[...])
    gq_ref[...] = p[:, _OFF_GQ:_OFF_GQ + QK_PAD]
    gk_ref[...] = p[:, _OFF_GK:_OFF_GK + QK_PAD]
    gv_ref[...] = p[:, _OFF_GV:_OFF_GV + GLA_V]
    gg_ref[...] = p[:, _OFF_GG:_OFF_GG + GLA_V]
    r = p[:, _OFF_GR:_OFF_GR + RANK_PAD].astype(BF16)
    ga_ref[...] = _log_sigmoid(_dot(r, wup_ref[...]) + bg_ref[...]) * (1.0 / GLA_TAU)
    sq_ref[...] = p[:, _OFF_SQ:_OFF_SQ + SB_W]
    sk_ref[...] = p[:, _OFF_SK:_OFF_SK + SB_W]
    sv_ref[...] = p[:, _OFF_SV:_OFF_SV + SB_W]
    cu_ref[...] = p[:, _OFF_CA:_OFF_CA + CONV_CH] * _sigmoid(p[:, _OFF_CG:_OFF_CG + CONV_CH])


def _pad_cols(w, width):
    return jnp.pad(w, ((0, 0), (0, width - w.shape[1])))


def _inproj(x2, g, shift, scale, w_in, w_up, b_gate, seq):
    N, D = x2.shape
    o = 0
    parts = []
    for width, pad in ((GLA_QK, QK_PAD), (GLA_QK, QK_PAD), (GLA_V, GLA_V), (GLA_V, GLA_V),
                       (GLA_RANK, RANK_PAD), (SB_W, SB_W), (SB_W, SB_W), (SB_W, SB_W),
                       (CONV_CH, CONV_CH), (CONV_CH, CONV_CH)):
        parts.append(_pad_cols(w_in[:, o:o + width], pad))
        o += width
    w = jnp.concatenate(parts, axis=1).astype(BF16)
    wup = jnp.pad(w_up, ((0, RANK_PAD - GLA_RANK), (0, QK_PAD - GLA_QK))).astype(BF16)
    bg = jnp.pad(b_gate, (0, QK_PAD - GLA_QK)).reshape(1, QK_PAD)
    tm = TOKEN_TILE
    tpb = seq // tm
    widths = (QK_PAD, QK_PAD, QK_PAD, GLA_V, GLA_V, SB_W, SB_W, SB_W, CONV_CH)
    return pl.pallas_call(
        _inproj_kernel,
        out_shape=[jax.ShapeDtypeStruct((N, wd), F32) for wd in widths],
        grid=(N // tm,),
        in_specs=[
            _row_spec(tm, D),
            _const_spec((1, D)),
            _batch_vec_spec(tpb, D), _batch_vec_spec(tpb, D),
            _const_spec(w.shape), _const_spec(wup.shape), _const_spec(bg.shape),
        ],
        out_specs=[_row_spec(tm, wd) for wd in widths],
        compiler_params=pltpu.CompilerParams(
            dimension_semantics=("parallel",), vmem_limit_bytes=VMEM_LIMIT),
        name="inproj",
    )(x2, g.reshape(1, D), shift, scale, w, wup, bg)


_GLA_LEVELS = 6


def _gla_constants():
    C = CHUNK
    t = np.arange(C)[:, None]
    j = np.arange(C)[None, :]
    mats = [j <= t, j > t]
    masks = [t == j]
    for l in range(1, _GLA_LEVELS + 1):
        n, m = 1 << l, 1 << (l - 1)
        ref = t // n * n + m - 1
        right = (t % n) >= m
        mats.append(right & (j > ref) & (j <= t))
        mats.append((~right) & (j > t) & (j <= ref))
        masks.append(((t // n) == (j // n)) & right & ((j % n) < m))
    prefix = np.concatenate(mats, axis=0).astype(np.float32)
    masks = np.stack([np.tile(mk, (1, GLA_HEADS)) for mk in masks]).astype(np.float32)
    kl = np.arange(QK_PAD)[None, :] // GLA_DK
    vl = np.arange(GLA_V)[:, None] // GLA_DV
    state_mask = (kl == vl).astype(np.float32)
    seg = (np.arange(GLA_V)[:, None] // GLA_DV == np.arange(GLA_V)[None, :] // GLA_DV)
    return prefix, masks, state_mask, seg.astype(np.float32)


def _gla_kernel(q_ref, k_ref, a_ref, v_ref, g_ref, pre_ref, msk_ref, smask_ref, seg_ref, on_ref,
                o_ref, st_ref):
    C = CHUNK

    @pl.when(pl.program_id(1) == 0)
    def _():
        st_ref[...] = jnp.zeros_like(st_ref)

    klane = lax.broadcasted_iota(jnp.int32, (1, QK_PAD), 1) // GLA_DK
    vlane = lax.broadcasted_iota(jnp.int32, (1, GLA_V), 1) // GLA_DV
    pre = pre_ref[...]

    def chunk(c, carry):
        rows = pl.ds(pl.multiple_of(c * C, C), C)
        q = q_ref[0, rows, :] * (GLA_DK ** -0.5)
        k = k_ref[0, rows, :]
        v = v_ref[0, rows, :]
        e = _split_dot_left(pre, a_ref[0, rows, :])
        vb = v.astype(BF16)

        st = st_ref[...]
        o = _dot_nt((q * jnp.exp(e[0:C])).astype(BF16), st.astype(BF16))

        att = jnp.zeros((C, GLA_HEADS * C), F32)
        for l in range(_GLA_LEVELS + 1):
            if l == 0:
                qt, kt = q, k
            else:
                base = (2 * l) * C
                qt = q * jnp.exp(e[base:base + C])
                kt = k * jnp.exp(e[base + C:base + 2 * C])
            kst = jnp.concatenate(
                [jnp.where(klane == h, kt, 0.0) for h in range(GLA_HEADS)], axis=0).astype(BF16)
            att = att + _dot_nt(qt.astype(BF16), kst) * msk_ref[l]
        vst = jnp.concatenate(
            [jnp.where(vlane == h, v, 0.0) for h in range(GLA_HEADS)], axis=0).astype(BF16)
        o = o + _dot(att.astype(BF16), vst)

        kdec = (k * jnp.exp(e[C:2 * C])).astype(BF16)
        upd = _dot_tn(vb, kdec) * smask_ref[...]
        st_ref[...] = st * jnp.exp(e[C - 1:C]) + upd

        ssq = _split_dot(o * o, seg_ref[...])
        y = o * lax.rsqrt(ssq * (1.0 / GLA_DV) + EPS) * on_ref[...]
        o_ref[0, rows, :] = y * _silu(g_ref[0, rows, :])
        return carry

    lax.fori_loop(0, q_ref.shape[1] // C, chunk, 0)


def _gla(gq, gk, ga, gv, gg, out_norm):
    B, T, _ = gq.shape
    prefix, masks, state_mask, seg = _gla_constants()
    tt = GLA_TILE
    seq_spec = lambda w: pl.BlockSpec((1, tt, w), lambda b, i: (b, i, 0))
    consts = (jnp.asarray(prefix, BF16), jnp.asarray(masks), jnp.asarray(state_mask),
              jnp.asarray(seg, BF16), jnp.tile(out_norm, GLA_HEADS).reshape(1, GLA_V))
    return pl.pallas_call(
        _gla_kernel,
        out_shape=jax.ShapeDtypeStruct((B, T, GLA_V), F32),
        grid=(B, T // tt),
        in_specs=[seq_spec(QK_PAD), seq_spec(QK_PAD), seq_spec(QK_PAD), seq_spec(GLA_V), seq_spec(GLA_V)]
                 + [_const_spec(cst.shape) for cst in consts],
        out_specs=seq_spec(GLA_V),
        scratch_shapes=[pltpu.VMEM((GLA_V, QK_PAD), F32)],
        compiler_params=pltpu.CompilerParams(
            dimension_semantics=("parallel", "arbitrary"), vmem_limit_bytes=VMEM_LIMIT),
        name="gla",
    )(gq, gk, ga, gv, gg, *consts)


def _pair_rms(x, gain, first):
    x2 = x * x
    s0 = jnp.sum(jnp.where(first, x2, 0.0), axis=-1, keepdims=True)
    s1 = jnp.sum(jnp.where(first, 0.0, x2), axis=-1, keepdims=True)
    ms = jnp.where(first, s0, s1) * (1.0 / SB_DH)
    return x * lax.rsqrt(ms + EPS) * gain


def _sb_kernel(q_ref, k_ref, v_ref, qg_ref, kg_ref, og_ref, suf_ref, o_ref, qn_ref, kn_ref, vb_ref):
    T = q_ref.shape[1]
    BLK = SB_BLOCK
    first = lax.broadcasted_iota(jnp.int32, (1, 2 * SB_DH), 1) < SB_DH
    row = lax.broadcasted_iota(jnp.int32, (BLK, BLK), 0)
    col = lax.broadcasted_iota(jnp.int32, (BLK, BLK), 1)
    tri = col < row

    def prep(i, carry):
        rows = pl.ds(pl.multiple_of(i * BLK, BLK), BLK)
        qn_ref[rows, :] = (_pair_rms(q_ref[0, rows, :], qg_ref[...], first) * (SB_DH ** -0.5)).astype(BF16)
        kn_ref[rows, :] = _pair_rms(k_ref[0, rows, :], kg_ref[...], first).astype(BF16)
        vb_ref[rows, :] = v_ref[0, rows, :].astype(BF16)
        return carry

    lax.fori_loop(0, T // BLK, prep, 0)

    def block(qh, kj, carry, acc, diag):
        krows = pl.ds(pl.multiple_of(kj * BLK, BLK), BLK)
        z = _dot_nt(qh, kn_ref[krows, :])
        l1p = jnp.log(1.0 + jnp.exp(-jnp.abs(z)))
        log_beta = jnp.minimum(z, 0.0) - l1p
        log_keep = -jnp.maximum(z, 0.0) - l1p
        if diag:
            log_keep = jnp.where(tri, log_keep, 0.0)
        cs = _split_dot(log_keep, suf_ref[...])
        w = jnp.exp(log_beta + cs[:, :BLK] + carry)
        if diag:
            w = jnp.where(tri, w, 0.0)
        acc = acc + _dot(w.astype(BF16), vb_ref[krows, :])
        return carry + cs[:, BLK:], acc

    def qblock(qi, c0):
        qrows = pl.ds(pl.multiple_of(qi * BLK, BLK), BLK)
        q2 = qn_ref[qrows, :]
        accs = []
        for h in range(2):
            hm = first if h == 0 else jnp.logical_not(first)
            qh = jnp.where(hm, q2, jnp.zeros_like(q2))
            zero = jnp.zeros((BLK, BLK), F32)
            carry, acc = block(qh, qi, zero, zero, True)

            def cond(s):
                return jnp.logical_and(s[0] >= 0, s[3])

            def body(s, qh=qh):
                kj, carry, acc, _ = s
                carry, acc = block(qh, kj, carry, acc, False)
                return kj - 1, carry, acc, jnp.max(carry) > -SB_SKIP

            _, _, acc, _ = lax.while_loop(cond, body, (qi - 1, carry, acc, jnp.max(carry) > -SB_SKIP))
            accs.append(acc)
        o = jnp.where(first, accs[0], accs[1])
        o_ref[0, qrows, :] = _pair_rms(o, og_ref[...], first)
        return c0

    lax.fori_loop(0, T // BLK, qblock, 0)


def _sb(sq, sk, sv, q_norm, k_norm, out_norm):
    B, T, _ = sq.shape
    BLK = SB_BLOCK
    j = np.arange(BLK)[:, None]
    s = np.arange(BLK)[None, :]
    suffix = np.concatenate([(j > s), np.ones((BLK, BLK), bool)], axis=1).astype(np.float32)
    pair = lambda g: jnp.tile(g, 2).reshape(1, 2 * SB_DH)
    consts = (pair(q_norm), pair(k_norm), pair(out_norm), jnp.asarray(suffix, BF16))
    spec = pl.BlockSpec((1, T, 2 * SB_DH), lambda b, p: (b, 0, p))
    return pl.pallas_call(
        _sb_kernel,
        out_shape=jax.ShapeDtypeStruct((B, T, SB_W), F32),
        grid=(B, SB_HEADS // 2),
        in_specs=[spec, spec, spec] + [_const_spec(cst.shape) for cst in consts],
        out_specs=spec,
        scratch_shapes=[pltpu.VMEM((T, 2 * SB_DH), BF16)] * 3,
        compiler_params=pltpu.CompilerParams(
            dimension_semantics=("parallel", "parallel"), vmem_limit_bytes=VMEM_LIMIT),
        name="stickbreak",
    )(sq, sk, sv, *consts)


def _conv_kernel(u_ref, w_ref, b_ref, lg_ref, lb_ref, o_ref, pad_ref):
    T = u_ref.shape[1]
    pad_ref[0:CONV_PAD, :] = jnp.zeros((CONV_PAD, CONV_CH), F32)
    pad_ref[CONV_PAD:, :] = u_ref[0]
    first_tap = CONV_PAD - (CONV_WIDTH - 1)
    for i in range(T // CONV_TILE):
        base = i * CONV_TILE + first_tap
        acc = jnp.zeros((CONV_TILE, CONV_CH), F32)
        for j in range(CONV_WIDTH):
            acc = acc + w_ref[j:j + 1, :] * pad_ref[base + j:base + j + CONV_TILE, :]
        acc = acc + b_ref[...]
        mu = jnp.mean(acc, axis=-1, keepdims=True)
        xc = acc - mu
        var = jnp.mean(xc * xc, axis=-1, keepdims=True)
        y = xc * lax.rsqrt(var + EPS) * lg_ref[...] + lb_ref[...]
        o_ref[0, i * CONV_TILE:(i + 1) * CONV_TILE, :] = _silu(y)


def _conv(cu, w, b, ln_g, ln_b):
    B, T, C = cu.shape
    spec = pl.BlockSpec((1, T, C), lambda bi: (bi, 0, 0))
    vec = lambda a: a.reshape(1, C)
    return pl.pallas_call(
        _conv_kernel,
        out_shape=jax.ShapeDtypeStruct((B, T, C), F32),
        grid=(B,),
        in_specs=[spec, _const_spec(w.shape), _const_spec((1, C)), _const_spec((1, C)), _const_spec((1, C))],
        out_specs=spec,
        scratch_shapes=[pltpu.VMEM((T + CONV_PAD, C), F32)],
        compiler_params=pltpu.CompilerParams(
            dimension_semantics=("parallel",), vmem_limit_bytes=VMEM_LIMIT),
        name="conv",
    )(cu, w, vec(b), vec(ln_g), vec(ln_b))


def _outproj_kernel(x_ref, a_ref, b_ref, c_ref, gt_ref, wa_ref, wb_ref, wc_ref, o_ref):
    y = (_dot(a_ref[...].astype(BF16), wa_ref[...]) + _dot(b_ref[...].astype(BF16), wb_ref[...])
         + _dot(c_ref[...].astype(BF16), wc_ref[...]))
    o_ref[...] = x_ref[...] + gt_ref[0] * y


def _outproj(x2, oa, ob, oc, gate, w_out, seq):
    N, D = x2.shape
    wb16 = w_out.astype(BF16)
    wa, wb, wc = wb16[:GLA_V], wb16[GLA_V:GLA_V + SB_W], wb16[GLA_V + SB_W:]
    tm = TOKEN_TILE
    tpb = seq // tm
    return pl.pallas_call(
        _outproj_kernel,
        out_shape=jax.ShapeDtypeStruct((N, D), F32),
        grid=(N // tm,),
        in_specs=[
            _row_spec(tm, D), _row_spec(tm, GLA_V), _row_spec(tm, SB_W), _row_spec(tm, CONV_CH),
            _batch_vec_spec(tpb, D),
            _const_spec(wa.shape), _const_spec(wb.shape), _const_spec(wc.shape),
        ],
        out_specs=_row_spec(tm, D),
        compiler_params=pltpu.CompilerParams(
            dimension_semantics=("parallel",), vmem_limit_bytes=VMEM_LIMIT),
        name="outproj",
    )(x2, oa, ob, oc, gate, wa, wb, wc)


def kernel(x, c, w_ada, b_ada, norm_ffn1, ffn1_w_in, ffn1_w_out, norm_mix, w_in, w_out, gla_w_gate_up, gla_b_gate, gla_out_norm, sb_q_norm, sb_k_norm, sb_out_norm, conv_w, conv_b, conv_ln_g, conv_ln_b, norm_ffn2, ffn2_w_in, ffn2_w_out):
    B, T, D = x.shape
    L = w_ada.shape[0]
    mod = _adaln(c, w_ada, b_ada).reshape(L, B, N_MOD, 1, D)
    x2 = x.reshape(B * T, D)
    for l in range(L):
        sh1, sc1, gt1, sh2, sc2, gt2, sh3, sc3, gt3 = (mod[l, :, i] for i in range(N_MOD))
        x2 = _ffn(x2, norm_ffn1[l], sh1, sc1, gt1, ffn1_w_in[l], ffn1_w_out[l], T)
        gq, gk, ga, gv, gg, sq, sk, sv, cu = _inproj(
            x2, norm_mix[l], sh2, sc2, w_in[l], gla_w_gate_up[l], gla_b_gate[l], T)
        seq = lambda a: a.reshape(B, T, a.shape[-1])
        oa = _gla(seq(gq), seq(gk), seq(ga), seq(gv), seq(gg), gla_out_norm[l])
        ob = _sb(seq(sq), seq(sk), seq(sv), sb_q_norm[l], sb_k_norm[l], sb_out_norm[l])
        oc = _conv(seq(cu), conv_w[l], conv_b[l], conv_ln_g[l], conv_ln_b[l])
        x2 = _outproj(x2, oa.reshape(B * T, GLA_V), ob.reshape(B * T, SB_W), oc.reshape(B * T, CONV_CH),
                      gt2, w_out[l], T)
        x2 = _ffn(x2, norm_ffn2[l], sh3, sc3, gt3, ffn2_w_in[l], ffn2_w_out[l], T)
    return x2.reshape(B, T, D)
```

```python
import functools

import numpy as np
import jax
import jax.numpy as jnp
from jax import lax
from jax.experimental import pallas as pl
from jax.experimental.pallas import tpu as pltpu

F32 = jnp.float32
BF16 = jnp.bfloat16

EPS = 1e-6
CHUNK = 64
GLA_HEADS, GLA_DK, GLA_DV, GLA_RANK, GLA_TAU = 4, 48, 96, 16, 16.0
SB_HEADS, SB_DH = 6, 64
CONV_CH, CONV_WIDTH = 256, 31
D_FF = 2816
N_MOD = 9
GLA_QK = GLA_HEADS * GLA_DK
GLA_V = GLA_HEADS * GLA_DV
SB_W = SB_HEADS * SB_DH

LANES = 128
QK_PAD = 256
RANK_PAD = 128
FF_CHUNK = 256
TOKEN_TILE = 512
SB_BLOCK = 128
SB_WINDOW = 2
SB_SKIP = 104.0
GLA_TILE = 512
CONV_TILE = 256
CONV_PAD = 32
VMEM_LIMIT = 52 * 1024 * 1024

_OFF_GQ, _OFF_GK, _OFF_GV, _OFF_GG, _OFF_GR = 0, 256, 512, 896, 1280
_OFF_SQ, _OFF_SK, _OFF_SV, _OFF_CA, _OFF_CG = 1408, 1792, 2176, 2560, 2816
PROJ_W = 3072


def _dot(a, b):
    return jnp.dot(a, b, preferred_element_type=F32)


def _dot_nt(a, b):
    return lax.dot_general(a, b, (((1,), (1,)), ((), ())), preferred_element_type=F32)


def _dot_tn(a, b):
    return lax.dot_general(a, b, (((0,), (0,)), ((), ())), preferred_element_type=F32)


def _split_dot(x, m):
    hi = x.astype(BF16)
    lo = (x - hi.astype(F32)).astype(BF16)
    return _dot(hi, m) + _dot(lo, m)


def _split_dot_left(m, x):
    hi = x.astype(BF16)
    lo = (x - hi.astype(F32)).astype(BF16)
    return _dot(m, hi) + _dot(m, lo)


def _sigmoid(x):
    return 1.0 / (1.0 + jnp.exp(-x))


def _silu(x):
    return x * _sigmoid(x)


def _log_sigmoid(x):
    return jnp.minimum(x, 0.0) - jnp.log(1.0 + jnp.exp(-jnp.abs(x)))


def _modulated_norm(x, g, scale, shift):
    ms = jnp.mean(x * x, axis=-1, keepdims=True)
    return (x * lax.rsqrt(ms + EPS) * g) * (1.0 + scale) + shift


def _adaln_kernel(c_ref, w_ref, b_ref, o_ref):
    ca = _silu(c_ref[...]).astype(BF16)
    o_ref[0] = _dot(ca, w_ref[0].astype(BF16)) + b_ref[0]


def _adaln(c, w_ada, b_ada):
    L, D, W = w_ada.shape
    B = c.shape[0]
    tn = 1536
    return pl.pallas_call(
        _adaln_kernel,
        out_shape=jax.ShapeDtypeStruct((L, B, W), F32),
        grid=(L, W // tn),
        in_specs=[
            pl.BlockSpec((B, D), lambda l, j: (0, 0)),
            pl.BlockSpec((1, D, tn), lambda l, j: (l, 0, j)),
            pl.BlockSpec((1, 1, tn), lambda l, j: (l, 0, j)),
        ],
        out_specs=pl.BlockSpec((1, B, tn), lambda l, j: (l, 0, j)),
        compiler_params=pltpu.CompilerParams(
            dimension_semantics=("parallel", "parallel"), vmem_limit_bytes=VMEM_LIMIT),
        name="adaln",
    )(c, w_ada, b_ada.reshape(L, 1, W))


def _ffn_kernel(x_ref, g_ref, sh_ref, sc_ref, gt_ref, win_ref, wout_ref, o_ref, acc_ref):
    x = x_ref[...]
    hb = _modulated_norm(x, g_ref[...], sc_ref[0], sh_ref[0]).astype(BF16)
    n = wout_ref.shape[0]
    for j in range(n):
        a = _dot(hb, win_ref[j])
        b = _dot(hb, win_ref[n + j])
        y = _dot((_silu(a) * b).astype(BF16), wout_ref[j])
        if j == 0:
            acc_ref[...] = y
        else:
            acc_ref[...] += y
    o_ref[...] = x + (0.5 * gt_ref[0]) * acc_ref[...]


def _const_spec(shape):
    nd = len(shape)
    return pl.BlockSpec(shape, lambda *_: (0,) * nd, pipeline_mode=pl.Buffered(1))


def _row_spec(tm, w):
    return pl.BlockSpec((tm, w), lambda i: (i, 0))


def _batch_vec_spec(tiles_per_batch, w):
    return pl.BlockSpec((1, 1, w), lambda i: (i // tiles_per_batch, 0, 0))


def _ffn(x2, g, shift, scale, gate, w_in, w_out, seq):
    N, D = x2.shape
    nchunk = D_FF // FF_CHUNK
    win = w_in.reshape(D, 2 * nchunk, FF_CHUNK).transpose(1, 0, 2).astype(BF16)
    wout = w_out.reshape(nchunk, FF_CHUNK, D).astype(BF16)
    tm = TOKEN_TILE
    tpb = seq // tm
    return pl.pallas_call(
        _ffn_kernel,
        out_shape=jax.ShapeDtypeStruct((N, D), F32),
        grid=(N // tm,),
        in_specs=[
            _row_spec(tm, D),
            _const_spec((1, D)),
            _batch_vec_spec(tpb, D), _batch_vec_spec(tpb, D), _batch_vec_spec(tpb, D),
            _const_spec(win.shape), _const_spec(wout.shape),
        ],
        out_specs=_row_spec(tm, D),
        scratch_shapes=[pltpu.VMEM((tm, D), F32)],
        compiler_params=pltpu.CompilerParams(
            dimension_semantics=("parallel",), vmem_limit_bytes=VMEM_LIMIT),
        name="ffn",
    )(x2, g.reshape(1, D), shift, scale, gate, win, wout)


def _inproj_kernel(x_ref, g_ref, sh_ref, sc_ref, w_ref, wup_ref, bg_ref,
                   gq_ref, gk_ref, ga_ref, gv_ref, gg_ref, sq_ref, sk_ref, sv_ref, cu_ref):
    hb = _modulated_norm(x_ref[...], g_ref[...], sc_ref[0], sh_ref[0]).astype(BF16)
    p = _dot(hb, w_ref[...])
    gq_ref[...] = p[:, _OFF_GQ:_OFF_GQ + QK_PAD]
    gk_ref[...] = p[:, _OFF_GK:_OFF_GK + QK_PAD]
    gv_ref[...] = p[:, _OFF_GV:_OFF_GV + GLA_V]
    gg_ref[...] = p[:, _OFF_GG:_OFF_GG + GLA_V]
    r = p[:, _OFF_GR:_OFF_GR + RANK_PAD].astype(BF16)
    ga_ref[...] = _log_sigmoid(_dot(r, wup_ref[...]) + bg_ref[...]) * (1.0 / GLA_TAU)
    sq_ref[...] = p[:, _OFF_SQ:_OFF_SQ + SB_W]
    sk_ref[...] = p[:, _OFF_SK:_OFF_SK + SB_W]
    sv_ref[...] = p[:, _OFF_SV:_OFF_SV + SB_W]
    cu_ref[...] = p[:, _OFF_CA:_OFF_CA + CONV_CH] * _sigmoid(p[:, _OFF_CG:_OFF_CG + CONV_CH])


def _pad_cols(w, width):
    return jnp.pad(w, ((0, 0), (0, width - w.shape[1])))


def _inproj(x2, g, shift, scale, w_in, w_up, b_gate, seq):
    N, D = x2.shape
    o = 0
    parts = []
    for width, pad in ((GLA_QK, QK_PAD), (GLA_QK, QK_PAD), (GLA_V, GLA_V), (GLA_V, GLA_V),
                       (GLA_RANK, RANK_PAD), (SB_W, SB_W), (SB_W, SB_W), (SB_W, SB_W),
                       (CONV_CH, CONV_CH), (CONV_CH, CONV_CH)):
        parts.append(_pad_cols(w_in[:, o:o + width], pad))
        o += width
    w = jnp.concatenate(parts, axis=1).astype(BF16)
    wup = jnp.pad(w_up, ((0, RANK_PAD - GLA_RANK), (0, QK_PAD - GLA_QK))).astype(BF16)
    bg = jnp.pad(b_gate, (0, QK_PAD - GLA_QK)).reshape(1, QK_PAD)
    tm = TOKEN_TILE
    tpb = seq // tm
    widths = (QK_PAD, QK_PAD, QK_PAD, GLA_V, GLA_V, SB_W, SB_W, SB_W, CONV_CH)
    return pl.pallas_call(
        _inproj_kernel,
        out_shape=[jax.ShapeDtypeStruct((N, wd), F32) for wd in widths],
        grid=(N // tm,),
        in_specs=[
            _row_spec(tm, D),
            _const_spec((1, D)),
            _batch_vec_spec(tpb, D), _batch_vec_spec(tpb, D),
            _const_spec(w.shape), _const_spec(wup.shape), _const_spec(bg.shape),
        ],
        out_specs=[_row_spec(tm, wd) for wd in widths],
        compiler_params=pltpu.CompilerParams(
            dimension_semantics=("parallel",), vmem_limit_bytes=VMEM_LIMIT),
        name="inproj",
    )(x2, g.reshape(1, D), shift, scale, w, wup, bg)


_GLA_LEVELS = 6


def _gla_constants():
    C = CHUNK
    t = np.arange(C)[:, None]
    j = np.arange(C)[None, :]
    mats = [j <= t, j > t]
    masks = [t == j]
    for l in range(1, _GLA_LEVELS + 1):
        n, m = 1 << l, 1 << (l - 1)
        ref = t // n * n + m - 1
        right = (t % n) >= m
        mats.append(right & (j > ref) & (j <= t))
        mats.append((~right) & (j > t) & (j <= ref))
        masks.append(((t // n) == (j // n)) & right & ((j % n) < m))
    prefix = np.concatenate(mats, axis=0).astype(np.float32)
    masks = np.stack([np.tile(mk, (1, GLA_HEADS)) for mk in masks]).astype(np.float32)
    kl = np.arange(QK_PAD)[None, :] // GLA_DK
    vl = np.arange(GLA_V)[:, None] // GLA_DV
    state_mask = (kl == vl).astype(np.float32)
    seg = (np.arange(GLA_V)[:, None] // GLA_DV == np.arange(GLA_V)[None, :] // GLA_DV)
    return prefix, masks, state_mask, seg.astype(np.float32)


def _gla_kernel(q_ref, k_ref, a_ref, v_ref, g_ref, pre_ref, msk_ref, smask_ref, seg_ref, on_ref,
                o_ref, st_ref):
    C = CHUNK

    @pl.when(pl.program_id(1) == 0)
    def _():
        st_ref[...] = jnp.zeros_like(st_ref)

    klane = lax.broadcasted_iota(jnp.int32, (1, QK_PAD), 1) // GLA_DK
    vlane = lax.broadcasted_iota(jnp.int32, (1, GLA_V), 1) // GLA_DV
    pre = pre_ref[...]

    def chunk(c, carry):
        rows = pl.ds(pl.multiple_of(c * C, C), C)
        q = q_ref[0, rows, :] * (GLA_DK ** -0.5)
        k = k_ref[0, rows, :]
        v = v_ref[0, rows, :]
        e = _split_dot_left(pre, a_ref[0, rows, :])
        vb = v.astype(BF16)

        st = st_ref[...]
        o = _dot_nt((q * jnp.exp(e[0:C])).astype(BF16), st.astype(BF16))

        att = jnp.zeros((C, GLA_HEADS * C), F32)
        for l in range(_GLA_LEVELS + 1):
            if l == 0:
                qt, kt = q, k
            else:
                base = (2 * l) * C
                qt = q * jnp.exp(e[base:base + C])
                kt = k * jnp.exp(e[base + C:base + 2 * C])
            kst = jnp.concatenate(
                [jnp.where(klane == h, kt, 0.0) for h in range(GLA_HEADS)], axis=0).astype(BF16)
            att = att + _dot_nt(qt.astype(BF16), kst) * msk_ref[l]
        vst = jnp.concatenate(
            [jnp.where(vlane == h, v, 0.0) for h in range(GLA_HEADS)], axis=0).astype(BF16)
        o = o + _dot(att.astype(BF16), vst)

        kdec = (k * jnp.exp(e[C:2 * C])).astype(BF16)
        upd = _dot_tn(vb, kdec) * smask_ref[...]
        st_ref[...] = st * jnp.exp(e[C - 1:C]) + upd

        ssq = _split_dot(o * o, seg_ref[...])
        y = o * lax.rsqrt(ssq * (1.0 / GLA_DV) + EPS) * on_ref[...]
        o_ref[0, rows, :] = y * _silu(g_ref[0, rows, :])
        return carry

    lax.fori_loop(0, q_ref.shape[1] // C, chunk, 0)


def _gla(gq, gk, ga, gv, gg, out_norm):
    B, T, _ = gq.shape
    prefix, masks, state_mask, seg = _gla_constants()
    tt = GLA_TILE
    seq_spec = lambda w: pl.BlockSpec((1, tt, w), lambda b, i: (b, i, 0))
    consts = (jnp.asarray(prefix, BF16), jnp.asarray(masks), jnp.asarray(state_mask),
              jnp.asarray(seg, BF16), jnp.tile(out_norm, GLA_HEADS).reshape(1, GLA_V))
    return pl.pallas_call(
        _gla_kernel,
        out_shape=jax.ShapeDtypeStruct((B, T, GLA_V), F32),
        grid=(B, T // tt),
        in_specs=[seq_spec(QK_PAD), seq_spec(QK_PAD), seq_spec(QK_PAD), seq_spec(GLA_V), seq_spec(GLA_V)]
                 + [_const_spec(cst.shape) for cst in consts],
        out_specs=seq_spec(GLA_V),
        scratch_shapes=[pltpu.VMEM((GLA_V, QK_PAD), F32)],
        compiler_params=pltpu.CompilerParams(
            dimension_semantics=("parallel", "arbitrary"), vmem_limit_bytes=VMEM_LIMIT),
        name="gla",
    )(gq, gk, ga, gv, gg, *consts)


def _pair_rms(x, gain, first):
    x2 = x * x
    s0 = jnp.sum(jnp.where(first, x2, 0.0), axis=-1, keepdims=True)
    s1 = jnp.sum(jnp.where(first, 0.0, x2), axis=-1, keepdims=True)
    ms = jnp.where(first, s0, s1) * (1.0 / SB_DH)
    return x * lax.rsqrt(ms + EPS) * gain


def _sb_kernel(q_ref, k_ref, v_ref, qg_ref, kg_ref, og_ref, wsuf_ref, wmask_ref, suf_ref, o_ref,
               qn_ref, kn_ref, vb_ref):
    T = q_ref.shape[1]
    BLK = SB_BLOCK
    LEAD = SB_WINDOW * BLK
    first = lax.broadcasted_iota(jnp.int32, (1, 2 * SB_DH), 1) < SB_DH

    kn_ref[0:LEAD, :] = jnp.zeros((LEAD, 2 * SB_DH), BF16)
    vb_ref[0:LEAD, :] = jnp.zeros((LEAD, 2 * SB_DH), BF16)

    def prep(i, carry):
        rows = pl.ds(pl.multiple_of(i * BLK, BLK), BLK)
        dst = pl.ds(pl.multiple_of(i * BLK + LEAD, BLK), BLK)
        qn_ref[rows, :] = (_pair_rms(q_ref[0, rows, :], qg_ref[...], first) * (SB_DH ** -0.5)).astype(BF16)
        kn_ref[dst, :] = _pair_rms(k_ref[0, rows, :], kg_ref[...], first).astype(BF16)
        vb_ref[dst, :] = v_ref[0, rows, :].astype(BF16)
        return carry

    lax.fori_loop(0, T // BLK, prep, 0)

    def log_terms(z):
        l1p = jnp.log(1.0 + jnp.exp(-jnp.abs(z)))
        return jnp.minimum(z, 0.0) - l1p, -jnp.maximum(z, 0.0) - l1p

    def qblock(qi, c0):
        qrows = pl.ds(pl.multiple_of(qi * BLK, BLK), BLK)
        q2 = qn_ref[qrows, :]
        zq = jnp.zeros_like(q2)
        qs = jnp.concatenate([jnp.where(first, q2, zq), jnp.where(first, zq, q2)], axis=0)

        slab = pl.ds(pl.multiple_of(qi * BLK, BLK), LEAD + BLK)
        valid = wmask_ref[jnp.minimum(qi, SB_WINDOW)]
        log_beta, log_keep = log_terms(_dot_nt(qs, kn_ref[slab, :]))
        log_keep = log_keep * valid
        suffix = _split_dot(log_keep, wsuf_ref[...])
        w = jnp.exp(log_beta + suffix) * valid
        acc = _dot(w.astype(BF16), vb_ref[slab, :])
        total = suffix[:, 0:1] + log_keep[:, 0:1]

        def cond(s):
            return jnp.logical_and(s[0] >= 0, s[3])

        def body(s):
            kj, carry, acc, _ = s
            krows = pl.ds(pl.multiple_of(kj * BLK + LEAD, BLK), BLK)
            log_beta, log_keep = log_terms(_dot_nt(qs, kn_ref[krows, :]))
            cs = _split_dot(log_keep, suf_ref[...])
            w = jnp.exp(log_beta + cs[:, :BLK] + carry)
            acc = acc + _dot(w.astype(BF16), vb_ref[krows, :])
            carry = carry + cs[:, BLK:]
            return kj - 1, carry, acc, jnp.max(carry) > -SB_SKIP

        carry = jnp.broadcast_to(total, (2 * BLK, BLK))
        _, _, acc, _ = lax.while_loop(
            cond, body, (qi - SB_WINDOW - 1, carry, acc, jnp.max(total) > -SB_SKIP))
        o = jnp.where(first, acc[:BLK], acc[BLK:])
        o_ref[0, qrows, :] = _pair_rms(o, og_ref[...], first)
        return c0

    lax.fori_loop(0, T // BLK, qblock, 0)


def _sb_constants():
    BLK, LEAD = SB_BLOCK, SB_WINDOW * SB_BLOCK
    W = LEAD + BLK
    j = np.arange(W)[:, None]
    s = np.arange(W)[None, :]
    wide_suffix = (j > s).astype(np.float32)
    t = np.tile(np.arange(BLK), 2)[:, None]
    masks = []
    for qi in range(SB_WINDOW + 1):
        pos = s - LEAD
        masks.append((pos < t) & (pos + qi * BLK >= 0))
    jb = np.arange(BLK)[:, None]
    sb = np.arange(BLK)[None, :]
    suffix = np.concatenate([(jb > sb), np.ones((BLK, BLK), bool)], axis=1).astype(np.float32)
    return wide_suffix, np.stack(masks).astype(np.float32), suffix


def _sb(sq, sk, sv, q_norm, k_norm, out_norm):
    B, T, _ = sq.shape
    wide_suffix, masks, suffix = _sb_constants()
    pair = lambda g: jnp.tile(g, 2).reshape(1, 2 * SB_DH)
    consts = (pair(q_norm), pair(k_norm), pair(out_norm), jnp.asarray(wide_suffix, BF16),
              jnp.asarray(masks), jnp.asarray(suffix, BF16))
    spec = pl.BlockSpec((1, T, 2 * SB_DH), lambda b, p: (b, 0, p))
    lead = SB_WINDOW * SB_BLOCK
    return pl.pallas_call(
        _sb_kernel,
        out_shape=jax.ShapeDtypeStruct((B, T, SB_W), F32),
        grid=(B, SB_HEADS // 2),
        in_specs=[spec, spec, spec] + [_const_spec(cst.shape) for cst in consts],
        out_specs=spec,
        scratch_shapes=[pltpu.VMEM((T, 2 * SB_DH), BF16), pltpu.VMEM((T + lead, 2 * SB_DH), BF16),
                        pltpu.VMEM((T + lead, 2 * SB_DH), BF16)],
        compiler_params=pltpu.CompilerParams(
            dimension_semantics=("parallel", "parallel"), vmem_limit_bytes=VMEM_LIMIT),
        name="stickbreak",
    )(sq, sk, sv, *consts)


def _conv_kernel(u_ref, w_ref, b_ref, lg_ref, lb_ref, o_ref, pad_ref):
    T = u_ref.shape[1]
    pad_ref[0:CONV_PAD, :] = jnp.zeros((CONV_PAD, CONV_CH), F32)
    pad_ref[CONV_PAD:, :] = u_ref[0]
    first_tap = CONV_PAD - (CONV_WIDTH - 1)
    for i in range(T // CONV_TILE):
        base = i * CONV_TILE + first_tap
        acc = jnp.zeros((CONV_TILE, CONV_CH), F32)
        for j in range(CONV_WIDTH):
            acc = acc + w_ref[j:j + 1, :] * pad_ref[base + j:base + j + CONV_TILE, :]
        acc = acc + b_ref[...]
        mu = jnp.mean(acc, axis=-1, keepdims=True)
        xc = acc - mu
        var = jnp.mean(xc * xc, axis=-1, keepdims=True)
        y = xc * lax.rsqrt(var + EPS) * lg_ref[...] + lb_ref[...]
        o_ref[0, i * CONV_TILE:(i + 1) * CONV_TILE, :] = _silu(y)


def _conv(cu, w, b, ln_g, ln_b):
    B, T, C = cu.shape
    spec = pl.BlockSpec((1, T, C), lambda bi: (bi, 0, 0))
    vec = lambda a: a.reshape(1, C)
    return pl.pallas_call(
        _conv_kernel,
        out_shape=jax.ShapeDtypeStruct((B, T, C), F32),
        grid=(B,),
        in_specs=[spec, _const_spec(w.shape), _const_spec((1, C)), _const_spec((1, C)), _const_spec((1, C))],
        out_specs=spec,
        scratch_shapes=[pltpu.VMEM((T + CONV_PAD, C), F32)],
        compiler_params=pltpu.CompilerParams(
            dimension_semantics=("parallel",), vmem_limit_bytes=VMEM_LIMIT),
        name="conv",
    )(cu, w, vec(b), vec(ln_g), vec(ln_b))


def _outproj_kernel(x_ref, a_ref, b_ref, c_ref, gt_ref, wa_ref, wb_ref, wc_ref, o_ref):
    y = (_dot(a_ref[...].astype(BF16), wa_ref[...]) + _dot(b_ref[...].astype(BF16), wb_ref[...])
         + _dot(c_ref[...].astype(BF16), wc_ref[...]))
    o_ref[...] = x_ref[...] + gt_ref[0] * y


def _outproj(x2, oa, ob, oc, gate, w_out, seq):
    N, D = x2.shape
    wb16 = w_out.astype(BF16)
    wa, wb, wc = wb16[:GLA_V], wb16[GLA_V:GLA_V + SB_W], wb16[GLA_V + SB_W:]
    tm = TOKEN_TILE
    tpb = seq // tm
    return pl.pallas_call(
        _outproj_kernel,
        out_shape=jax.ShapeDtypeStruct((N, D), F32),
        grid=(N // tm,),
        in_specs=[
            _row_spec(tm, D), _row_spec(tm, GLA_V), _row_spec(tm, SB_W), _row_spec(tm, CONV_CH),
            _batch_vec_spec(tpb, D),
            _const_spec(wa.shape), _const_spec(wb.shape), _const_spec(wc.shape),
        ],
        out_specs=_row_spec(tm, D),
        compiler_params=pltpu.CompilerParams(
            dimension_semantics=("parallel",), vmem_limit_bytes=VMEM_LIMIT),
        name="outproj",
    )(x2, oa, ob, oc, gate, wa, wb, wc)


def kernel(x, c, w_ada, b_ada, norm_ffn1, ffn1_w_in, ffn1_w_out, norm_mix, w_in, w_out, gla_w_gate_up, gla_b_gate, gla_out_norm, sb_q_norm, sb_k_norm, sb_out_norm, conv_w, conv_b, conv_ln_g, conv_ln_b, norm_ffn2, ffn2_w_in, ffn2_w_out):
    B, T, D = x.shape
    L = w_ada.shape[0]
    mod = _adaln(c, w_ada, b_ada).reshape(L, B, N_MOD, 1, D)
    x2 = x.reshape(B * T, D)
    for l in range(L):
        sh1, sc1, gt1, sh2, sc2, gt2, sh3, sc3, gt3 = (mod[l, :, i] for i in range(N_MOD))
        x2 = _ffn(x2, norm_ffn1[l], sh1, sc1, gt1, ffn1_w_in[l], ffn1_w_out[l], T)
        gq, gk, ga, gv, gg, sq, sk, sv, cu = _inproj(
            x2, norm_mix[l], sh2, sc2, w_in[l], gla_w_gate_up[l], gla_b_gate[l], T)
        seq = lambda a: a.reshape(B, T, a.shape[-1])
        oa = _gla(seq(gq), seq(gk), seq(ga), seq(gv), seq(gg), gla_out_norm[l])
        ob = _sb(seq(sq), seq(sk), seq(sv), sb_q_norm[l], sb_k_norm[l], sb_out_norm[l])
        oc = _conv(seq(cu), conv_w[l], conv_b[l], conv_ln_g[l], conv_ln_b[l])
        x2 = _outproj(x2, oa.reshape(B * T, GLA_V), ob.reshape(B * T, SB_W), oc.reshape(B * T, CONV_CH),
                      gt2, w_out[l], T)
        x2 = _ffn(x2, norm_ffn2[l], sh3, sc3, gt3, ffn2_w_in[l], ffn2_w_out[l], T)
    return x2.reshape(B, T, D)
```

```python
import functools

import numpy as np
import jax
import jax.numpy as jnp
from jax import lax
from jax.experimental import pallas as pl
from jax.experimental.pallas import tpu as pltpu

F32 = jnp.float32
BF16 = jnp.bfloat16

EPS = 1e-6
CHUNK = 64
GLA_HEADS, GLA_DK, GLA_DV, GLA_RANK, GLA_TAU = 4, 48, 96, 16, 16.0
SB_HEADS, SB_DH = 6, 64
CONV_CH, CONV_WIDTH = 256, 31
D_FF = 2816
N_MOD = 9
GLA_QK = GLA_HEADS * GLA_DK
GLA_V = GLA_HEADS * GLA_DV
SB_W = SB_HEADS * SB_DH

LANES = 128
QK_PAD = 256
RANK_PAD = 128
FF_CHUNK = 256
TOKEN_TILE = 512
SB_BLOCK = 128
SB_ALIGN = 64
SB_LEAD = 192
SB_GROUP = 2
SB_SKIP = 104.0
GLA_TILE = 512
CONV_TILE = 256
CONV_PAD = 32
VMEM_LIMIT = 52 * 1024 * 1024

_OFF_GQ, _OFF_GK, _OFF_GV, _OFF_GG, _OFF_GR = 0, 256, 512, 896, 1280
_OFF_SQ, _OFF_SK, _OFF_SV, _OFF_CA, _OFF_CG = 1408, 1792, 2176, 2560, 2816
PROJ_W = 3072


def _dot(a, b):
    return jnp.dot(a, b, preferred_element_type=F32)


def _dot_nt(a, b):
    return lax.dot_general(a, b, (((1,), (1,)), ((), ())), preferred_element_type=F32)


def _dot_tn(a, b):
    return lax.dot_general(a, b, (((0,), (0,)), ((), ())), preferred_element_type=F32)


def _split_dot(x, m):
    hi = x.astype(BF16)
    lo = (x - hi.astype(F32)).astype(BF16)
    return _dot(hi, m) + _dot(lo, m)


def _split_dot_left(m, x):
    hi = x.astype(BF16)
    lo = (x - hi.astype(F32)).astype(BF16)
    return _dot(m, hi) + _dot(m, lo)


def _sigmoid(x):
    return 1.0 / (1.0 + jnp.exp(-x))


def _silu(x):
    return x * _sigmoid(x)


def _log_sigmoid(x):
    return jnp.minimum(x, 0.0) - jnp.log(1.0 + jnp.exp(-jnp.abs(x)))


def _modulated_norm(x, g, scale, shift):
    ms = jnp.mean(x * x, axis=-1, keepdims=True)
    return (x * lax.rsqrt(ms + EPS) * g) * (1.0 + scale) + shift


def _adaln_kernel(c_ref, w_ref, b_ref, o_ref):
    ca = _silu(c_ref[...]).astype(BF16)
    o_ref[0] = _dot(ca, w_ref[0].astype(BF16)) + b_ref[0]


def _adaln(c, w_ada, b_ada):
    L, D, W = w_ada.shape
    B = c.shape[0]
    tn = 1536
    return pl.pallas_call(
        _adaln_kernel,
        out_shape=jax.ShapeDtypeStruct((L, B, W), F32),
        grid=(L, W // tn),
        in_specs=[
            pl.BlockSpec((B, D), lambda l, j: (0, 0)),
            pl.BlockSpec((1, D, tn), lambda l, j: (l, 0, j)),
            pl.BlockSpec((1, 1, tn), lambda l, j: (l, 0, j)),
        ],
        out_specs=pl.BlockSpec((1, B, tn), lambda l, j: (l, 0, j)),
        compiler_params=pltpu.CompilerParams(
            dimension_semantics=("parallel", "parallel"), vmem_limit_bytes=VMEM_LIMIT),
        name="adaln",
    )(c, w_ada, b_ada.reshape(L, 1, W))


def _ffn_kernel(x_ref, g_ref, sh_ref, sc_ref, gt_ref, win_ref, wout_ref, o_ref, acc_ref):
    x = x_ref[...]
    hb = _modulated_norm(x, g_ref[...], sc_ref[0], sh_ref[0]).astype(BF16)
    n = wout_ref.shape[0]
    for j in range(n):
        a = _dot(hb, win_ref[j])
        b = _dot(hb, win_ref[n + j])
        y = _dot((_silu(a) * b).astype(BF16), wout_ref[j])
        if j == 0:
            acc_ref[...] = y
        else:
            acc_ref[...] += y
    o_ref[...] = x + (0.5 * gt_ref[0]) * acc_ref[...]


def _const_spec(shape):
    nd = len(shape)
    return pl.BlockSpec(shape, lambda *_: (0,) * nd, pipeline_mode=pl.Buffered(1))


def _row_spec(tm, w):
    return pl.BlockSpec((tm, w), lambda i: (i, 0))


def _batch_vec_spec(tiles_per_batch, w):
    return pl.BlockSpec((1, 1, w), lambda i: (i // tiles_per_batch, 0, 0))


def _ffn(x2, g, shift, scale, gate, w_in, w_out, seq):
    N, D = x2.shape
    nchunk = D_FF // FF_CHUNK
    win = w_in.reshape(D, 2 * nchunk, FF_CHUNK).transpose(1, 0, 2).astype(BF16)
    wout = w_out.reshape(nchunk, FF_CHUNK, D).astype(BF16)
    tm = TOKEN_TILE
    tpb = seq // tm
    return pl.pallas_call(
        _ffn_kernel,
        out_shape=jax.ShapeDtypeStruct((N, D), F32),
        grid=(N // tm,),
        in_specs=[
            _row_spec(tm, D),
            _const_spec((1, D)),
            _batch_vec_spec(tpb, D), _batch_vec_spec(tpb, D), _batch_vec_spec(tpb, D),
            _const_spec(win.shape), _const_spec(wout.shape),
        ],
        out_specs=_row_spec(tm, D),
        scratch_shapes=[pltpu.VMEM((tm, D), F32)],
        compiler_params=pltpu.CompilerParams(
            dimension_semantics=("parallel",), vmem_limit_bytes=VMEM_LIMIT),
        name="ffn",
    )(x2, g.reshape(1, D), shift, scale, gate, win, wout)


def _head_rms(x, seg, gain, head_dim):
    ms = _split_dot(x * x, seg) * (1.0 / head_dim)
    return x * lax.rsqrt(ms + EPS) * gain


def _inproj_kernel(x_ref, g_ref, sh_ref, sc_ref, w_ref, wup_ref, bg_ref, seg_ref, qn_ref, kn_ref,
                   gq_ref, gk_ref, ga_ref, gv_ref, gg_ref, sq_ref, sk_ref, sv_ref, cu_ref):
    hb = _modulated_norm(x_ref[...], g_ref[...], sc_ref[0], sh_ref[0]).astype(BF16)
    p = _dot(hb, w_ref[...])
    gq_ref[...] = p[:, _OFF_GQ:_OFF_GQ + QK_PAD]
    gk_ref[...] = p[:, _OFF_GK:_OFF_GK + QK_PAD]
    gv_ref[...] = p[:, _OFF_GV:_OFF_GV + GLA_V]
    gg_ref[...] = p[:, _OFF_GG:_OFF_GG + GLA_V]
    r = p[:, _OFF_GR:_OFF_GR + RANK_PAD].astype(BF16)
    ga_ref[...] = _log_sigmoid(_dot(r, wup_ref[...]) + bg_ref[...]) * (1.0 / GLA_TAU)
    seg = seg_ref[...]
    sq = _head_rms(p[:, _OFF_SQ:_OFF_SQ + SB_W], seg, qn_ref[...], SB_DH)
    sq_ref[...] = (sq * (SB_DH ** -0.5)).astype(BF16)
    sk_ref[...] = _head_rms(p[:, _OFF_SK:_OFF_SK + SB_W], seg, kn_ref[...], SB_DH).astype(BF16)
    sv_ref[...] = p[:, _OFF_SV:_OFF_SV + SB_W].astype(BF16)
    cu_ref[...] = p[:, _OFF_CA:_OFF_CA + CONV_CH] * _sigmoid(p[:, _OFF_CG:_OFF_CG + CONV_CH])


def _same_head(width, head_dim):
    h = np.arange(width) // head_dim
    return jnp.asarray((h[:, None] == h[None, :]).astype(np.float32), BF16)


def _pad_cols(w, width):
    return jnp.pad(w, ((0, 0), (0, width - w.shape[1])))


def _inproj(x2, g, shift, scale, w_in, w_up, b_gate, q_norm, k_norm, seq):
    N, D = x2.shape
    o = 0
    parts = []
    for width, pad in ((GLA_QK, QK_PAD), (GLA_QK, QK_PAD), (GLA_V, GLA_V), (GLA_V, GLA_V),
                       (GLA_RANK, RANK_PAD), (SB_W, SB_W), (SB_W, SB_W), (SB_W, SB_W),
                       (CONV_CH, CONV_CH), (CONV_CH, CONV_CH)):
        parts.append(_pad_cols(w_in[:, o:o + width], pad))
        o += width
    w = jnp.concatenate(parts, axis=1).astype(BF16)
    wup = jnp.pad(w_up, ((0, RANK_PAD - GLA_RANK), (0, QK_PAD - GLA_QK))).astype(BF16)
    bg = jnp.pad(b_gate, (0, QK_PAD - GLA_QK)).reshape(1, QK_PAD)
    tm = TOKEN_TILE
    tpb = seq // tm
    seg = _same_head(SB_W, SB_DH)
    heads = lambda gn: jnp.tile(gn, SB_HEADS).reshape(1, SB_W)
    outs = ((QK_PAD, F32), (QK_PAD, F32), (QK_PAD, F32), (GLA_V, F32), (GLA_V, F32),
            (SB_W, BF16), (SB_W, BF16), (SB_W, BF16), (CONV_CH, F32))
    return pl.pallas_call(
        _inproj_kernel,
        out_shape=[jax.ShapeDtypeStruct((N, wd), dt) for wd, dt in outs],
        grid=(N // tm,),
        in_specs=[
            _row_spec(tm, D),
            _const_spec((1, D)),
            _batch_vec_spec(tpb, D), _batch_vec_spec(tpb, D),
            _const_spec(w.shape), _const_spec(wup.shape), _const_spec(bg.shape),
            _const_spec(seg.shape), _const_spec((1, SB_W)), _const_spec((1, SB_W)),
        ],
        out_specs=[_row_spec(tm, wd) for wd, _ in outs],
        compiler_params=pltpu.CompilerParams(
            dimension_semantics=("parallel",), vmem_limit_bytes=VMEM_LIMIT),
        name="inproj",
    )(x2, g.reshape(1, D), shift, scale, w, wup, bg, seg, heads(q_norm), heads(k_norm))


_GLA_LEVELS = 6


def _gla_constants():
    C = CHUNK
    t = np.arange(C)[:, None]
    j = np.arange(C)[None, :]
    mats = [j <= t, j > t]
    masks = [t == j]
    for l in range(1, _GLA_LEVELS + 1):
        n, m = 1 << l, 1 << (l - 1)
        ref = t // n * n + m - 1
        right = (t % n) >= m
        mats.append(right & (j > ref) & (j <= t))
        mats.append((~right) & (j > t) & (j <= ref))
        masks.append(((t // n) == (j // n)) & right & ((j % n) < m))
    prefix = np.concatenate(mats, axis=0).astype(np.float32)
    masks = np.stack([np.tile(mk, (1, GLA_HEADS)) for mk in masks]).astype(np.float32)
    kl = np.arange(QK_PAD)[None, :] // GLA_DK
    vl = np.arange(GLA_V)[:, None] // GLA_DV
    state_mask = (kl == vl).astype(np.float32)
    seg = (np.arange(GLA_V)[:, None] // GLA_DV == np.arange(GLA_V)[None, :] // GLA_DV)
    return prefix, masks, state_mask, seg.astype(np.float32)


def _gla_kernel(q_ref, k_ref, a_ref, v_ref, g_ref, pre_ref, msk_ref, smask_ref, seg_ref, on_ref,
                o_ref, st_ref):
    C = CHUNK

    @pl.when(pl.program_id(1) == 0)
    def _():
        st_ref[...] = jnp.zeros_like(st_ref)

    klane = lax.broadcasted_iota(jnp.int32, (1, QK_PAD), 1) // GLA_DK
    vlane = lax.broadcasted_iota(jnp.int32, (1, GLA_V), 1) // GLA_DV
    pre = pre_ref[...]

    def chunk(c, carry):
        rows = pl.ds(pl.multiple_of(c * C, C), C)
        q = q_ref[0, rows, :] * (GLA_DK ** -0.5)
        k = k_ref[0, rows, :]
        v = v_ref[0, rows, :]
        e = _split_dot_left(pre, a_ref[0, rows, :])
        vb = v.astype(BF16)

        st = st_ref[...]
        o = _dot_nt((q * jnp.exp(e[0:C])).astype(BF16), st.astype(BF16))

        att = jnp.zeros((C, GLA_HEADS * C), F32)
        for l in range(_GLA_LEVELS + 1):
            if l == 0:
                qt, kt = q, k
            else:
                base = (2 * l) * C
                qt = q * jnp.exp(e[base:base + C])
                kt = k * jnp.exp(e[base + C:base + 2 * C])
            kst = jnp.concatenate(
                [jnp.where(klane == h, kt, 0.0) for h in range(GLA_HEADS)], axis=0).astype(BF16)
            att = att + _dot_nt(qt.astype(BF16), kst) * msk_ref[l]
        vst = jnp.concatenate(
            [jnp.where(vlane == h, v, 0.0) for h in range(GLA_HEADS)], axis=0).astype(BF16)
        o = o + _dot(att.astype(BF16), vst)

        kdec = (k * jnp.exp(e[C:2 * C])).astype(BF16)
        upd = _dot_tn(vb, kdec) * smask_ref[...]
        st_ref[...] = st * jnp.exp(e[C - 1:C]) + upd

        ssq = _split_dot(o * o, seg_ref[...])
        y = o * lax.rsqrt(ssq * (1.0 / GLA_DV) + EPS) * on_ref[...]
        o_ref[0, rows, :] = y * _silu(g_ref[0, rows, :])
        return carry

    lax.fori_loop(0, q_ref.shape[1] // C, chunk, 0)


def _gla(gq, gk, ga, gv, gg, out_norm):
    B, T, _ = gq.shape
    prefix, masks, state_mask, seg = _gla_constants()
    tt = GLA_TILE
    seq_spec = lambda w: pl.BlockSpec((1, tt, w), lambda b, i: (b, i, 0))
    consts = (jnp.asarray(prefix, BF16), jnp.asarray(masks), jnp.asarray(state_mask),
              jnp.asarray(seg, BF16), jnp.tile(out_norm, GLA_HEADS).reshape(1, GLA_V))
    return pl.pallas_call(
        _gla_kernel,
        out_shape=jax.ShapeDtypeStruct((B, T, GLA_V), F32),
        grid=(B, T // tt),
        in_specs=[seq_spec(QK_PAD), seq_spec(QK_PAD), seq_spec(QK_PAD), seq_spec(GLA_V), seq_spec(GLA_V)]
                 + [_const_spec(cst.shape) for cst in consts],
        out_specs=seq_spec(GLA_V),
        scratch_shapes=[pltpu.VMEM((GLA_V, QK_PAD), F32)],
        compiler_params=pltpu.CompilerParams(
            dimension_semantics=("parallel", "arbitrary"), vmem_limit_bytes=VMEM_LIMIT),
        name="gla",
    )(gq, gk, ga, gv, gg, *consts)


def _sb_kernel(q_ref, k_ref, v_ref, og_ref, seg_ref, causal_ref, suf_ref, o_ref, kn_ref, vb_ref, ot_ref):
    T = q_ref.shape[1]
    BLK = SB_BLOCK
    LEAD = SB_LEAD
    first = lax.broadcasted_iota(jnp.int32, (1, 2 * SB_DH), 1) < SB_DH
    top = lax.broadcasted_iota(jnp.int32, (2 * SB_DH, 1), 0) < SB_DH

    kn_ref[0:LEAD, :] = jnp.zeros((LEAD, 2 * SB_DH), BF16)
    vb_ref[0:LEAD, :] = jnp.zeros((LEAD, 2 * SB_DH), BF16)
    kn_ref[LEAD:, :] = k_ref[0]
    vb_ref[LEAD:, :] = v_ref[0]

    def scores(qs, krows):
        return _dot_nt(kn_ref[krows, :], qs)

    def log_terms(z, causal):
        l1p = jnp.log(1.0 + jnp.exp(-jnp.abs(z)))
        log_beta = jnp.minimum(z, 0.0) - l1p
        log_keep = -jnp.maximum(z, 0.0) - l1p
        if causal:
            log_keep = log_keep * causal_ref[...]
        return log_beta, log_keep

    def suffix_sums(log_keep):
        size = log_keep.shape[0]
        return _split_dot_left(suf_ref[0:size, 0:size], log_keep)

    def weights(log_beta, suffix, carry, causal):
        w = jnp.exp(log_beta + suffix + carry)
        if causal:
            w = w * causal_ref[...]
        return w.astype(BF16)

    def stacked_queries(qi):
        q2 = q_ref[0, pl.ds(pl.multiple_of(qi * BLK, BLK), BLK), :]
        zq = jnp.zeros_like(q2)
        return jnp.concatenate([jnp.where(first, q2, zq), jnp.where(first, zq, q2)], axis=0)

    near_blocks = [(LEAD, BLK, True)]
    stop = LEAD
    while stop > 0:
        size = min(BLK, stop)
        near_blocks.append((stop - size, size, False))
        stop -= size

    def near_keys(qis, qss):
        jobs = [(g, pl.ds(pl.multiple_of(qi * BLK + off, SB_ALIGN), size), causal)
                for g, qi in enumerate(qis) for off, size, causal in near_blocks]
        zs = [scores(qss[g], krows) for g, krows, _ in jobs]
        terms = [log_terms(z, causal) for z, (_, _, causal) in zip(zs, jobs)]
        sufs = [suffix_sums(lk) for _, lk in terms]
        accs = [None] * len(qis)
        carries = [jnp.zeros((1, 2 * BLK), F32)] * len(qis)
        ws = []
        for (g, krows, causal), (lb, lk), suf in zip(jobs, terms, sufs):
            ws.append(weights(lb, suf, carries[g], causal))
            carries[g] = carries[g] + suf[0:1, :] + lk[0:1, :]
        for (g, krows, _), w in zip(jobs, ws):
            upd = _dot_tn(vb_ref[krows, :], w)
            accs[g] = upd if accs[g] is None else accs[g] + upd
        return accs, carries

    def far_keys(qi, qs, acc, carry):
        def cond(s):
            return jnp.logical_and(s[0] > -BLK, s[3])

        def body(s):
            kpos, carry, acc, _ = s
            krows = pl.ds(pl.multiple_of(kpos + LEAD, SB_ALIGN), BLK)
            lb, lk = log_terms(scores(qs, krows), False)
            suf = suffix_sums(lk)
            acc = acc + _dot_tn(vb_ref[krows, :], weights(lb, suf, carry, False))
            carry = carry + suf[0:1, :] + lk[0:1, :]
            return kpos - BLK, carry, acc, jnp.max(carry) > -SB_SKIP

        return lax.while_loop(cond, body, (qi * BLK - LEAD - BLK, carry, acc, jnp.max(carry) > -SB_SKIP))[2]

    def qgroup(m, c0):
        qis = [SB_GROUP * m + g for g in range(SB_GROUP)]
        qss = [stacked_queries(qi) for qi in qis]
        accs, carries = near_keys(qis, qss)
        for qi, qs, acc, carry in zip(qis, qss, accs, carries):
            acc = far_keys(qi, qs, acc, carry)
            ot_ref[:, pl.ds(pl.multiple_of(qi * BLK, BLK), BLK)] = jnp.where(top, acc[:, :BLK], acc[:, BLK:])
        return c0

    lax.fori_loop(0, T // (SB_GROUP * BLK), qgroup, 0)

    def finish(i, carry):
        cols = pl.ds(pl.multiple_of(i * TOKEN_TILE, TOKEN_TILE), TOKEN_TILE)
        o_ref[0, cols, :] = _head_rms(ot_ref[:, cols].T, seg_ref[...], og_ref[...], SB_DH)
        return carry

    lax.fori_loop(0, T // TOKEN_TILE, finish, 0)


def _sb_constants():
    BLK = SB_BLOCK
    s = np.arange(BLK)[:, None]
    j = np.arange(BLK)[None, :]
    suffix = (j > s).astype(np.float32)
    causal = (s < np.tile(np.arange(BLK), 2)[None, :]).astype(np.float32)
    return causal, suffix


def _sb(sq, sk, sv, out_norm):
    B, T, _ = sq.shape
    causal, suffix = _sb_constants()
    consts = (jnp.tile(out_norm, 2).reshape(1, 2 * SB_DH), _same_head(2 * SB_DH, SB_DH),
              jnp.asarray(causal), jnp.asarray(suffix, BF16))
    spec = pl.BlockSpec((1, T, 2 * SB_DH), lambda b, p: (b, 0, p))
    return pl.pallas_call(
        _sb_kernel,
        out_shape=jax.ShapeDtypeStruct((B, T, SB_W), F32),
        grid=(B, SB_HEADS // 2),
        in_specs=[spec, spec, spec] + [_const_spec(cst.shape) for cst in consts],
        out_specs=spec,
        scratch_shapes=[pltpu.VMEM((T + SB_LEAD, 2 * SB_DH), BF16), pltpu.VMEM((T + SB_LEAD, 2 * SB_DH), BF16),
                        pltpu.VMEM((2 * SB_DH, T), F32)],
        compiler_params=pltpu.CompilerParams(
            dimension_semantics=("parallel", "parallel"), vmem_limit_bytes=VMEM_LIMIT),
        name="stickbreak",
    )(sq, sk, sv, *consts)


def _conv_kernel(u_ref, w_ref, b_ref, lg_ref, lb_ref, o_ref, pad_ref):
    T = u_ref.shape[1]
    pad_ref[0:CONV_PAD, :] = jnp.zeros((CONV_PAD, CONV_CH), F32)
    pad_ref[CONV_PAD:, :] = u_ref[0]
    first_tap = CONV_PAD - (CONV_WIDTH - 1)
    for i in range(T // CONV_TILE):
        base = i * CONV_TILE + first_tap
        acc = jnp.zeros((CONV_TILE, CONV_CH), F32)
        for j in range(CONV_WIDTH):
            acc = acc + w_ref[j:j + 1, :] * pad_ref[base + j:base + j + CONV_TILE, :]
        acc = acc + b_ref[...]
        mu = jnp.mean(acc, axis=-1, keepdims=True)
        xc = acc - mu
        var = jnp.mean(xc * xc, axis=-1, keepdims=True)
        y = xc * lax.rsqrt(var + EPS) * lg_ref[...] + lb_ref[...]
        o_ref[0, i * CONV_TILE:(i + 1) * CONV_TILE, :] = _silu(y)


def _conv(cu, w, b, ln_g, ln_b):
    B, T, C = cu.shape
    spec = pl.BlockSpec((1, T, C), lambda bi: (bi, 0, 0))
    vec = lambda a: a.reshape(1, C)
    return pl.pallas_call(
        _conv_kernel,
        out_shape=jax.ShapeDtypeStruct((B, T, C), F32),
        grid=(B,),
        in_specs=[spec, _const_spec(w.shape), _const_spec((1, C)), _const_spec((1, C)), _const_spec((1, C))],
        out_specs=spec,
        scratch_shapes=[pltpu.VMEM((T + CONV_PAD, C), F32)],
        compiler_params=pltpu.CompilerParams(
            dimension_semantics=("parallel",), vmem_limit_bytes=VMEM_LIMIT),
        name="conv",
    )(cu, w, vec(b), vec(ln_g), vec(ln_b))


def _outproj_kernel(x_ref, a_ref, b_ref, c_ref, gt_ref, wa_ref, wb_ref, wc_ref, o_ref):
    y = (_dot(a_ref[...].astype(BF16), wa_ref[...]) + _dot(b_ref[...].astype(BF16), wb_ref[...])
         + _dot(c_ref[...].astype(BF16), wc_ref[...]))
    o_ref[...] = x_ref[...] + gt_ref[0] * y


def _outproj(x2, oa, ob, oc, gate, w_out, seq):
    N, D = x2.shape
    wb16 = w_out.astype(BF16)
    wa, wb, wc = wb16[:GLA_V], wb16[GLA_V:GLA_V + SB_W], wb16[GLA_V + SB_W:]
    tm = TOKEN_TILE
    tpb = seq // tm
    return pl.pallas_call(
        _outproj_kernel,
        out_shape=jax.ShapeDtypeStruct((N, D), F32),
        grid=(N // tm,),
        in_specs=[
            _row_spec(tm, D), _row_spec(tm, GLA_V), _row_spec(tm, SB_W), _row_spec(tm, CONV_CH),
            _batch_vec_spec(tpb, D),
            _const_spec(wa.shape), _const_spec(wb.shape), _const_spec(wc.shape),
        ],
        out_specs=_row_spec(tm, D),
        compiler_params=pltpu.CompilerParams(
            dimension_semantics=("parallel",), vmem_limit_bytes=VMEM_LIMIT),
        name="outproj",
    )(x2, oa, ob, oc, gate, wa, wb, wc)


def kernel(x, c, w_ada, b_ada, norm_ffn1, ffn1_w_in, ffn1_w_out, norm_mix, w_in, w_out, gla_w_gate_up, gla_b_gate, gla_out_norm, sb_q_norm, sb_k_norm, sb_out_norm, conv_w, conv_b, conv_ln_g, conv_ln_b, norm_ffn2, ffn2_w_in, ffn2_w_out):
    B, T, D = x.shape
    L = w_ada.shape[0]
    mod = _adaln(c, w_ada, b_ada).reshape(L, B, N_MOD, 1, D)
    x2 = x.reshape(B * T, D)
    for l in range(L):
        sh1, sc1, gt1, sh2, sc2, gt2, sh3, sc3, gt3 = (mod[l, :, i] for i in range(N_MOD))
        x2 = _ffn(x2, norm_ffn1[l], sh1, sc1, gt1, ffn1_w_in[l], ffn1_w_out[l], T)
        gq, gk, ga, gv, gg, sq, sk, sv, cu = _inproj(
            x2, norm_mix[l], sh2, sc2, w_in[l], gla_w_gate_up[l], gla_b_gate[l], sb_q_norm[l], sb_k_norm[l], T)
        seq = lambda a: a.reshape(B, T, a.shape[-1])
        oa = _gla(seq(gq), seq(gk), seq(ga), seq(gv), seq(gg), gla_out_norm[l])
        ob = _sb(seq(sq), seq(sk), seq(sv), sb_out_norm[l])
        oc = _conv(seq(cu), conv_w[l], conv_b[l], conv_ln_g[l], conv_ln_b[l])
        x2 = _outproj(x2, oa.reshape(B * T, GLA_V), ob.reshape(B * T, SB_W), oc.reshape(B * T, CONV_CH),
                      gt2, w_out[l], T)
        x2 = _ffn(x2, norm_ffn2[l], sh3, sc3, gt3, ffn2_w_in[l], ffn2_w_out[l], T)
    return x2.reshape(B, T, D)
```

```python
import functools

import numpy as np
import jax
import jax.numpy as jnp
from jax import lax
from jax.experimental import pallas as pl
from jax.experimental.pallas import tpu as pltpu

F32 = jnp.float32
BF16 = jnp.bfloat16

EPS = 1e-6
CHUNK = 64
GLA_HEADS, GLA_DK, GLA_DV, GLA_RANK, GLA_TAU = 4, 48, 96, 16, 16.0
SB_HEADS, SB_DH = 6, 64
CONV_CH, CONV_WIDTH = 256, 31
D_FF = 2816
N_MOD = 9
GLA_QK = GLA_HEADS * GLA_DK
GLA_V = GLA_HEADS * GLA_DV
SB_W = SB_HEADS * SB_DH

LANES = 128
QK_PAD = 256
RANK_PAD = 128
FF_CHUNK = 256
TOKEN_TILE = 512
SB_BLOCK = 128
SB_GROUP = 4
SB_SKIP = 104.0
GLA_TILE = 512
CONV_TILE = 256
CONV_PAD = 32
VMEM_LIMIT = 52 * 1024 * 1024

_OFF_GQ, _OFF_GK, _OFF_GV, _OFF_GG, _OFF_GR = 0, 256, 512, 896, 1280
_OFF_SQ, _OFF_SK, _OFF_SV, _OFF_CA, _OFF_CG = 1408, 1792, 2176, 2560, 2816
PROJ_W = 3072


def _dot(a, b):
    return jnp.dot(a, b, preferred_element_type=F32)


def _dot_nt(a, b):
    return lax.dot_general(a, b, (((1,), (1,)), ((), ())), preferred_element_type=F32)


def _dot_tn(a, b):
    return lax.dot_general(a, b, (((0,), (0,)), ((), ())), preferred_element_type=F32)


def _split_dot(x, m):
    hi = x.astype(BF16)
    lo = (x - hi.astype(F32)).astype(BF16)
    return _dot(hi, m) + _dot(lo, m)


def _split_dot_left(m, x):
    hi = x.astype(BF16)
    lo = (x - hi.astype(F32)).astype(BF16)
    return _dot(m, hi) + _dot(m, lo)


def _sigmoid(x):
    return 1.0 / (1.0 + jnp.exp(-x))


def _silu(x):
    return x * _sigmoid(x)


def _log_sigmoid(x):
    return jnp.minimum(x, 0.0) - jnp.log(1.0 + jnp.exp(-jnp.abs(x)))


def _modulated_norm(x, g, scale, shift):
    ms = jnp.mean(x * x, axis=-1, keepdims=True)
    return (x * lax.rsqrt(ms + EPS) * g) * (1.0 + scale) + shift


def _adaln_kernel(c_ref, w_ref, b_ref, o_ref):
    ca = _silu(c_ref[...]).astype(BF16)
    o_ref[0] = _dot(ca, w_ref[0].astype(BF16)) + b_ref[0]


def _adaln(c, w_ada, b_ada):
    L, D, W = w_ada.shape
    B = c.shape[0]
    tn = 1536
    return pl.pallas_call(
        _adaln_kernel,
        out_shape=jax.ShapeDtypeStruct((L, B, W), F32),
        grid=(L, W // tn),
        in_specs=[
            pl.BlockSpec((B, D), lambda l, j: (0, 0)),
            pl.BlockSpec((1, D, tn), lambda l, j: (l, 0, j)),
            pl.BlockSpec((1, 1, tn), lambda l, j: (l, 0, j)),
        ],
        out_specs=pl.BlockSpec((1, B, tn), lambda l, j: (l, 0, j)),
        compiler_params=pltpu.CompilerParams(
            dimension_semantics=("parallel", "parallel"), vmem_limit_bytes=VMEM_LIMIT),
        name="adaln",
    )(c, w_ada, b_ada.reshape(L, 1, W))


def _ffn_kernel(x_ref, g_ref, sh_ref, sc_ref, gt_ref, win_ref, wout_ref, o_ref, acc_ref):
    x = x_ref[...]
    hb = _modulated_norm(x, g_ref[...], sc_ref[0], sh_ref[0]).astype(BF16)
    n = wout_ref.shape[0]
    for j in range(n):
        a = _dot(hb, win_ref[j])
        b = _dot(hb, win_ref[n + j])
        y = _dot((_silu(a) * b).astype(BF16), wout_ref[j])
        if j == 0:
            acc_ref[...] = y
        else:
            acc_ref[...] += y
    o_ref[...] = x + (0.5 * gt_ref[0]) * acc_ref[...]


def _const_spec(shape):
    nd = len(shape)
    return pl.BlockSpec(shape, lambda *_: (0,) * nd, pipeline_mode=pl.Buffered(1))


def _row_spec(tm, w):
    return pl.BlockSpec((tm, w), lambda i: (i, 0))


def _batch_vec_spec(tiles_per_batch, w):
    return pl.BlockSpec((1, 1, w), lambda i: (i // tiles_per_batch, 0, 0))


def _ffn(x2, g, shift, scale, gate, w_in, w_out, seq):
    N, D = x2.shape
    nchunk = D_FF // FF_CHUNK
    win = w_in.reshape(D, 2 * nchunk, FF_CHUNK).transpose(1, 0, 2).astype(BF16)
    wout = w_out.reshape(nchunk, FF_CHUNK, D).astype(BF16)
    tm = TOKEN_TILE
    tpb = seq // tm
    return pl.pallas_call(
        _ffn_kernel,
        out_shape=jax.ShapeDtypeStruct((N, D), F32),
        grid=(N // tm,),
        in_specs=[
            _row_spec(tm, D),
            _const_spec((1, D)),
            _batch_vec_spec(tpb, D), _batch_vec_spec(tpb, D), _batch_vec_spec(tpb, D),
            _const_spec(win.shape), _const_spec(wout.shape),
        ],
        out_specs=_row_spec(tm, D),
        scratch_shapes=[pltpu.VMEM((tm, D), F32)],
        compiler_params=pltpu.CompilerParams(
            dimension_semantics=("parallel",), vmem_limit_bytes=VMEM_LIMIT),
        name="ffn",
    )(x2, g.reshape(1, D), shift, scale, gate, win, wout)


def _head_rms(x, seg, gain, head_dim):
    ms = _split_dot(x * x, seg) * (1.0 / head_dim)
    return x * lax.rsqrt(ms + EPS) * gain


def _inproj_kernel(x_ref, g_ref, sh_ref, sc_ref, w_ref, wup_ref, bg_ref, seg_ref, qn_ref, kn_ref,
                   gq_ref, gk_ref, ga_ref, gv_ref, gg_ref, sq_ref, sk_ref, sv_ref, cu_ref):
    hb = _modulated_norm(x_ref[...], g_ref[...], sc_ref[0], sh_ref[0]).astype(BF16)
    p = _dot(hb, w_ref[...])
    gq_ref[...] = p[:, _OFF_GQ:_OFF_GQ + QK_PAD]
    gk_ref[...] = p[:, _OFF_GK:_OFF_GK + QK_PAD]
    gv_ref[...] = p[:, _OFF_GV:_OFF_GV + GLA_V]
    gg_ref[...] = p[:, _OFF_GG:_OFF_GG + GLA_V]
    r = p[:, _OFF_GR:_OFF_GR + RANK_PAD].astype(BF16)
    ga_ref[...] = _log_sigmoid(_dot(r, wup_ref[...]) + bg_ref[...]) * (1.0 / GLA_TAU)
    seg = seg_ref[...]
    sq = _head_rms(p[:, _OFF_SQ:_OFF_SQ + SB_W], seg, qn_ref[...], SB_DH)
    sq_ref[...] = (sq * (SB_DH ** -0.5)).astype(BF16)
    sk_ref[...] = _head_rms(p[:, _OFF_SK:_OFF_SK + SB_W], seg, kn_ref[...], SB_DH).astype(BF16)
    sv_ref[...] = p[:, _OFF_SV:_OFF_SV + SB_W].astype(BF16)
    cu_ref[...] = p[:, _OFF_CA:_OFF_CA + CONV_CH] * _sigmoid(p[:, _OFF_CG:_OFF_CG + CONV_CH])


def _same_head(width, head_dim):
    h = np.arange(width) // head_dim
    return jnp.asarray((h[:, None] == h[None, :]).astype(np.float32), BF16)


def _pad_cols(w, width):
    return jnp.pad(w, ((0, 0), (0, width - w.shape[1])))


def _inproj(x2, g, shift, scale, w_in, w_up, b_gate, q_norm, k_norm, seq):
    N, D = x2.shape
    o = 0
    parts = []
    for width, pad in ((GLA_QK, QK_PAD), (GLA_QK, QK_PAD), (GLA_V, GLA_V), (GLA_V, GLA_V),
                       (GLA_RANK, RANK_PAD), (SB_W, SB_W), (SB_W, SB_W), (SB_W, SB_W),
                       (CONV_CH, CONV_CH), (CONV_CH, CONV_CH)):
        parts.append(_pad_cols(w_in[:, o:o + width], pad))
        o += width
    w = jnp.concatenate(parts, axis=1).astype(BF16)
    wup = jnp.pad(w_up, ((0, RANK_PAD - GLA_RANK), (0, QK_PAD - GLA_QK))).astype(BF16)
    bg = jnp.pad(b_gate, (0, QK_PAD - GLA_QK)).reshape(1, QK_PAD)
    tm = TOKEN_TILE
    tpb = seq // tm
    seg = _same_head(SB_W, SB_DH)
    heads = lambda gn: jnp.tile(gn, SB_HEADS).reshape(1, SB_W)
    outs = ((QK_PAD, F32), (QK_PAD, F32), (QK_PAD, F32), (GLA_V, F32), (GLA_V, F32),
            (SB_W, BF16), (SB_W, BF16), (SB_W, BF16), (CONV_CH, F32))
    return pl.pallas_call(
        _inproj_kernel,
        out_shape=[jax.ShapeDtypeStruct((N, wd), dt) for wd, dt in outs],
        grid=(N // tm,),
        in_specs=[
            _row_spec(tm, D),
            _const_spec((1, D)),
            _batch_vec_spec(tpb, D), _batch_vec_spec(tpb, D),
            _const_spec(w.shape), _const_spec(wup.shape), _const_spec(bg.shape),
            _const_spec(seg.shape), _const_spec((1, SB_W)), _const_spec((1, SB_W)),
        ],
        out_specs=[_row_spec(tm, wd) for wd, _ in outs],
        compiler_params=pltpu.CompilerParams(
            dimension_semantics=("parallel",), vmem_limit_bytes=VMEM_LIMIT),
        name="inproj",
    )(x2, g.reshape(1, D), shift, scale, w, wup, bg, seg, heads(q_norm), heads(k_norm))


_GLA_LEVELS = 6


def _gla_constants():
    C = CHUNK
    t = np.arange(C)[:, None]
    j = np.arange(C)[None, :]
    mats = [j <= t, j > t]
    masks = [t == j]
    for l in range(1, _GLA_LEVELS + 1):
        n, m = 1 << l, 1 << (l - 1)
        ref = t // n * n + m - 1
        right = (t % n) >= m
        mats.append(right & (j > ref) & (j <= t))
        mats.append((~right) & (j > t) & (j <= ref))
        masks.append(((t // n) == (j // n)) & right & ((j % n) < m))
    prefix = np.concatenate(mats, axis=0).astype(np.float32)
    masks = np.stack([np.tile(mk, (1, GLA_HEADS)) for mk in masks]).astype(np.float32)
    kl = np.arange(QK_PAD)[None, :] // GLA_DK
    vl = np.arange(GLA_V)[:, None] // GLA_DV
    state_mask = (kl == vl).astype(np.float32)
    seg = (np.arange(GLA_V)[:, None] // GLA_DV == np.arange(GLA_V)[None, :] // GLA_DV)
    return prefix, masks, state_mask, seg.astype(np.float32)


def _gla_kernel(q_ref, k_ref, a_ref, v_ref, g_ref, pre_ref, msk_ref, smask_ref, seg_ref, on_ref,
                o_ref, st_ref):
    C = CHUNK

    @pl.when(pl.program_id(1) == 0)
    def _():
        st_ref[...] = jnp.zeros_like(st_ref)

    klane = lax.broadcasted_iota(jnp.int32, (1, QK_PAD), 1) // GLA_DK
    vlane = lax.broadcasted_iota(jnp.int32, (1, GLA_V), 1) // GLA_DV
    pre = pre_ref[...]

    def chunk(c, carry):
        rows = pl.ds(pl.multiple_of(c * C, C), C)
        q = q_ref[0, rows, :] * (GLA_DK ** -0.5)
        k = k_ref[0, rows, :]
        v = v_ref[0, rows, :]
        e = _split_dot_left(pre, a_ref[0, rows, :])
        vb = v.astype(BF16)

        st = st_ref[...]
        o = _dot_nt((q * jnp.exp(e[0:C])).astype(BF16), st.astype(BF16))

        att = jnp.zeros((C, GLA_HEADS * C), F32)
        for l in range(_GLA_LEVELS + 1):
            if l == 0:
                qt, kt = q, k
            else:
                base = (2 * l) * C
                qt = q * jnp.exp(e[base:base + C])
                kt = k * jnp.exp(e[base + C:base + 2 * C])
            kst = jnp.concatenate(
                [jnp.where(klane == h, kt, 0.0) for h in range(GLA_HEADS)], axis=0).astype(BF16)
            att = att + _dot_nt(qt.astype(BF16), kst) * msk_ref[l]
        vst = jnp.concatenate(
            [jnp.where(vlane == h, v, 0.0) for h in range(GLA_HEADS)], axis=0).astype(BF16)
        o = o + _dot(att.astype(BF16), vst)

        kdec = (k * jnp.exp(e[C:2 * C])).astype(BF16)
        upd = _dot_tn(vb, kdec) * smask_ref[...]
        st_ref[...] = st * jnp.exp(e[C - 1:C]) + upd

        ssq = _split_dot(o * o, seg_ref[...])
        y = o * lax.rsqrt(ssq * (1.0 / GLA_DV) + EPS) * on_ref[...]
        o_ref[0, rows, :] = y * _silu(g_ref[0, rows, :])
        return carry

    lax.fori_loop(0, q_ref.shape[1] // C, chunk, 0)


def _gla(gq, gk, ga, gv, gg, out_norm):
    B, T, _ = gq.shape
    prefix, masks, state_mask, seg = _gla_constants()
    tt = GLA_TILE
    seq_spec = lambda w: pl.BlockSpec((1, tt, w), lambda b, i: (b, i, 0))
    consts = (jnp.asarray(prefix, BF16), jnp.asarray(masks), jnp.asarray(state_mask),
              jnp.asarray(seg, BF16), jnp.tile(out_norm, GLA_HEADS).reshape(1, GLA_V))
    return pl.pallas_call(
        _gla_kernel,
        out_shape=jax.ShapeDtypeStruct((B, T, GLA_V), F32),
        grid=(B, T // tt),
        in_specs=[seq_spec(QK_PAD), seq_spec(QK_PAD), seq_spec(QK_PAD), seq_spec(GLA_V), seq_spec(GLA_V)]
                 + [_const_spec(cst.shape) for cst in consts],
        out_specs=seq_spec(GLA_V),
        scratch_shapes=[pltpu.VMEM((GLA_V, QK_PAD), F32)],
        compiler_params=pltpu.CompilerParams(
            dimension_semantics=("parallel", "arbitrary"), vmem_limit_bytes=VMEM_LIMIT),
        name="gla",
    )(gq, gk, ga, gv, gg, *consts)


def _sb_kernel(q_ref, k_ref, v_ref, og_ref, seg_ref, causal_ref, suf_ref, o_ref,
               kn_ref, vb_ref, acc_ref, carry_ref, alive_ref):
    T = q_ref.shape[1]
    BLK, G = SB_BLOCK, SB_GROUP
    n_q = T // BLK
    n_groups = n_q // G
    first = lax.broadcasted_iota(jnp.int32, (1, 2 * SB_DH), 1) < SB_DH
    top = lax.broadcasted_iota(jnp.int32, (2 * SB_DH, 1), 0) < SB_DH

    kn_ref[0:BLK, :] = jnp.zeros((BLK, 2 * SB_DH), BF16)
    vb_ref[0:BLK, :] = jnp.zeros((BLK, 2 * SB_DH), BF16)
    kn_ref[BLK:, :] = k_ref[0]
    vb_ref[BLK:, :] = v_ref[0]

    def stacked_queries(qi):
        q2 = q_ref[0, pl.ds(pl.multiple_of(qi * BLK, BLK), BLK), :]
        zq = jnp.zeros_like(q2)
        return jnp.concatenate([jnp.where(first, q2, zq), jnp.where(first, zq, q2)], axis=0)

    def group_step(grp, dist, causal):
        qis = [grp * G + g for g in range(G)]
        krows = [pl.ds(pl.multiple_of(jnp.maximum(qi - dist + 1, 0) * BLK, BLK), BLK) for qi in qis]
        zs = [_dot_nt(kn_ref[kr, :], stacked_queries(qi)) for qi, kr in zip(qis, krows)]
        log_betas, log_keeps = [], []
        for z in zs:
            l1p = jnp.log(1.0 + jnp.exp(-jnp.abs(z)))
            log_betas.append(jnp.minimum(z, 0.0) - l1p)
            lk = -jnp.maximum(z, 0.0) - l1p
            log_keeps.append(lk * causal_ref[...] if causal else lk)
        sufs = [_split_dot_left(suf_ref[...], lk) for lk in log_keeps]
        ws, alive = [], None
        for qi, lb, lk, suf in zip(qis, log_betas, log_keeps, sufs):
            carry = jnp.zeros((1, 2 * BLK), F32) if causal else carry_ref[qi]
            w = jnp.exp(lb + suf + carry)
            ws.append((w * causal_ref[...] if causal else w).astype(BF16))
            carry = carry + suf[0:1, :] + lk[0:1, :]
            carry_ref[qi] = carry
            alive = carry if alive is None else jnp.maximum(alive, carry)
        for qi, kr, w in zip(qis, krows, ws):
            upd = _dot_tn(vb_ref[kr, :], w)
            acc_ref[qi] = upd if causal else acc_ref[qi] + upd
        return jnp.max(alive) > -SB_SKIP

    def diagonal(grp, c0):
        alive_ref[grp] = group_step(grp, 0, True).astype(jnp.int32)
        return c0

    lax.fori_loop(0, n_groups, diagonal, 0)

    def any_alive():
        total = alive_ref[0]
        for g in range(1, n_groups):
            total = total + alive_ref[g]
        return total > 0

    def sweep(state):
        dist, _ = state

        def visit(grp, c0):
            run = jnp.logical_and(alive_ref[grp] > 0, grp * G + G - 1 >= dist)

            @pl.when(run)
            def _():
                alive_ref[grp] = group_step(grp, dist, False).astype(jnp.int32)

            @pl.when(jnp.logical_not(run))
            def _():
                alive_ref[grp] = 0

            return c0

        lax.fori_loop(0, n_groups, visit, 0)
        return dist + 1, any_alive()

    lax.while_loop(lambda s: jnp.logical_and(s[0] < n_q, s[1]), sweep, (1, any_alive()))

    def finish(grp, c0):
        cols = pl.ds(pl.multiple_of(grp * G * BLK, G * BLK), G * BLK)
        ot = jnp.concatenate(
            [jnp.where(top, acc_ref[grp * G + g][:, :BLK], acc_ref[grp * G + g][:, BLK:]) for g in range(G)],
            axis=1)
        o_ref[0, cols, :] = _head_rms(ot.T, seg_ref[...], og_ref[...], SB_DH)
        return c0

    lax.fori_loop(0, n_groups, finish, 0)


def _sb_constants():
    BLK = SB_BLOCK
    s = np.arange(BLK)[:, None]
    j = np.arange(BLK)[None, :]
    suffix = (j > s).astype(np.float32)
    causal = (s < np.tile(np.arange(BLK), 2)[None, :]).astype(np.float32)
    return causal, suffix


def _sb(sq, sk, sv, out_norm):
    B, T, _ = sq.shape
    causal, suffix = _sb_constants()
    consts = (jnp.tile(out_norm, 2).reshape(1, 2 * SB_DH), _same_head(2 * SB_DH, SB_DH),
              jnp.asarray(causal), jnp.asarray(suffix, BF16))
    spec = pl.BlockSpec((1, T, 2 * SB_DH), lambda b, p: (b, 0, p))
    n_q = T // SB_BLOCK
    return pl.pallas_call(
        _sb_kernel,
        out_shape=jax.ShapeDtypeStruct((B, T, SB_W), F32),
        grid=(B, SB_HEADS // 2),
        in_specs=[spec, spec, spec] + [_const_spec(cst.shape) for cst in consts],
        out_specs=spec,
        scratch_shapes=[pltpu.VMEM((T + SB_BLOCK, 2 * SB_DH), BF16), pltpu.VMEM((T + SB_BLOCK, 2 * SB_DH), BF16),
                        pltpu.VMEM((n_q, 2 * SB_DH, 2 * SB_BLOCK), F32), pltpu.VMEM((n_q, 1, 2 * SB_BLOCK), F32),
                        pltpu.SMEM((n_q // SB_GROUP,), jnp.int32)],
        compiler_params=pltpu.CompilerParams(
            dimension_semantics=("parallel", "parallel"), vmem_limit_bytes=VMEM_LIMIT),
        name="stickbreak",
    )(sq, sk, sv, *consts)


def _conv_kernel(u_ref, w_ref, b_ref, lg_ref, lb_ref, o_ref, pad_ref):
    T = u_ref.shape[1]
    pad_ref[0:CONV_PAD, :] = jnp.zeros((CONV_PAD, CONV_CH), F32)
    pad_ref[CONV_PAD:, :] = u_ref[0]
    first_tap = CONV_PAD - (CONV_WIDTH - 1)
    for i in range(T // CONV_TILE):
        base = i * CONV_TILE + first_tap
        acc = jnp.zeros((CONV_TILE, CONV_CH), F32)
        for j in range(CONV_WIDTH):
            acc = acc + w_ref[j:j + 1, :] * pad_ref[base + j:base + j + CONV_TILE, :]
        acc = acc + b_ref[...]
        mu = jnp.mean(acc, axis=-1, keepdims=True)
        xc = acc - mu
        var = jnp.mean(xc * xc, axis=-1, keepdims=True)
        y = xc * lax.rsqrt(var + EPS) * lg_ref[...] + lb_ref[...]
        o_ref[0, i * CONV_TILE:(i + 1) * CONV_TILE, :] = _silu(y)


def _conv(cu, w, b, ln_g, ln_b):
    B, T, C = cu.shape
    spec = pl.BlockSpec((1, T, C), lambda bi: (bi, 0, 0))
    vec = lambda a: a.reshape(1, C)
    return pl.pallas_call(
        _conv_kernel,
        out_shape=jax.ShapeDtypeStruct((B, T, C), F32),
        grid=(B,),
        in_specs=[spec, _const_spec(w.shape), _const_spec((1, C)), _const_spec((1, C)), _const_spec((1, C))],
        out_specs=spec,
        scratch_shapes=[pltpu.VMEM((T + CONV_PAD, C), F32)],
        compiler_params=pltpu.CompilerParams(
            dimension_semantics=("parallel",), vmem_limit_bytes=VMEM_LIMIT),
        name="conv",
    )(cu, w, vec(b), vec(ln_g), vec(ln_b))


def _outproj_kernel(x_ref, a_ref, b_ref, c_ref, gt_ref, wa_ref, wb_ref, wc_ref, o_ref):
    y = (_dot(a_ref[...].astype(BF16), wa_ref[...]) + _dot(b_ref[...].astype(BF16), wb_ref[...])
         + _dot(c_ref[...].astype(BF16), wc_ref[...]))
    o_ref[...] = x_ref[...] + gt_ref[0] * y


def _outproj(x2, oa, ob, oc, gate, w_out, seq):
    N, D = x2.shape
    wb16 = w_out.astype(BF16)
    wa, wb, wc = wb16[:GLA_V], wb16[GLA_V:GLA_V + SB_W], wb16[GLA_V + SB_W:]
    tm = TOKEN_TILE
    tpb = seq // tm
    return pl.pallas_call(
        _outproj_kernel,
        out_shape=jax.ShapeDtypeStruct((N, D), F32),
        grid=(N // tm,),
        in_specs=[
            _row_spec(tm, D), _row_spec(tm, GLA_V), _row_spec(tm, SB_W), _row_spec(tm, CONV_CH),
            _batch_vec_spec(tpb, D),
            _const_spec(wa.shape), _const_spec(wb.shape), _const_spec(wc.shape),
        ],
        out_specs=_row_spec(tm, D),
        compiler_params=pltpu.CompilerParams(
            dimension_semantics=("parallel",), vmem_limit_bytes=VMEM_LIMIT),
        name="outproj",
    )(x2, oa, ob, oc, gate, wa, wb, wc)


def kernel(x, c, w_ada, b_ada, norm_ffn1, ffn1_w_in, ffn1_w_out, norm_mix, w_in, w_out, gla_w_gate_up, gla_b_gate, gla_out_norm, sb_q_norm, sb_k_norm, sb_out_norm, conv_w, conv_b, conv_ln_g, conv_ln_b, norm_ffn2, ffn2_w_in, ffn2_w_out):
    B, T, D = x.shape
    L = w_ada.shape[0]
    mod = _adaln(c, w_ada, b_ada).reshape(L, B, N_MOD, 1, D)
    x2 = x.reshape(B * T, D)
    for l in range(L):
        sh1, sc1, gt1, sh2, sc2, gt2, sh3, sc3, gt3 = (mod[l, :, i] for i in range(N_MOD))
        x2 = _ffn(x2, norm_ffn1[l], sh1, sc1, gt1, ffn1_w_in[l], ffn1_w_out[l], T)
        gq, gk, ga, gv, gg, sq, sk, sv, cu = _inproj(
            x2, norm_mix[l], sh2, sc2, w_in[l], gla_w_gate_up[l], gla_b_gate[l], sb_q_norm[l], sb_k_norm[l], T)
        seq = lambda a: a.reshape(B, T, a.shape[-1])
        oa = _gla(seq(gq), seq(gk), seq(ga), seq(gv), seq(gg), gla_out_norm[l])
        ob = _sb(seq(sq), seq(sk), seq(sv), sb_out_norm[l])
        oc = _conv(seq(cu), conv_w[l], conv_b[l], conv_ln_g[l], conv_ln_b[l])
        x2 = _outproj(x2, oa.reshape(B * T, GLA_V), ob.reshape(B * T, SB_W), oc.reshape(B * T, CONV_CH),
                      gt2, w_out[l], T)
        x2 = _ffn(x2, norm_ffn2[l], sh3, sc3, gt3, ffn2_w_in[l], ffn2_w_out[l], T)
    return x2.reshape(B, T, D)
```

```python
import functools

import numpy as np
import jax
import jax.numpy as jnp
from jax import lax
from jax.experimental import pallas as pl
from jax.experimental.pallas import tpu as pltpu

F32 = jnp.float32
BF16 = jnp.bfloat16

EPS = 1e-6
CHUNK = 64
GLA_HEADS, GLA_DK, GLA_DV, GLA_RANK, GLA_TAU = 4, 48, 96, 16, 16.0
SB_HEADS, SB_DH = 6, 64
CONV_CH, CONV_WIDTH = 256, 31
D_FF = 2816
N_MOD = 9
GLA_QK = GLA_HEADS * GLA_DK
GLA_V = GLA_HEADS * GLA_DV
SB_W = SB_HEADS * SB_DH

LANES = 128
QK_PAD = 256
RANK_PAD = 128
FF_CHUNK = 256
TOKEN_TILE = 512
SB_BLOCK = 128
SB_GROUP = 4
SB_SKIP = 104.0
GLA_TILE = 512
GLA_GROUP = 4
CONV_TILE = 256
CONV_PAD = 32
VMEM_LIMIT = 52 * 1024 * 1024

_OFF_GQ, _OFF_GK, _OFF_GV, _OFF_GG, _OFF_GR = 0, 256, 512, 896, 1280
_OFF_SQ, _OFF_SK, _OFF_SV, _OFF_CA, _OFF_CG = 1408, 1792, 2176, 2560, 2816
PROJ_W = 3072


def _dot(a, b):
    return jnp.dot(a, b, preferred_element_type=F32)


def _dot_nt(a, b):
    return lax.dot_general(a, b, (((1,), (1,)), ((), ())), preferred_element_type=F32)


def _dot_tn(a, b):
    return lax.dot_general(a, b, (((0,), (0,)), ((), ())), preferred_element_type=F32)


def _hi_lo(x, axis):
    hi = x.astype(BF16)
    lo = (x - hi.astype(F32)).astype(BF16)
    return jnp.concatenate([hi, lo], axis=axis)


def _split_dot(x, m2):
    return _dot(_hi_lo(x, 1), m2)


def _split_dot_left(m2, x):
    return _dot(m2, _hi_lo(x, 0))


def _sigmoid(x):
    return 1.0 / (1.0 + jnp.exp(-x))


def _silu(x):
    return x * _sigmoid(x)


def _log_sigmoid(x):
    return jnp.minimum(x, 0.0) - jnp.log(1.0 + jnp.exp(-jnp.abs(x)))


def _modulated_norm(x, g, scale, shift):
    ms = jnp.mean(x * x, axis=-1, keepdims=True)
    return (x * lax.rsqrt(ms + EPS) * g) * (1.0 + scale) + shift


def _adaln_kernel(c_ref, w_ref, b_ref, o_ref):
    ca = _silu(c_ref[...]).astype(BF16)
    o_ref[0] = _dot(ca, w_ref[0].astype(BF16)) + b_ref[0]


def _adaln(c, w_ada, b_ada):
    L, D, W = w_ada.shape
    B = c.shape[0]
    tn = 1536
    return pl.pallas_call(
        _adaln_kernel,
        out_shape=jax.ShapeDtypeStruct((L, B, W), F32),
        grid=(L, W // tn),
        in_specs=[
            pl.BlockSpec((B, D), lambda l, j: (0, 0)),
            pl.BlockSpec((1, D, tn), lambda l, j: (l, 0, j)),
            pl.BlockSpec((1, 1, tn), lambda l, j: (l, 0, j)),
        ],
        out_specs=pl.BlockSpec((1, B, tn), lambda l, j: (l, 0, j)),
        compiler_params=pltpu.CompilerParams(
            dimension_semantics=("parallel", "parallel"), vmem_limit_bytes=VMEM_LIMIT),
        name="adaln",
    )(c, w_ada, b_ada.reshape(L, 1, W))


def _ffn_kernel(x_ref, g_ref, sh_ref, sc_ref, gt_ref, win_ref, wout_ref, o_ref, acc_ref):
    x = x_ref[...]
    hb = _modulated_norm(x, g_ref[...], sc_ref[0], sh_ref[0]).astype(BF16)
    n = wout_ref.shape[0]
    for j in range(n):
        a = _dot(hb, win_ref[j])
        b = _dot(hb, win_ref[n + j])
        y = _dot((_silu(a) * b).astype(BF16), wout_ref[j])
        if j == 0:
            acc_ref[...] = y
        else:
            acc_ref[...] += y
    o_ref[...] = x + (0.5 * gt_ref[0]) * acc_ref[...]


def _const_spec(shape):
    nd = len(shape)
    return pl.BlockSpec(shape, lambda *_: (0,) * nd, pipeline_mode=pl.Buffered(1))


def _row_spec(tm, w):
    return pl.BlockSpec((tm, w), lambda i: (i, 0))


def _batch_vec_spec(tiles_per_batch, w):
    return pl.BlockSpec((1, 1, w), lambda i: (i // tiles_per_batch, 0, 0))


def _ffn(x2, g, shift, scale, gate, w_in, w_out, seq):
    N, D = x2.shape
    nchunk = D_FF // FF_CHUNK
    win = w_in.reshape(D, 2 * nchunk, FF_CHUNK).transpose(1, 0, 2).astype(BF16)
    wout = w_out.reshape(nchunk, FF_CHUNK, D).astype(BF16)
    tm = TOKEN_TILE
    tpb = seq // tm
    return pl.pallas_call(
        _ffn_kernel,
        out_shape=jax.ShapeDtypeStruct((N, D), F32),
        grid=(N // tm,),
        in_specs=[
            _row_spec(tm, D),
            _const_spec((1, D)),
            _batch_vec_spec(tpb, D), _batch_vec_spec(tpb, D), _batch_vec_spec(tpb, D),
            _const_spec(win.shape), _const_spec(wout.shape),
        ],
        out_specs=_row_spec(tm, D),
        scratch_shapes=[pltpu.VMEM((tm, D), F32)],
        compiler_params=pltpu.CompilerParams(
            dimension_semantics=("parallel",), vmem_limit_bytes=VMEM_LIMIT),
        name="ffn",
    )(x2, g.reshape(1, D), shift, scale, gate, win, wout)


def _head_rms(x, seg2, gain, head_dim):
    ms = _split_dot(x * x, seg2) * (1.0 / head_dim)
    return x * lax.rsqrt(ms + EPS) * gain


def _inproj_kernel(x_ref, g_ref, sh_ref, sc_ref, w_ref, wup_ref, bg_ref, seg_ref, qn_ref, kn_ref,
                   gq_ref, gk_ref, ga_ref, gv_ref, gg_ref, sq_ref, sk_ref, sv_ref, cu_ref):
    hb = _modulated_norm(x_ref[...], g_ref[...], sc_ref[0], sh_ref[0]).astype(BF16)
    p = _dot(hb, w_ref[...])
    gq_ref[...] = p[:, _OFF_GQ:_OFF_GQ + QK_PAD]
    gk_ref[...] = p[:, _OFF_GK:_OFF_GK + QK_PAD]
    gv_ref[...] = p[:, _OFF_GV:_OFF_GV + GLA_V]
    gg_ref[...] = p[:, _OFF_GG:_OFF_GG + GLA_V]
    r = p[:, _OFF_GR:_OFF_GR + RANK_PAD].astype(BF16)
    ga_ref[...] = _log_sigmoid(_dot(r, wup_ref[...]) + bg_ref[...]) * (1.0 / GLA_TAU)
    seg = seg_ref[...]
    sq = _head_rms(p[:, _OFF_SQ:_OFF_SQ + SB_W], seg, qn_ref[...], SB_DH)
    sq_ref[...] = (sq * (SB_DH ** -0.5)).astype(BF16)
    sk_ref[...] = _head_rms(p[:, _OFF_SK:_OFF_SK + SB_W], seg, kn_ref[...], SB_DH).astype(BF16)
    sv_ref[...] = p[:, _OFF_SV:_OFF_SV + SB_W].astype(BF16)
    cu_ref[...] = p[:, _OFF_CA:_OFF_CA + CONV_CH] * _sigmoid(p[:, _OFF_CG:_OFF_CG + CONV_CH])


def _same_head(width, head_dim):
    h = np.arange(width) // head_dim
    m = (h[:, None] == h[None, :]).astype(np.float32)
    return jnp.asarray(np.concatenate([m, m], axis=0), BF16)


def _pad_cols(w, width):
    return jnp.pad(w, ((0, 0), (0, width - w.shape[1])))


def _inproj(x2, g, shift, scale, w_in, w_up, b_gate, q_norm, k_norm, seq):
    N, D = x2.shape
    o = 0
    parts = []
    for width, pad in ((GLA_QK, QK_PAD), (GLA_QK, QK_PAD), (GLA_V, GLA_V), (GLA_V, GLA_V),
                       (GLA_RANK, RANK_PAD), (SB_W, SB_W), (SB_W, SB_W), (SB_W, SB_W),
                       (CONV_CH, CONV_CH), (CONV_CH, CONV_CH)):
        parts.append(_pad_cols(w_in[:, o:o + width], pad))
        o += width
    w = jnp.concatenate(parts, axis=1).astype(BF16)
    wup = jnp.pad(w_up, ((0, RANK_PAD - GLA_RANK), (0, QK_PAD - GLA_QK))).astype(BF16)
    bg = jnp.pad(b_gate, (0, QK_PAD - GLA_QK)).reshape(1, QK_PAD)
    tm = TOKEN_TILE
    tpb = seq // tm
    seg = _same_head(SB_W, SB_DH)
    heads = lambda gn: jnp.tile(gn, SB_HEADS).reshape(1, SB_W)
    outs = ((QK_PAD, F32), (QK_PAD, F32), (QK_PAD, F32), (GLA_V, F32), (GLA_V, F32),
            (SB_W, BF16), (SB_W, BF16), (SB_W, BF16), (CONV_CH, F32))
    return pl.pallas_call(
        _inproj_kernel,
        out_shape=[jax.ShapeDtypeStruct((N, wd), dt) for wd, dt in outs],
        grid=(N // tm,),
        in_specs=[
            _row_spec(tm, D),
            _const_spec((1, D)),
            _batch_vec_spec(tpb, D), _batch_vec_spec(tpb, D),
            _const_spec(w.shape), _const_spec(wup.shape), _const_spec(bg.shape),
            _const_spec(seg.shape), _const_spec((1, SB_W)), _const_spec((1, SB_W)),
        ],
        out_specs=[_row_spec(tm, wd) for wd, _ in outs],
        compiler_params=pltpu.CompilerParams(
            dimension_semantics=("parallel",), vmem_limit_bytes=VMEM_LIMIT),
        name="inproj",
    )(x2, g.reshape(1, D), shift, scale, w, wup, bg, seg, heads(q_norm), heads(k_norm))


_GLA_LEVELS = 6


def _gla_constants():
    C = CHUNK
    t = np.arange(C)[:, None]
    j = np.arange(C)[None, :]
    mats = [j <= t, j > t]
    masks = [t == j]
    for l in range(1, _GLA_LEVELS + 1):
        n, m = 1 << l, 1 << (l - 1)
        ref = t // n * n + m - 1
        right = (t % n) >= m
        mats.append((right & (j > ref) & (j <= t)) | ((~right) & (j > t) & (j <= ref)))
        masks.append(((t // n) == (j // n)) & right & ((j % n) < m))
    prefix = np.concatenate(mats, axis=0).astype(np.float32)
    prefix = np.concatenate([prefix, prefix], axis=1)
    masks = np.stack([np.tile(mk, (1, GLA_HEADS)) for mk in masks]).astype(np.float32)
    kl = np.arange(QK_PAD)[None, :] // GLA_DK
    vl = np.arange(GLA_V)[:, None] // GLA_DV
    state_mask = (kl == vl).astype(np.float32)
    return prefix, masks, state_mask


def _gla_kernel(q_ref, k_ref, a_ref, v_ref, g_ref, pre_ref, msk_ref, smask_ref, seg_ref, on_ref,
                o_ref, st_ref):
    C = CHUNK

    @pl.when(pl.program_id(1) == 0)
    def _():
        st_ref[...] = jnp.zeros_like(st_ref)

    klane = lax.broadcasted_iota(jnp.int32, (1, QK_PAD), 1) // GLA_DK
    vlane = lax.broadcasted_iota(jnp.int32, (1, GLA_V), 1) // GLA_DV
    n_grp = GLA_GROUP

    def group(i, carry):
        base = i * (n_grp * C)
        rows = [pl.ds(pl.multiple_of(base + j * C, C), C) for j in range(n_grp)]
        qs = [q_ref[0, r, :] * (GLA_DK ** -0.5) for r in rows]
        ks = [k_ref[0, r, :] for r in rows]
        vs = [v_ref[0, r, :] for r in rows]

        es = []
        for r in rows:
            es.append(_split_dot_left(pre_ref[...], a_ref[0, r, :]))

        atts = []
        for q, k, e in zip(qs, ks, es):
            att = None
            for l in range(_GLA_LEVELS + 1):
                if l == 0:
                    qt, kt = q, k
                else:
                    f = jnp.exp(e[(l + 1) * C:(l + 2) * C])
                    qt, kt = q * f, k * f
                kst = jnp.concatenate(
                    [jnp.where(klane == h, kt, 0.0) for h in range(GLA_HEADS)], axis=0).astype(BF16)
                term = _dot_nt(qt.astype(BF16), kst) * msk_ref[l]
                att = term if att is None else att + term
            atts.append(att)

        intra, upds, qbs, decs = [], [], [], []
        for q, k, v, e, att in zip(qs, ks, vs, es, atts):
            vst = jnp.concatenate(
                [jnp.where(vlane == h, v, 0.0) for h in range(GLA_HEADS)], axis=0).astype(BF16)
            intra.append(_dot(att.astype(BF16), vst))
            kdec = (k * jnp.exp(e[C:2 * C])).astype(BF16)
            upds.append(_dot_tn(v.astype(BF16), kdec) * smask_ref[...])
            qbs.append((q * jnp.exp(e[0:C])).astype(BF16))
            decs.append(jnp.exp(e[C - 1:C]))

        st = st_ref[...]
        outs = []
        for o_intra, upd, qb, dec in zip(intra, upds, qbs, decs):
            outs.append(o_intra + _dot_nt(qb, st.astype(BF16)))
            st = st * dec + upd
        st_ref[...] = st

        o = jnp.concatenate(outs, axis=0)
        out_rows = pl.ds(pl.multiple_of(base, n_grp * C), n_grp * C)
        y = _head_rms(o, seg_ref[...], on_ref[...], GLA_DV)
        o_ref[0, out_rows, :] = y * _silu(g_ref[0, out_rows, :])
        return carry

    lax.fori_loop(0, q_ref.shape[1] // (n_grp * C), group, 0)


def _gla(gq, gk, ga, gv, gg, out_norm):
    B, T, _ = gq.shape
    prefix, masks, state_mask = _gla_constants()
    tt = GLA_TILE
    seq_spec = lambda w: pl.BlockSpec((1, tt, w), lambda b, i: (b, i, 0))
    consts = (jnp.asarray(prefix, BF16), jnp.asarray(masks), jnp.asarray(state_mask),
              _same_head(GLA_V, GLA_DV), jnp.tile(out_norm, GLA_HEADS).reshape(1, GLA_V))
    return pl.pallas_call(
        _gla_kernel,
        out_shape=jax.ShapeDtypeStruct((B, T, GLA_V), F32),
        grid=(B, T // tt),
        in_specs=[seq_spec(QK_PAD), seq_spec(QK_PAD), seq_spec(QK_PAD), seq_spec(GLA_V), seq_spec(GLA_V)]
                 + [_const_spec(cst.shape) for cst in consts],
        out_specs=seq_spec(GLA_V),
        scratch_shapes=[pltpu.VMEM((GLA_V, QK_PAD), F32)],
        compiler_params=pltpu.CompilerParams(
            dimension_semantics=("parallel", "arbitrary"), vmem_limit_bytes=VMEM_LIMIT),
        name="gla",
    )(gq, gk, ga, gv, gg, *consts)


def _sb_kernel(q_ref, k_ref, v_ref, og_ref, seg_ref, causal_ref, suf_ref, o_ref,
               kn_ref, vb_ref, acc_ref, carry_ref, alive_ref):
    T = q_ref.shape[1]
    BLK, G = SB_BLOCK, SB_GROUP
    n_q = T // BLK
    n_groups = n_q // G
    first = lax.broadcasted_iota(jnp.int32, (1, 2 * SB_DH), 1) < SB_DH
    top = lax.broadcasted_iota(jnp.int32, (2 * SB_DH, 1), 0) < SB_DH

    kn_ref[0:BLK, :] = jnp.zeros((BLK, 2 * SB_DH), BF16)
    vb_ref[0:BLK, :] = jnp.zeros((BLK, 2 * SB_DH), BF16)
    kn_ref[BLK:, :] = k_ref[0]
    vb_ref[BLK:, :] = v_ref[0]

    def stacked_queries(qi):
        q2 = q_ref[0, pl.ds(pl.multiple_of(qi * BLK, BLK), BLK), :]
        zq = jnp.zeros_like(q2)
        return jnp.concatenate([jnp.where(first, q2, zq), jnp.where(first, zq, q2)], axis=0)

    def group_step(grp, dist, causal):
        qis = [grp * G + g for g in range(G)]
        krows = [pl.ds(pl.multiple_of(jnp.maximum(qi - dist + 1, 0) * BLK, BLK), BLK) for qi in qis]
        zs = [_dot_nt(kn_ref[kr, :], stacked_queries(qi)) for qi, kr in zip(qis, krows)]
        log_betas, log_keeps = [], []
        for z in zs:
            l1p = jnp.log(1.0 + jnp.exp(-jnp.abs(z)))
            log_betas.append(jnp.minimum(z, 0.0) - l1p)
            lk = -jnp.maximum(z, 0.0) - l1p
            log_keeps.append(lk * causal_ref[...] if causal else lk)
        sufs = [_split_dot_left(suf_ref[...], lk) for lk in log_keeps]
        ws, alive = [], None
        for qi, lb, lk, suf in zip(qis, log_betas, log_keeps, sufs):
            carry = jnp.zeros((1, 2 * BLK), F32) if causal else carry_ref[qi]
            w = jnp.exp(lb + suf + carry)
            ws.append((w * causal_ref[...] if causal else w).astype(BF16))
            carry = carry + suf[0:1, :] + lk[0:1, :]
            carry_ref[qi] = carry
            alive = carry if alive is None else jnp.maximum(alive, carry)
        for qi, kr, w in zip(qis, krows, ws):
            upd = _dot_tn(vb_ref[kr, :], w)
            acc_ref[qi] = upd if causal else acc_ref[qi] + upd
        return jnp.max(alive) > -SB_SKIP

    def diagonal(grp, c0):
        alive_ref[grp] = group_step(grp, 0, True).astype(jnp.int32)
        return c0

    lax.fori_loop(0, n_groups, diagonal, 0)

    def any_alive():
        total = alive_ref[0]
        for g in range(1, n_groups):
            total = total + alive_ref[g]
        return total > 0

    def sweep(state):
        dist, _ = state

        def visit(grp, c0):
            run = jnp.logical_and(alive_ref[grp] > 0, grp * G + G - 1 >= dist)

            @pl.when(run)
            def _():
                alive_ref[grp] = group_step(grp, dist, False).astype(jnp.int32)

            @pl.when(jnp.logical_not(run))
            def _():
                alive_ref[grp] = 0

            return c0

        lax.fori_loop(0, n_groups, visit, 0)
        return dist + 1, any_alive()

    lax.while_loop(lambda s: jnp.logical_and(s[0] < n_q, s[1]), sweep, (1, any_alive()))

    def finish(grp, c0):
        cols = pl.ds(pl.multiple_of(grp * G * BLK, G * BLK), G * BLK)
        ot = jnp.concatenate(
            [jnp.where(top, acc_ref[grp * G + g][:, :BLK], acc_ref[grp * G + g][:, BLK:]) for g in range(G)],
            axis=1)
        o_ref[0, cols, :] = _head_rms(ot.T, seg_ref[...], og_ref[...], SB_DH)
        return c0

    lax.fori_loop(0, n_groups, finish, 0)


def _sb_constants():
    BLK = SB_BLOCK
    s = np.arange(BLK)[:, None]
    j = np.arange(BLK)[None, :]
    suffix = (j > s).astype(np.float32)
    suffix = np.concatenate([suffix, suffix], axis=1)
    causal =(s < np.tile(np.arange(BLK), 2)[None, :]).astype(np.float32)
    return causal, suffix


def _sb(sq, sk, sv, out_norm):
    B, T, _ = sq.shape
    causal, suffix = _sb_constants()
    consts = (jnp.tile(out_norm, 2).reshape(1, 2 * SB_DH), _same_head(2 * SB_DH, SB_DH),
              jnp.asarray(causal), jnp.asarray(suffix, BF16))
    spec = pl.BlockSpec((1, T, 2 * SB_DH), lambda b, p: (b, 0, p))
    n_q = T // SB_BLOCK
    return pl.pallas_call(
        _sb_kernel,
        out_shape=jax.ShapeDtypeStruct((B, T, SB_W), F32),
        grid=(B, SB_HEADS // 2),
        in_specs=[spec, spec, spec] + [_const_spec(cst.shape) for cst in consts],
        out_specs=spec,
        scratch_shapes=[pltpu.VMEM((T + SB_BLOCK, 2 * SB_DH), BF16), pltpu.VMEM((T + SB_BLOCK, 2 * SB_DH), BF16),
                        pltpu.VMEM((n_q, 2 * SB_DH, 2 * SB_BLOCK), F32), pltpu.VMEM((n_q, 1, 2 * SB_BLOCK), F32),
                        pltpu.SMEM((n_q // SB_GROUP,), jnp.int32)],
        compiler_params=pltpu.CompilerParams(
            dimension_semantics=("parallel", "parallel"), vmem_limit_bytes=VMEM_LIMIT),
        name="stickbreak",
    )(sq, sk, sv, *consts)


def _conv_kernel(u_ref, w_ref, b_ref, lg_ref, lb_ref, o_ref, pad_ref):
    T = u_ref.shape[1]
    pad_ref[0:CONV_PAD, :] = jnp.zeros((CONV_PAD, CONV_CH), F32)
    pad_ref[CONV_PAD:, :] = u_ref[0]
    first_tap = CONV_PAD - (CONV_WIDTH - 1)
    for i in range(T // CONV_TILE):
        base = i * CONV_TILE + first_tap
        acc = jnp.zeros((CONV_TILE, CONV_CH), F32)
        for j in range(CONV_WIDTH):
            acc = acc + w_ref[j:j + 1, :] * pad_ref[base + j:base + j + CONV_TILE, :]
        acc = acc + b_ref[...]
        mu = jnp.mean(acc, axis=-1, keepdims=True)
        xc = acc - mu
        var = jnp.mean(xc * xc, axis=-1, keepdims=True)
        y = xc * lax.rsqrt(var + EPS) * lg_ref[...] + lb_ref[...]
        o_ref[0, i * CONV_TILE:(i + 1) * CONV_TILE, :] = _silu(y)


def _conv(cu, w, b, ln_g, ln_b):
    B, T, C = cu.shape
    spec = pl.BlockSpec((1, T, C), lambda bi: (bi, 0, 0))
    vec = lambda a: a.reshape(1, C)
    return pl.pallas_call(
        _conv_kernel,
        out_shape=jax.ShapeDtypeStruct((B, T, C), F32),
        grid=(B,),
        in_specs=[spec, _const_spec(w.shape), _const_spec((1, C)), _const_spec((1, C)), _const_spec((1, C))],
        out_specs=spec,
        scratch_shapes=[pltpu.VMEM((T + CONV_PAD, C), F32)],
        compiler_params=pltpu.CompilerParams(
            dimension_semantics=("parallel",), vmem_limit_bytes=VMEM_LIMIT),
        name="conv",
    )(cu, w, vec(b), vec(ln_g), vec(ln_b))


def _outproj_kernel(x_ref, a_ref, b_ref, c_ref, gt_ref, wa_ref, wb_ref, wc_ref, o_ref):
    y = (_dot(a_ref[...].astype(BF16), wa_ref[...]) + _dot(b_ref[...].astype(BF16), wb_ref[...])
         + _dot(c_ref[...].astype(BF16), wc_ref[...]))
    o_ref[...] = x_ref[...] + gt_ref[0] * y


def _outproj(x2, oa, ob, oc, gate, w_out, seq):
    N, D = x2.shape
    wb16 = w_out.astype(BF16)
    wa, wb, wc = wb16[:GLA_V], wb16[GLA_V:GLA_V + SB_W], wb16[GLA_V + SB_W:]
    tm = TOKEN_TILE
    tpb = seq // tm
    return pl.pallas_call(
        _outproj_kernel,
        out_shape=jax.ShapeDtypeStruct((N, D), F32),
        grid=(N // tm,),
        in_specs=[
            _row_spec(tm, D), _row_spec(tm, GLA_V), _row_spec(tm, SB_W), _row_spec(tm, CONV_CH),
            _batch_vec_spec(tpb, D),
            _const_spec(wa.shape), _const_spec(wb.shape), _const_spec(wc.shape),
        ],
        out_specs=_row_spec(tm, D),
        compiler_params=pltpu.CompilerParams(
            dimension_semantics=("parallel",), vmem_limit_bytes=VMEM_LIMIT),
        name="outproj",
    )(x2, oa, ob, oc, gate, wa, wb, wc)


def kernel(x, c, w_ada, b_ada, norm_ffn1, ffn1_w_in, ffn1_w_out, norm_mix, w_in, w_out, gla_w_gate_up, gla_b_gate, gla_out_norm, sb_q_norm, sb_k_norm, sb_out_norm, conv_w, conv_b, conv_ln_g, conv_ln_b, norm_ffn2, ffn2_w_in, ffn2_w_out):
    B, T, D = x.shape
    L = w_ada.shape[0]
    mod = _adaln(c, w_ada, b_ada).reshape(L, B, N_MOD, 1, D)
    x2 = x.reshape(B * T, D)
    for l in range(L):
        sh1, sc1, gt1, sh2, sc2, gt2, sh3, sc3, gt3 = (mod[l, :, i] for i in range(N_MOD))
        x2 = _ffn(x2, norm_ffn1[l], sh1, sc1, gt1, ffn1_w_in[l], ffn1_w_out[l], T)
        gq, gk, ga, gv, gg, sq, sk, sv, cu = _inproj(
            x2, norm_mix[l], sh2, sc2, w_in[l], gla_w_gate_up[l], gla_b_gate[l], sb_q_norm[l], sb_k_norm[l], T)
        seq = lambda a: a.reshape(B, T, a.shape[-1])
        oa = _gla(seq(gq), seq(gk), seq(ga), seq(gv), seq(gg), gla_out_norm[l])
        ob = _sb(seq(sq), seq(sk), seq(sv), sb_out_norm[l])
        oc = _conv(seq(cu), conv_w[l], conv_b[l], conv_ln_g[l], conv_ln_b[l])
        x2 = _outproj(x2, oa.reshape(B * T, GLA_V), ob.reshape(B * T, SB_W), oc.reshape(B * T, CONV_CH),
                      gt2, w_out[l], T)
        x2 = _ffn(x2, norm_ffn2[l], sh3, sc3, gt3, ffn2_w_in[l], ffn2_w_out[l], T)
    return x2.reshape(B, T, D)
```

```python
import functools

import numpy as np
import jax
import jax.numpy as jnp
from jax import lax
from jax.experimental import pallas as pl
from jax.experimental.pallas import tpu as pltpu

F32 = jnp.float32
BF16 = jnp.bfloat16

EPS = 1e-6
CHUNK = 64
GLA_HEADS, GLA_DK, GLA_DV, GLA_RANK, GLA_TAU = 4, 48, 96, 16, 16.0
SB_HEADS, SB_DH = 6, 64
CONV_CH, CONV_WIDTH = 256, 31
D_FF = 2816
N_MOD = 9
GLA_QK = GLA_HEADS * GLA_DK
GLA_V = GLA_HEADS * GLA_DV
SB_W = SB_HEADS * SB_DH

LANES = 128
SUBLANES = 8
QK_PAD = 256
RANK_PAD = 128
FF_CHUNK = 256
TOKEN_TILE = 512
SB_BLOCK = 128
SB_GROUP = 8
SB_SKIP = 104.0
GLA_TILE = 512
GLA_GROUP = 4
CONV_TILE = 256
CONV_PAD = 32
VMEM_LIMIT = 52 * 1024 * 1024

_OFF_GQ, _OFF_GK, _OFF_GV, _OFF_GG, _OFF_GR = 0, 256, 512, 896, 1280
_OFF_SQ, _OFF_SK, _OFF_SV, _OFF_CA, _OFF_CG = 1408, 1792, 2176, 2560, 2816
PROJ_W = 3072


def _dot(a, b):
    return jnp.dot(a, b, preferred_element_type=F32)


def _dot_nt(a, b):
    return lax.dot_general(a, b, (((1,), (1,)), ((), ())), preferred_element_type=F32)


def _dot_tn(a, b):
    return lax.dot_general(a, b, (((0,), (0,)), ((), ())), preferred_element_type=F32)


def _hi_lo(x, axis):
    hi = x.astype(BF16)
    lo = (x - hi.astype(F32)).astype(BF16)
    return jnp.concatenate([hi, lo], axis=axis)


def _split_dot(x, m2):
    return _dot(_hi_lo(x, 1), m2)


def _split_dot_left(m2, x):
    return _dot(m2, _hi_lo(x, 0))


def _sigmoid(x):
    return 1.0 / (1.0 + jnp.exp(-x))


def _silu(x):
    return x * _sigmoid(x)


def _log_sigmoid(x):
    return jnp.minimum(x, 0.0) - jnp.log(1.0 + jnp.exp(-jnp.abs(x)))


def _modulated_norm(x, g, scale, shift):
    ms = jnp.mean(x * x, axis=-1, keepdims=True)
    return (x * lax.rsqrt(ms + EPS) * g) * (1.0 + scale) + shift


def _adaln_kernel(c_ref, w_ref, b_ref, o_ref):
    ca = _silu(c_ref[...]).astype(BF16)
    o_ref[0] = _dot(ca, w_ref[0].astype(BF16)) + b_ref[0]


def _adaln(c, w_ada, b_ada):
    L, D, W = w_ada.shape
    B = c.shape[0]
    tn = 1536
    return pl.pallas_call(
        _adaln_kernel,
        out_shape=jax.ShapeDtypeStruct((L, B, W), F32),
        grid=(L, W // tn),
        in_specs=[
            pl.BlockSpec((B, D), lambda l, j: (0, 0)),
            pl.BlockSpec((1, D, tn), lambda l, j: (l, 0, j)),
            pl.BlockSpec((1, 1, tn), lambda l, j: (l, 0, j)),
        ],
        out_specs=pl.BlockSpec((1, B, tn), lambda l, j: (l, 0, j)),
        compiler_params=pltpu.CompilerParams(
            dimension_semantics=("parallel", "parallel"), vmem_limit_bytes=VMEM_LIMIT),
        name="adaln",
    )(c, w_ada, b_ada.reshape(L, 1, W))


def _ffn_kernel(x_ref, g_ref, sh_ref, sc_ref, gt_ref, win_ref, wout_ref, o_ref, acc_ref):
    x = x_ref[...]
    hb = _modulated_norm(x, g_ref[...], sc_ref[0], sh_ref[0]).astype(BF16)
    d_ff = wout_ref.shape[0]
    for j in range(d_ff // FF_CHUNK):
        cols = slice(j * FF_CHUNK, (j + 1) * FF_CHUNK)
        a = _dot(hb, win_ref[:, cols])
        b = _dot(hb, win_ref[:, d_ff + j * FF_CHUNK:d_ff + (j + 1) * FF_CHUNK])
        y = _dot((_silu(a) * b).astype(BF16), wout_ref[cols, :])
        if j == 0:
            acc_ref[...] = y
        else:
            acc_ref[...] += y
    o_ref[...] = x + (0.5 * gt_ref[0]) * acc_ref[...]


def _const_spec(shape):
    nd = len(shape)
    return pl.BlockSpec(shape, lambda *_: (0,) * nd, pipeline_mode=pl.Buffered(1))


def _row_spec(tm, w):
    return pl.BlockSpec((tm, w), lambda i: (i, 0))


def _batch_vec_spec(tiles_per_batch, w):
    return pl.BlockSpec((1, 1, w), lambda i: (i // tiles_per_batch, 0, 0))


def _ffn(x2, g, shift, scale, gate, w_in, w_out, seq):
    N, D = x2.shape
    win = w_in.astype(BF16)
    wout = w_out.astype(BF16)
    tm = TOKEN_TILE
    tpb = seq // tm
    return pl.pallas_call(
        _ffn_kernel,
        out_shape=jax.ShapeDtypeStruct((N, D), F32),
        grid=(N // tm,),
        in_specs=[
            _row_spec(tm, D),
            _const_spec((1, D)),
            _batch_vec_spec(tpb, D), _batch_vec_spec(tpb, D), _batch_vec_spec(tpb, D),
            _const_spec(win.shape), _const_spec(wout.shape),
        ],
        out_specs=_row_spec(tm, D),
        scratch_shapes=[pltpu.VMEM((tm, D), F32)],
        compiler_params=pltpu.CompilerParams(
            dimension_semantics=("parallel",), vmem_limit_bytes=VMEM_LIMIT),
        name="ffn",
    )(x2, g.reshape(1, D), shift, scale, gate, win, wout)


def _head_rms(x, seg2, gain, head_dim):
    ms = _split_dot(x * x, seg2) * (1.0 / head_dim)
    return x * lax.rsqrt(ms + EPS) * gain


def _inproj_kernel(x_ref, g_ref, sh_ref, sc_ref, w_ref, wup_ref, bg_ref, seg_ref, qn_ref, kn_ref,
                   gq_ref, gk_ref, ga_ref, gv_ref, gg_ref, sq_ref, sk_ref, sv_ref, cu_ref):
    hb = _modulated_norm(x_ref[...], g_ref[...], sc_ref[0], sh_ref[0]).astype(BF16)
    p = _dot(hb, w_ref[...])
    gq_ref[...] = p[:, _OFF_GQ:_OFF_GQ + QK_PAD]
    gk_ref[...] = p[:, _OFF_GK:_OFF_GK + QK_PAD]
    gv_ref[...] = p[:, _OFF_GV:_OFF_GV + GLA_V]
    gg_ref[...] = p[:, _OFF_GG:_OFF_GG + GLA_V]
    r = p[:, _OFF_GR:_OFF_GR + RANK_PAD].astype(BF16)
    ga_ref[...] = _log_sigmoid(_dot(r, wup_ref[...]) + bg_ref[...]) * (1.0 / GLA_TAU)
    seg = seg_ref[...]
    sq = _head_rms(p[:, _OFF_SQ:_OFF_SQ + SB_W], seg, qn_ref[...], SB_DH)
    sq_ref[...] = (sq * (SB_DH ** -0.5)).astype(BF16)
    sk_ref[...] = _head_rms(p[:, _OFF_SK:_OFF_SK + SB_W], seg, kn_ref[...], SB_DH).astype(BF16)
    sv_ref[...] = p[:, _OFF_SV:_OFF_SV + SB_W].astype(BF16)
    cu_ref[...] = p[:, _OFF_CA:_OFF_CA + CONV_CH] * _sigmoid(p[:, _OFF_CG:_OFF_CG + CONV_CH])


def _same_head(width, head_dim):
    h = np.arange(width) // head_dim
    m = (h[:, None] == h[None, :]).astype(np.float32)
    return jnp.asarray(np.concatenate([m, m], axis=0), BF16)


def _pad_cols(w, width):
    return jnp.pad(w, ((0, 0), (0, width - w.shape[1])))


def _inproj(x2, g, shift, scale, w_in, w_up, b_gate, q_norm, k_norm, seq):
    N, D = x2.shape
    o = 0
    parts = []
    for width, pad in ((GLA_QK, QK_PAD), (GLA_QK, QK_PAD), (GLA_V, GLA_V), (GLA_V, GLA_V),
                       (GLA_RANK, RANK_PAD), (SB_W, SB_W), (SB_W, SB_W), (SB_W, SB_W),
                       (CONV_CH, CONV_CH), (CONV_CH, CONV_CH)):
        parts.append(_pad_cols(w_in[:, o:o + width], pad))
        o += width
    w = jnp.concatenate(parts, axis=1).astype(BF16)
    wup = jnp.pad(w_up, ((0, RANK_PAD - GLA_RANK), (0, QK_PAD - GLA_QK))).astype(BF16)
    bg = jnp.pad(b_gate, (0, QK_PAD - GLA_QK)).reshape(1, QK_PAD)
    tm = TOKEN_TILE
    tpb = seq // tm
    seg = _same_head(SB_W, SB_DH)
    heads = lambda gn: jnp.tile(gn, SB_HEADS).reshape(1, SB_W)
    outs = ((QK_PAD, F32), (QK_PAD, F32), (QK_PAD, F32), (GLA_V, F32), (GLA_V, F32),
            (SB_W, BF16), (SB_W, BF16), (SB_W, BF16), (CONV_CH, F32))
    return pl.pallas_call(
        _inproj_kernel,
        out_shape=[jax.ShapeDtypeStruct((N, wd), dt) for wd, dt in outs],
        grid=(N // tm,),
        in_specs=[
            _row_spec(tm, D),
            _const_spec((1, D)),
            _batch_vec_spec(tpb, D), _batch_vec_spec(tpb, D),
            _const_spec(w.shape), _const_spec(wup.shape), _const_spec(bg.shape),
            _const_spec(seg.shape), _const_spec((1, SB_W)), _const_spec((1, SB_W)),
        ],
        out_specs=[_row_spec(tm, wd) for wd, _ in outs],
        compiler_params=pltpu.CompilerParams(
            dimension_semantics=("parallel",), vmem_limit_bytes=VMEM_LIMIT),
        name="inproj",
    )(x2, g.reshape(1, D), shift, scale, w, wup, bg, seg, heads(q_norm), heads(k_norm))


_GLA_LEVELS = 6


def _gla_constants():
    C = CHUNK
    t = np.arange(C)[:, None]
    j = np.arange(C)[None, :]
    mats = [j <= t, j > t]
    masks = [t == j]
    for l in range(1, _GLA_LEVELS + 1):
        n, m = 1 << l, 1 << (l - 1)
        ref = t // n * n + m - 1
        right = (t % n) >= m
        mats.append((right & (j > ref) & (j <= t)) | ((~right) & (j > t) & (j <= ref)))
        masks.append(((t // n) == (j // n)) & right & ((j % n) < m))
    prefix = np.concatenate(mats, axis=0).astype(np.float32)
    prefix = np.concatenate([prefix, prefix], axis=1)
    masks = np.stack([np.tile(mk, (1, GLA_HEADS)) for mk in masks]).astype(np.float32)
    kl = np.arange(QK_PAD)[None, :] // GLA_DK
    vl = np.arange(GLA_V)[:, None] // GLA_DV
    state_mask = (kl == vl).astype(np.float32)
    return prefix, masks, state_mask


def _gla_kernel(q_ref, k_ref, a_ref, v_ref, g_ref, pre_ref, msk_ref, smask_ref, seg_ref, on_ref,
                o_ref, st_ref):
    C = CHUNK

    @pl.when(pl.program_id(1) == 0)
    def _():
        st_ref[...] = jnp.zeros_like(st_ref)

    klane = lax.broadcasted_iota(jnp.int32, (1, QK_PAD), 1) // GLA_DK
    vlane = lax.broadcasted_iota(jnp.int32, (1, GLA_V), 1) // GLA_DV
    n_grp = GLA_GROUP

    def group(i, carry):
        base = i * (n_grp * C)
        rows = [pl.ds(pl.multiple_of(base + j * C, C), C) for j in range(n_grp)]
        qs = [q_ref[0, r, :] * (GLA_DK ** -0.5) for r in rows]
        ks = [k_ref[0, r, :] for r in rows]
        vs = [v_ref[0, r, :] for r in rows]

        es = []
        for r in rows:
            es.append(_split_dot_left(pre_ref[...], a_ref[0, r, :]))

        atts = []
        for q, k, e in zip(qs, ks, es):
            att = None
            for l in range(_GLA_LEVELS + 1):
                if l == 0:
                    qt, kt = q, k
                else:
                    f = jnp.exp(e[(l + 1) * C:(l + 2) * C])
                    qt, kt = q * f, k * f
                kst = jnp.concatenate(
                    [jnp.where(klane == h, kt, 0.0) for h in range(GLA_HEADS)], axis=0).astype(BF16)
                term = _dot_nt(qt.astype(BF16), kst) * msk_ref[l]
                att = term if att is None else att + term
            atts.append(att)

        intra, upds, qbs, decs = [], [], [], []
        for q, k, v, e, att in zip(qs, ks, vs, es, atts):
            vst = jnp.concatenate(
                [jnp.where(vlane == h, v, 0.0) for h in range(GLA_HEADS)], axis=0).astype(BF16)
            intra.append(_dot(att.astype(BF16), vst))
            kdec = (k * jnp.exp(e[C:2 * C])).astype(BF16)
            upds.append(_dot_tn(v.astype(BF16), kdec) * smask_ref[...])
            qbs.append((q * jnp.exp(e[0:C])).astype(BF16))
            decs.append(jnp.exp(e[C - 1:C]))

        st = st_ref[...]
        outs = []
        for o_intra, upd, qb, dec in zip(intra, upds, qbs, decs):
            outs.append(o_intra + _dot_nt(qb, st.astype(BF16)))
            st = st * dec + upd
        st_ref[...] = st

        o = jnp.concatenate(outs, axis=0)
        out_rows = pl.ds(pl.multiple_of(base, n_grp * C), n_grp * C)
        y = _head_rms(o, seg_ref[...], on_ref[...], GLA_DV)
        o_ref[0, out_rows, :] = y * _silu(g_ref[0, out_rows, :])
        return carry

    lax.fori_loop(0, q_ref.shape[1] // (n_grp * C), group, 0)


def _gla(gq, gk, ga, gv, gg, out_norm):
    B, T, _ = gq.shape
    prefix, masks, state_mask = _gla_constants()
    tt = GLA_TILE
    seq_spec = lambda w: pl.BlockSpec((1, tt, w), lambda b, i: (b, i, 0))
    consts = (jnp.asarray(prefix, BF16), jnp.asarray(masks), jnp.asarray(state_mask),
              _same_head(GLA_V, GLA_DV), jnp.tile(out_norm, GLA_HEADS).reshape(1, GLA_V))
    return pl.pallas_call(
        _gla_kernel,
        out_shape=jax.ShapeDtypeStruct((B, T, GLA_V), F32),
        grid=(B, T // tt),
        in_specs=[seq_spec(QK_PAD), seq_spec(QK_PAD), seq_spec(QK_PAD), seq_spec(GLA_V), seq_spec(GLA_V)]
                 + [_const_spec(cst.shape) for cst in consts],
        out_specs=seq_spec(GLA_V),
        scratch_shapes=[pltpu.VMEM((GLA_V, QK_PAD), F32)],
        compiler_params=pltpu.CompilerParams(
            dimension_semantics=("parallel", "arbitrary"), vmem_limit_bytes=VMEM_LIMIT),
        name="gla",
    )(gq, gk, ga, gv, gg, *consts)


def _sb_kernel(q_ref, k_ref, v_ref, og_ref, seg_ref, causal_ref, suf_ref, o_ref,
               kn_ref, vb_ref, acc_ref, carry_ref, alive_ref):
    T = q_ref.shape[1]
    BLK, G = SB_BLOCK, SB_GROUP
    n_q = T // BLK
    n_groups = n_q // G
    first = lax.broadcasted_iota(jnp.int32, (1, 2 * SB_DH), 1) < SB_DH
    top = lax.broadcasted_iota(jnp.int32, (2 * SB_DH, 1), 0) < SB_DH

    kn_ref[0:BLK, :] = jnp.zeros((BLK, 2 * SB_DH), BF16)
    vb_ref[0:BLK, :] = jnp.zeros((BLK, 2 * SB_DH), BF16)
    kn_ref[BLK:, :] = k_ref[0]
    vb_ref[BLK:, :] = v_ref[0]

    def stacked_queries(qi):
        q2 = q_ref[0, pl.ds(pl.multiple_of(qi * BLK, BLK), BLK), :]
        zq = jnp.zeros_like(q2)
        return jnp.concatenate([jnp.where(first, q2, zq), jnp.where(first, zq, q2)], axis=0)

    def group_step(grp, dist, causal):
        qis = [grp * G + g for g in range(G)]
        krows = [pl.ds(pl.multiple_of(jnp.maximum(qi - dist + 1, 0) * BLK, BLK), BLK) for qi in qis]
        zs = [_dot_nt(kn_ref[kr, :], stacked_queries(qi)) for qi, kr in zip(qis, krows)]
        log_betas, drops = [], []
        for z in zs:
            sp = jnp.maximum(z, 0.0) + jnp.log(1.0 + jnp.exp(-jnp.abs(z)))
            log_betas.append(z - sp)
            drops.append(sp * causal_ref[...] if causal else sp)
        sufs = [_split_dot_left(suf_ref[...], d) for d in drops]
        ws, alive = [], None
        for qi, lb, d, suf in zip(qis, log_betas, drops, sufs):
            carry = jnp.zeros((1, 2 * BLK), F32) if causal else carry_ref[qi]
            w = jnp.exp(lb - suf + carry)
            ws.append((w * causal_ref[...] if causal else w).astype(BF16))
            carry = carry - (suf[0:1, :] + d[0:1, :])
            carry_ref[qi] = carry
            alive = carry if alive is None else jnp.maximum(alive, carry)
        for qi, kr, w in zip(qis, krows, ws):
            upd = _dot_tn(vb_ref[kr, :], w)
            acc_ref[qi] = upd if causal else acc_ref[qi] + upd
        return jnp.max(alive) > -SB_SKIP

    def diagonal(grp, c0):
        alive_ref[grp] = group_step(grp, 0, True).astype(jnp.int32)
        return c0

    lax.fori_loop(0, n_groups, diagonal, 0)

    def any_alive():
        total = alive_ref[0]
        for g in range(1, n_groups):
            total = total + alive_ref[g]
        return total > 0

    def sweep(state):
        dist, _ = state

        def visit(grp, c0):
            run = jnp.logical_and(alive_ref[grp] > 0, grp * G + G - 1 >= dist)

            @pl.when(run)
            def _():
                alive_ref[grp] = group_step(grp, dist, False).astype(jnp.int32)

            @pl.when(jnp.logical_not(run))
            def _():
                alive_ref[grp] = 0

            return c0

        lax.fori_loop(0, n_groups, visit, 0)
        return dist + 1, any_alive()

    lax.while_loop(lambda s: jnp.logical_and(s[0] < n_q, s[1]), sweep, (1, any_alive()))

    def finish(grp, c0):
        cols = pl.ds(pl.multiple_of(grp * G * BLK, G * BLK), G * BLK)
        ot = jnp.concatenate(
            [jnp.where(top, acc_ref[grp * G + g][:, :BLK], acc_ref[grp * G + g][:, BLK:]) for g in range(G)],
            axis=1)
        o_ref[0, cols, :] = _head_rms(ot.T, seg_ref[...], og_ref[...], SB_DH)
        return c0

    lax.fori_loop(0, n_groups, finish, 0)


def _sb_constants():
    BLK = SB_BLOCK
    s = np.arange(BLK)[:, None]
    j = np.arange(BLK)[None, :]
    suffix = (j > s).astype(np.float32)
    suffix = np.concatenate([suffix, suffix], axis=1)
    causal =(s < np.tile(np.arange(BLK), 2)[None, :]).astype(np.float32)
    return causal, suffix


def _sb(sq, sk, sv, out_norm):
    B, T, _ = sq.shape
    causal, suffix = _sb_constants()
    consts = (jnp.tile(out_norm, 2).reshape(1, 2 * SB_DH), _same_head(2 * SB_DH, SB_DH),
              jnp.asarray(causal), jnp.asarray(suffix, BF16))
    spec = pl.BlockSpec((1, T, 2 * SB_DH), lambda b, p: (b, 0, p))
    n_q = T // SB_BLOCK
    return pl.pallas_call(
        _sb_kernel,
        out_shape=jax.ShapeDtypeStruct((B, T, SB_W), F32),
        grid=(B, SB_HEADS // 2),
        in_specs=[spec, spec, spec] + [_const_spec(cst.shape) for cst in consts],
        out_specs=spec,
        scratch_shapes=[pltpu.VMEM((T + SB_BLOCK, 2 * SB_DH), BF16), pltpu.VMEM((T + SB_BLOCK, 2 * SB_DH), BF16),
                        pltpu.VMEM((n_q, 2 * SB_DH, 2 * SB_BLOCK), F32), pltpu.VMEM((n_q, 1, 2 * SB_BLOCK), F32),
                        pltpu.SMEM((n_q // SB_GROUP,), jnp.int32)],
        compiler_params=pltpu.CompilerParams(
            dimension_semantics=("parallel", "parallel"), vmem_limit_bytes=VMEM_LIMIT),
        name="stickbreak",
    )(sq, sk, sv, *consts)


def _conv_kernel(u_ref, w_ref, b_ref, lg_ref, lb_ref, o_ref, pad_ref, sh_ref):
    T = u_ref.shape[1]
    S = SUBLANES
    pad_ref[0:CONV_PAD, :] = jnp.zeros((CONV_PAD, CONV_CH), F32)
    pad_ref[CONV_PAD:, :] = u_ref[0]
    for r in range(1, S):
        sh_ref[r - 1, S:, :] = pad_ref[S - r:T + CONV_PAD - r, :]
    for i in range(T // CONV_TILE):
        acc = jnp.zeros((CONV_TILE, CONV_CH), F32)
        for j in range(CONV_WIDTH):
            a, r = divmod(CONV_WIDTH - 1 - j, S)
            start = i * CONV_TILE + CONV_PAD - S * a
            src = pad_ref[start:start + CONV_TILE, :] if r == 0 else sh_ref[r - 1, start:start + CONV_TILE, :]
            acc = acc + w_ref[j:j + 1, :] * src
        acc = acc + b_ref[...]
        mu = jnp.mean(acc, axis=-1, keepdims=True)
        xc = acc - mu
        var = jnp.mean(xc * xc, axis=-1, keepdims=True)
        y = xc * lax.rsqrt(var + EPS) * lg_ref[...] + lb_ref[...]
        o_ref[0, i * CONV_TILE:(i + 1) * CONV_TILE, :] = _silu(y)


def _conv(cu, w, b, ln_g, ln_b):
    B, T, C = cu.shape
    spec = pl.BlockSpec((1, T, C), lambda bi: (bi, 0, 0))
    vec = lambda a: a.reshape(1, C)
    return pl.pallas_call(
        _conv_kernel,
        out_shape=jax.ShapeDtypeStruct((B, T, C), F32),
        grid=(B,),
        in_specs=[spec, _const_spec(w.shape), _const_spec((1, C)), _const_spec((1, C)), _const_spec((1, C))],
        out_specs=spec,
        scratch_shapes=[pltpu.VMEM((T + CONV_PAD, C), F32), pltpu.VMEM((SUBLANES - 1, T + CONV_PAD, C), F32)],
        compiler_params=pltpu.CompilerParams(
            dimension_semantics=("parallel",), vmem_limit_bytes=VMEM_LIMIT),
        name="conv",
    )(cu, w, vec(b), vec(ln_g), vec(ln_b))


def _outproj_kernel(x_ref, a_ref, b_ref, c_ref, gt_ref, wa_ref, wb_ref, wc_ref, o_ref):
    y = (_dot(a_ref[...].astype(BF16), wa_ref[...]) + _dot(b_ref[...].astype(BF16), wb_ref[...])
         + _dot(c_ref[...].astype(BF16), wc_ref[...]))
    o_ref[...] = x_ref[...] + gt_ref[0] * y


def _outproj(x2, oa, ob, oc, gate, w_out, seq):
    N, D = x2.shape
    wb16 = w_out.astype(BF16)
    wa, wb, wc = wb16[:GLA_V], wb16[GLA_V:GLA_V + SB_W], wb16[GLA_V + SB_W:]
    tm = TOKEN_TILE
    tpb = seq // tm
    return pl.pallas_call(
        _outproj_kernel,
        out_shape=jax.ShapeDtypeStruct((N, D), F32),
        grid=(N // tm,),
        in_specs=[
            _row_spec(tm, D), _row_spec(tm, GLA_V), _row_spec(tm, SB_W), _row_spec(tm, CONV_CH),
            _batch_vec_spec(tpb, D),
            _const_spec(wa.shape), _const_spec(wb.shape), _const_spec(wc.shape),
        ],
        out_specs=_row_spec(tm, D),
        compiler_params=pltpu.CompilerParams(
            dimension_semantics=("parallel",), vmem_limit_bytes=VMEM_LIMIT),
        name="outproj",
    )(x2, oa, ob, oc, gate, wa, wb, wc)


def kernel(x, c, w_ada, b_ada, norm_ffn1, ffn1_w_in, ffn1_w_out, norm_mix, w_in, w_out, gla_w_gate_up, gla_b_gate, gla_out_norm, sb_q_norm, sb_k_norm, sb_out_norm, conv_w, conv_b, conv_ln_g, conv_ln_b, norm_ffn2, ffn2_w_in, ffn2_w_out):
    B, T, D = x.shape
    L = w_ada.shape[0]
    mod = _adaln(c, w_ada, b_ada).reshape(L, B, N_MOD, 1, D)
    x2 = x.reshape(B * T, D)
    for l in range(L):
        sh1, sc1, gt1, sh2, sc2, gt2, sh3, sc3, gt3 = (mod[l, :, i] for i in range(N_MOD))
        x2 = _ffn(x2, norm_ffn1[l], sh1, sc1, gt1, ffn1_w_in[l], ffn1_w_out[l], T)
        gq, gk, ga, gv, gg, sq, sk, sv, cu = _inproj(
            x2, norm_mix[l], sh2, sc2, w_in[l], gla_w_gate_up[l], gla_b_gate[l], sb_q_norm[l], sb_k_norm[l], T)
        seq = lambda a: a.reshape(B, T, a.shape[-1])
        oa = _gla(seq(gq), seq(gk), seq(ga), seq(gv), seq(gg), gla_out_norm[l])
        ob = _sb(seq(sq), seq(sk), seq(sv), sb_out_norm[l])
        oc = _conv(seq(cu), conv_w[l], conv_b[l], conv_ln_g[l], conv_ln_b[l])
        x2 = _outproj(x2, oa.reshape(B * T, GLA_V), ob.reshape(B * T, SB_W), oc.reshape(B * T, CONV_CH),
                      gt2, w_out[l], T)
        x2 = _ffn(x2, norm_ffn2[l], sh3, sc3, gt3, ffn2_w_in[l], ffn2_w_out[l], T)
    return x2.reshape(B, T, D)
```

```python
import functools

import numpy as np
import jax
import jax.numpy as jnp
from jax import lax
from jax.experimental import pallas as pl
from jax.experimental.pallas import tpu as pltpu

F32 = jnp.float32
BF16 = jnp.bfloat16

EPS = 1e-6
CHUNK = 64
GLA_HEADS, GLA_DK, GLA_DV, GLA_RANK, GLA_TAU = 4, 48, 96, 16, 16.0
SB_HEADS, SB_DH = 6, 64
CONV_CH, CONV_WIDTH = 256, 31
D_FF = 2816
N_MOD = 9
GLA_QK = GLA_HEADS * GLA_DK
GLA_V = GLA_HEADS * GLA_DV
SB_W = SB_HEADS * SB_DH

LANES = 128
SUBLANES = 8
BF16_ROWS = 16
QK_PAD = 256
RANK_PAD = 128
FF_CHUNK = 256
TOKEN_TILE = 512
SB_BLOCK = 128
SB_GROUP = 8
SB_SKIP = 104.0
GLA_TILE = 512
GLA_GROUP = 4
CONV_TILE = 256
CONV_PAD = 32
VMEM_LIMIT = 52 * 1024 * 1024

_OFF_GQ, _OFF_GK, _OFF_GV, _OFF_GG, _OFF_GR = 0, 256, 512, 896, 1280
_OFF_SQ, _OFF_SK, _OFF_SV, _OFF_CA, _OFF_CG = 1408, 1792, 2176, 2560, 2816
PROJ_W = 3072


def _dot(a, b):
    return jnp.dot(a, b, preferred_element_type=F32)


def _dot_nt(a, b):
    return lax.dot_general(a, b, (((1,), (1,)), ((), ())), preferred_element_type=F32)


def _dot_tn(a, b):
    return lax.dot_general(a, b, (((0,), (0,)), ((), ())), preferred_element_type=F32)


def _hi_lo(x, axis):
    hi = x.astype(BF16)
    lo = (x - hi.astype(F32)).astype(BF16)
    return jnp.concatenate([hi, lo], axis=axis)


def _split_dot(x, m2):
    return _dot(_hi_lo(x, 1), m2)


def _split_dot_left(m2, x):
    return _dot(m2, _hi_lo(x, 0))


def _sigmoid(x):
    return 1.0 / (1.0 + jnp.exp(-x))


def _silu(x):
    return x * _sigmoid(x)


def _log_sigmoid(x):
    return jnp.minimum(x, 0.0) - jnp.log(1.0 + jnp.exp(-jnp.abs(x)))


def _modulated_norm(x, g, scale, shift):
    ms = jnp.mean(x * x, axis=-1, keepdims=True)
    return (x * lax.rsqrt(ms + EPS) * g) * (1.0 + scale) + shift


def _adaln_kernel(c_ref, w_ref, b_ref, o_ref):
    ca = _silu(c_ref[...]).astype(BF16)
    o_ref[0] = _dot(ca, w_ref[0].astype(BF16)) + b_ref[0]


def _adaln(c, w_ada, b_ada):
    L, D, W = w_ada.shape
    B = c.shape[0]
    tn = 1536
    return pl.pallas_call(
        _adaln_kernel,
        out_shape=jax.ShapeDtypeStruct((L, B, W), F32),
        grid=(L, W // tn),
        in_specs=[
            pl.BlockSpec((B, D), lambda l, j: (0, 0)),
            pl.BlockSpec((1, D, tn), lambda l, j: (l, 0, j)),
            pl.BlockSpec((1, 1, tn), lambda l, j: (l, 0, j)),
        ],
        out_specs=pl.BlockSpec((1, B, tn), lambda l, j: (l, 0, j)),
        compiler_params=pltpu.CompilerParams(
            dimension_semantics=("parallel", "parallel"), vmem_limit_bytes=VMEM_LIMIT),
        name="adaln",
    )(c, w_ada, b_ada.reshape(L, 1, W))


def _ffn_kernel(x_ref, g_ref, sh_ref, sc_ref, gt_ref, win_ref, wout_ref, *rest):
    if len(rest) == 6:
        nwi_ref, nwo_ref, o_ref, nwi_out_ref, nwo_out_ref, acc_ref = rest
        nwi_out_ref[...] = nwi_ref[...].astype(BF16)
        nwo_out_ref[...] = nwo_ref[...].astype(BF16)
    else:
        o_ref, acc_ref = rest
    x = x_ref[...]
    hb = _modulated_norm(x, g_ref[...], sc_ref[0], sh_ref[0]).astype(BF16)
    d_ff = wout_ref.shape[0]
    for j in range(d_ff // FF_CHUNK):
        cols = slice(j * FF_CHUNK, (j + 1) * FF_CHUNK)
        a = _dot(hb, win_ref[:, cols])
        b = _dot(hb, win_ref[:, d_ff + j * FF_CHUNK:d_ff + (j + 1) * FF_CHUNK])
        y = _dot((_silu(a) * b).astype(BF16), wout_ref[cols, :])
        if j == 0:
            acc_ref[...] = y
        else:
            acc_ref[...] += y
    o_ref[...] = x + (0.5 * gt_ref[0]) * acc_ref[...]


def _const_spec(shape):
    nd = len(shape)
    return pl.BlockSpec(shape, lambda *_: (0,) * nd, pipeline_mode=pl.Buffered(1))


def _row_spec(tm, w):
    return pl.BlockSpec((tm, w), lambda i: (i, 0))


def _batch_vec_spec(tiles_per_batch, w):
    return pl.BlockSpec((1, 1, w), lambda i: (i // tiles_per_batch, 0, 0))


def _ffn(x2, g, shift, scale, gate, win, wout, seq, next_weights=None):
    N, D = x2.shape
    tm = TOKEN_TILE
    tpb = seq // tm
    steps = N // tm
    in_specs = [
        _row_spec(tm, D),
        _const_spec((1, D)),
        _batch_vec_spec(tpb, D), _batch_vec_spec(tpb, D), _batch_vec_spec(tpb, D),
        _const_spec(win.shape), _const_spec(wout.shape),
    ]
    out_shape = [jax.ShapeDtypeStruct((N, D), F32)]
    out_specs = [_row_spec(tm, D)]
    args = [x2, g.reshape(1, D), shift, scale, gate, win, wout]
    if next_weights is not None:
        nwi, nwo = next_weights
        in_rows = nwi.shape[0] // steps
        out_rows = LANES
        last = nwo.shape[0] // out_rows - 1
        assert nwi.shape[0] % steps == 0 and in_rows % BF16_ROWS == 0 and nwo.shape[0] % out_rows == 0
        assert last < steps
        slabs = [pl.BlockSpec((in_rows, nwi.shape[1]), lambda i: (i, 0)),
                 pl.BlockSpec((out_rows, nwo.shape[1]), lambda i: (jnp.minimum(i, last), 0))]
        in_specs += slabs
        out_specs += slabs
        out_shape += [jax.ShapeDtypeStruct(nwi.shape, BF16), jax.ShapeDtypeStruct(nwo.shape, BF16)]
        args += [nwi, nwo]
    outs = pl.pallas_call(
        _ffn_kernel,
        out_shape=out_shape,
        grid=(steps,),
        in_specs=in_specs,
        out_specs=out_specs,
        scratch_shapes=[pltpu.VMEM((tm, D), F32)],
        compiler_params=pltpu.CompilerParams(
            dimension_semantics=("arbitrary",), vmem_limit_bytes=VMEM_LIMIT),
        name="ffn",
    )(*args)
    return outs[0] if next_weights is None else tuple(outs)


def _head_rms(x, seg2, gain, head_dim):
    ms = _split_dot(x * x, seg2) * (1.0 / head_dim)
    return x * lax.rsqrt(ms + EPS) * gain


def _inproj_kernel(x_ref, g_ref, sh_ref, sc_ref, w_ref, wup_ref, bg_ref, seg_ref, qn_ref, kn_ref,
                   gq_ref, gk_ref, ga_ref, gv_ref, gg_ref, sq_ref, sk_ref, sv_ref, cu_ref):
    hb = _modulated_norm(x_ref[...], g_ref[...], sc_ref[0], sh_ref[0]).astype(BF16)
    p = _dot(hb, w_ref[...])
    gq_ref[...] = p[:, _OFF_GQ:_OFF_GQ + QK_PAD]
    gk_ref[...] = p[:, _OFF_GK:_OFF_GK + QK_PAD]
    gv_ref[...] = p[:, _OFF_GV:_OFF_GV + GLA_V]
    gg_ref[...] = p[:, _OFF_GG:_OFF_GG + GLA_V]
    r = p[:, _OFF_GR:_OFF_GR + RANK_PAD].astype(BF16)
    ga_ref[...] = _log_sigmoid(_dot(r, wup_ref[...]) + bg_ref[...]) * (1.0 / GLA_TAU)
    seg = seg_ref[...]
    sq = _head_rms(p[:, _OFF_SQ:_OFF_SQ + SB_W], seg, qn_ref[...], SB_DH)
    sq_ref[...] = (sq * (SB_DH ** -0.5)).astype(BF16)
    sk_ref[...] = _head_rms(p[:, _OFF_SK:_OFF_SK + SB_W], seg, kn_ref[...], SB_DH).astype(BF16)
    sv_ref[...] = p[:, _OFF_SV:_OFF_SV + SB_W].astype(BF16)
    cu_ref[...] = p[:, _OFF_CA:_OFF_CA + CONV_CH] * _sigmoid(p[:, _OFF_CG:_OFF_CG + CONV_CH])


def _same_head(width, head_dim):
    h = np.arange(width) // head_dim
    m = (h[:, None] == h[None, :]).astype(np.float32)
    return jnp.asarray(np.concatenate([m, m], axis=0), BF16)


def _pad_cols(w, width):
    return jnp.pad(w, ((0, 0), (0, width - w.shape[1])))


def _inproj(x2, g, shift, scale, w_in, w_up, b_gate, q_norm, k_norm, seq):
    N, D = x2.shape
    o = 0
    parts = []
    for width, pad in ((GLA_QK, QK_PAD), (GLA_QK, QK_PAD), (GLA_V, GLA_V), (GLA_V, GLA_V),
                       (GLA_RANK, RANK_PAD), (SB_W, SB_W), (SB_W, SB_W), (SB_W, SB_W),
                       (CONV_CH, CONV_CH), (CONV_CH, CONV_CH)):
        parts.append(_pad_cols(w_in[:, o:o + width], pad))
        o += width
    w = jnp.concatenate(parts, axis=1).astype(BF16)
    wup = jnp.pad(w_up, ((0, RANK_PAD - GLA_RANK), (0, QK_PAD - GLA_QK))).astype(BF16)
    bg = jnp.pad(b_gate, (0, QK_PAD - GLA_QK)).reshape(1, QK_PAD)
    tm = TOKEN_TILE
    tpb = seq // tm
    seg = _same_head(SB_W, SB_DH)
    heads = lambda gn: jnp.tile(gn, SB_HEADS).reshape(1, SB_W)
    outs = ((QK_PAD, F32), (QK_PAD, F32), (QK_PAD, F32), (GLA_V, F32), (GLA_V, F32),
            (SB_W, BF16), (SB_W, BF16), (SB_W, BF16), (CONV_CH, F32))
    return pl.pallas_call(
        _inproj_kernel,
        out_shape=[jax.ShapeDtypeStruct((N, wd), dt) for wd, dt in outs],
        grid=(N // tm,),
        in_specs=[
            _row_spec(tm, D),
            _const_spec((1, D)),
            _batch_vec_spec(tpb, D), _batch_vec_spec(tpb, D),
            _const_spec(w.shape), _const_spec(wup.shape), _const_spec(bg.shape),
            _const_spec(seg.shape), _const_spec((1, SB_W)), _const_spec((1, SB_W)),
        ],
        out_specs=[_row_spec(tm, wd) for wd, _ in outs],
        compiler_params=pltpu.CompilerParams(
            dimension_semantics=("parallel",), vmem_limit_bytes=VMEM_LIMIT),
        name="inproj",
    )(x2, g.reshape(1, D), shift, scale, w, wup, bg, seg, heads(q_norm), heads(k_norm))


_GLA_LEVELS = 6


def _gla_constants():
    C = CHUNK
    t = np.arange(C)[:, None]
    j = np.arange(C)[None, :]
    mats = [j <= t, j > t]
    masks = [t == j]
    for l in range(1, _GLA_LEVELS + 1):
        n, m = 1 << l, 1 << (l - 1)
        ref = t // n * n + m - 1
        right = (t % n) >= m
        mats.append((right & (j > ref) & (j <= t)) | ((~right) & (j > t) & (j <= ref)))
        masks.append(((t // n) == (j // n)) & right & ((j % n) < m))
    prefix = np.concatenate(mats, axis=0).astype(np.float32)
    prefix = np.concatenate([prefix, prefix], axis=1)
    masks = np.stack([np.tile(mk, (1, GLA_HEADS)) for mk in masks]).astype(np.float32)
    kl = np.arange(QK_PAD)[None, :] // GLA_DK
    vl = np.arange(GLA_V)[:, None] // GLA_DV
    state_mask = (kl == vl).astype(np.float32)
    return prefix, masks, state_mask


def _gla_kernel(q_ref, k_ref, a_ref, v_ref, g_ref, pre_ref, msk_ref, smask_ref, seg_ref, on_ref,
                o_ref, st_ref):
    C = CHUNK

    @pl.when(pl.program_id(1) == 0)
    def _():
        st_ref[...] = jnp.zeros_like(st_ref)

    klane = lax.broadcasted_iota(jnp.int32, (1, QK_PAD), 1) // GLA_DK
    vlane = lax.broadcasted_iota(jnp.int32, (1, GLA_V), 1) // GLA_DV
    n_grp = GLA_GROUP

    def group(i, carry):
        base = i * (n_grp * C)
        rows = [pl.ds(pl.multiple_of(base + j * C, C), C) for j in range(n_grp)]
        qs = [q_ref[0, r, :] * (GLA_DK ** -0.5) for r in rows]
        ks = [k_ref[0, r, :] for r in rows]
        vs = [v_ref[0, r, :] for r in rows]

        es = []
        for r in rows:
            es.append(_split_dot_left(pre_ref[...], a_ref[0, r, :]))

        atts = []
        for q, k, e in zip(qs, ks, es):
            att = None
            for l in range(_GLA_LEVELS + 1):
                if l == 0:
                    qt, kt = q, k
                else:
                    f = jnp.exp(e[(l + 1) * C:(l + 2) * C])
                    qt, kt = q * f, k * f
                kst = jnp.concatenate(
                    [jnp.where(klane == h, kt, 0.0) for h in range(GLA_HEADS)], axis=0).astype(BF16)
                term = _dot_nt(qt.astype(BF16), kst) * msk_ref[l]
                att = term if att is None else att + term
            atts.append(att)

        intra, upds, qbs, decs = [], [], [], []
        for q, k, v, e, att in zip(qs, ks, vs, es, atts):
            vst = jnp.concatenate(
                [jnp.where(vlane == h, v, 0.0) for h in range(GLA_HEADS)], axis=0).astype(BF16)
            intra.append(_dot(att.astype(BF16), vst))
            kdec = (k * jnp.exp(e[C:2 * C])).astype(BF16)
            upds.append(_dot_tn(v.astype(BF16), kdec) * smask_ref[...])
            qbs.append((q * jnp.exp(e[0:C])).astype(BF16))
            decs.append(jnp.exp(e[C - 1:C]))

        st = st_ref[...]
        outs = []
        for o_intra, upd, qb, dec in zip(intra, upds, qbs, decs):
            outs.append(o_intra + _dot_nt(qb, st.astype(BF16)))
            st = st * dec + upd
        st_ref[...] = st

        o = jnp.concatenate(outs, axis=0)
        out_rows = pl.ds(pl.multiple_of(base, n_grp * C), n_grp * C)
        y = _head_rms(o, seg_ref[...], on_ref[...], GLA_DV)
        o_ref[0, out_rows, :] = y * _silu(g_ref[0, out_rows, :])
        return carry

    lax.fori_loop(0, q_ref.shape[1] // (n_grp * C), group, 0)


def _gla(gq, gk, ga, gv, gg, out_norm):
    B, T, _ = gq.shape
    prefix, masks, state_mask = _gla_constants()
    tt = GLA_TILE
    seq_spec = lambda w: pl.BlockSpec((1, tt, w), lambda b, i: (b, i, 0))
    consts = (jnp.asarray(prefix, BF16), jnp.asarray(masks), jnp.asarray(state_mask),
              _same_head(GLA_V, GLA_DV), jnp.tile(out_norm, GLA_HEADS).reshape(1, GLA_V))
    return pl.pallas_call(
        _gla_kernel,
        out_shape=jax.ShapeDtypeStruct((B, T, GLA_V), F32),
        grid=(B, T // tt),
        in_specs=[seq_spec(QK_PAD), seq_spec(QK_PAD), seq_spec(QK_PAD), seq_spec(GLA_V), seq_spec(GLA_V)]
                 + [_const_spec(cst.shape) for cst in consts],
        out_specs=seq_spec(GLA_V),
        scratch_shapes=[pltpu.VMEM((GLA_V, QK_PAD), F32)],
        compiler_params=pltpu.CompilerParams(
            dimension_semantics=("parallel", "arbitrary"), vmem_limit_bytes=VMEM_LIMIT),
        name="gla",
    )(gq, gk, ga, gv, gg, *consts)


def _sb_kernel(q_ref, k_ref, v_ref, og_ref, seg_ref, causal_ref, suf_ref, o_ref,
               qs_ref, kn_ref, v0_ref, v1_ref, acc_ref, carry_ref, alive_ref):
    T = q_ref.shape[1]
    BLK, G = SB_BLOCK, SB_GROUP
    n_q = T // BLK
    n_groups = n_q // G
    first = lax.broadcasted_iota(jnp.int32, (1, 2 * SB_DH), 1) < SB_DH

    kn_ref[0:BLK, :] = jnp.zeros((BLK, 2 * SB_DH), BF16)
    kn_ref[BLK:, :] = k_ref[0]
    v0_ref[0:BLK, :] = jnp.zeros((BLK, SB_DH), BF16)
    v1_ref[0:BLK, :] = jnp.zeros((BLK, SB_DH), BF16)
    v0_ref[BLK:, :] = v_ref[0, :, 0:SB_DH]
    v1_ref[BLK:, :] = v_ref[0, :, SB_DH:]

    for qi in range(n_q):
        q2 = q_ref[0, qi * BLK:(qi + 1) * BLK, :]
        zq = jnp.zeros_like(q2)
        qs_ref[qi, 0:BLK, :] = jnp.where(first, q2, zq)
        qs_ref[qi, BLK:, :] = jnp.where(first, zq, q2)

    def group_step(grp, dist, causal):
        qis = [grp * G + g for g in range(G)]
        krows = [pl.ds(pl.multiple_of(jnp.maximum(qi - dist + 1, 0) * BLK, BLK), BLK) for qi in qis]
        zs = [_dot_nt(kn_ref[kr, :], qs_ref[qi]) for qi, kr in zip(qis, krows)]
        log_betas, drops = [], []
        for z in zs:
            sp = jnp.maximum(z, 0.0) + jnp.log(1.0 + jnp.exp(-jnp.abs(z)))
            log_betas.append(z - sp)
            drops.append(sp * causal_ref[...] if causal else sp)
        sufs = [_split_dot_left(suf_ref[...], d) for d in drops]
        ws, alive = [], None
        for qi, lb, d, suf in zip(qis, log_betas, drops, sufs):
            carry = jnp.zeros((1, 2 * BLK), F32) if causal else carry_ref[qi]
            w = jnp.exp(lb - suf + carry)
            ws.append((w * causal_ref[...] if causal else w).astype(BF16))
            carry = carry - (suf[0:1, :] + d[0:1, :])
            carry_ref[qi] = carry
            alive = carry if alive is None else jnp.maximum(alive, carry)
        for qi, kr, w in zip(qis, krows, ws):
            for h, v_ref_h in enumerate((v0_ref, v1_ref)):
                upd = _dot_tn(v_ref_h[kr, :], w[:, h * BLK:(h + 1) * BLK])
                out = (qi, slice(h * SB_DH, (h + 1) * SB_DH), slice(None))
                acc_ref[out] = upd if causal else acc_ref[out] + upd
        return jnp.max(alive) > -SB_SKIP

    def diagonal(grp, c0):
        alive_ref[grp] = group_step(grp, 0, True).astype(jnp.int32)
        return c0

    lax.fori_loop(0, n_groups, diagonal, 0)

    def any_alive():
        total = alive_ref[0]
        for g in range(1, n_groups):
            total = total + alive_ref[g]
        return total > 0

    def sweep(state):
        dist, _ = state

        def visit(grp, c0):
            run = jnp.logical_and(alive_ref[grp] > 0, grp * G + G - 1 >= dist)

            @pl.when(run)
            def _():
                alive_ref[grp] = group_step(grp, dist, False).astype(jnp.int32)

            @pl.when(jnp.logical_not(run))
            def _():
                alive_ref[grp] = 0

            return c0

        lax.fori_loop(0, n_groups, visit, 0)
        return dist + 1, any_alive()

    lax.while_loop(lambda s: jnp.logical_and(s[0] < n_q, s[1]), sweep, (1, any_alive()))

    def finish(grp, c0):
        cols = pl.ds(pl.multiple_of(grp * G * BLK, G * BLK), G * BLK)
        ot = jnp.concatenate([acc_ref[grp * G + g] for g in range(G)], axis=1)
        o_ref[0, cols, :] = _head_rms(ot.T, seg_ref[...], og_ref[...], SB_DH)
        return c0

    lax.fori_loop(0, n_groups, finish, 0)


def _sb_constants():
    BLK = SB_BLOCK
    s = np.arange(BLK)[:, None]
    j = np.arange(BLK)[None, :]
    suffix = (j > s).astype(np.float32)
    suffix = np.concatenate([suffix, suffix], axis=1)
    causal =(s < np.tile(np.arange(BLK), 2)[None, :]).astype(np.float32)
    return causal, suffix


def _sb(sq, sk, sv, out_norm):
    B, T, _ = sq.shape
    causal, suffix = _sb_constants()
    consts = (jnp.tile(out_norm, 2).reshape(1, 2 * SB_DH), _same_head(2 * SB_DH, SB_DH),
              jnp.asarray(causal), jnp.asarray(suffix, BF16))
    spec = pl.BlockSpec((1, T, 2 * SB_DH), lambda b, p: (b, 0, p))
    assert T % (SB_GROUP * SB_BLOCK) == 0, (T, SB_GROUP, SB_BLOCK)
    n_q = T // SB_BLOCK
    return pl.pallas_call(
        _sb_kernel,
        out_shape=jax.ShapeDtypeStruct((B, T, SB_W), F32),
        grid=(B, SB_HEADS // 2),
        in_specs=[spec, spec, spec] + [_const_spec(cst.shape) for cst in consts],
        out_specs=spec,
        scratch_shapes=[pltpu.VMEM((n_q, 2 * SB_BLOCK, 2 * SB_DH), BF16),
                        pltpu.VMEM((T + SB_BLOCK, 2 * SB_DH), BF16),
                        pltpu.VMEM((T + SB_BLOCK, SB_DH), BF16),
                        pltpu.VMEM((T + SB_BLOCK, SB_DH), BF16),
                        pltpu.VMEM((n_q, 2 * SB_DH, SB_BLOCK), F32),
                        pltpu.VMEM((n_q, 1, 2 * SB_BLOCK), F32),
                        pltpu.SMEM((n_q // SB_GROUP,), jnp.int32)],
        compiler_params=pltpu.CompilerParams(
            dimension_semantics=("parallel", "parallel"), vmem_limit_bytes=VMEM_LIMIT),
        name="stickbreak",
    )(sq, sk, sv, *consts)


def _conv_kernel(u_ref, w_ref, b_ref, lg_ref, lb_ref, o_ref, pad_ref, sh_ref):
    T = u_ref.shape[1]
    S = SUBLANES
    pad_ref[0:CONV_PAD, :] = jnp.zeros((CONV_PAD, CONV_CH), F32)
    pad_ref[CONV_PAD:, :] = u_ref[0]
    for r in range(1, S):
        sh_ref[r - 1, S:, :] = pad_ref[S - r:T + CONV_PAD - r, :]
    for i in range(T // CONV_TILE):
        acc = jnp.zeros((CONV_TILE, CONV_CH), F32)
        for j in range(CONV_WIDTH):
            a, r = divmod(CONV_WIDTH - 1 - j, S)
            start = i * CONV_TILE + CONV_PAD - S * a
            src = pad_ref[start:start + CONV_TILE, :] if r == 0 else sh_ref[r - 1, start:start + CONV_TILE, :]
            acc = acc + w_ref[j:j + 1, :] * src
        acc = acc + b_ref[...]
        mu = jnp.mean(acc, axis=-1, keepdims=True)
        xc = acc - mu
        var = jnp.mean(xc * xc, axis=-1, keepdims=True)
        y = xc * lax.rsqrt(var + EPS) * lg_ref[...] + lb_ref[...]
        o_ref[0, i * CONV_TILE:(i + 1) * CONV_TILE, :] = _silu(y)


def _conv(cu, w, b, ln_g, ln_b):
    B, T, C = cu.shape
    spec = pl.BlockSpec((1, T, C), lambda bi: (bi, 0, 0))
    vec = lambda a: a.reshape(1, C)
    return pl.pallas_call(
        _conv_kernel,
        out_shape=jax.ShapeDtypeStruct((B, T, C), F32),
        grid=(B,),
        in_specs=[spec, _const_spec(w.shape), _const_spec((1, C)), _const_spec((1, C)), _const_spec((1, C))],
        out_specs=spec,
        scratch_shapes=[pltpu.VMEM((T + CONV_PAD, C), F32), pltpu.VMEM((SUBLANES - 1, T + CONV_PAD, C), F32)],
        compiler_params=pltpu.CompilerParams(
            dimension_semantics=("parallel",), vmem_limit_bytes=VMEM_LIMIT),
        name="conv",
    )(cu, w, vec(b), vec(ln_g), vec(ln_b))


def _outproj_kernel(x_ref, a_ref, b_ref, c_ref, gt_ref, wa_ref, wb_ref, wc_ref, o_ref):
    y = (_dot(a_ref[...].astype(BF16), wa_ref[...]) + _dot(b_ref[...].astype(BF16), wb_ref[...])
         + _dot(c_ref[...].astype(BF16), wc_ref[...]))
    o_ref[...] = x_ref[...] + gt_ref[0] * y


def _outproj(x2, oa, ob, oc, gate, w_out, seq):
    N, D = x2.shape
    wb16 = w_out.astype(BF16)
    wa, wb, wc = wb16[:GLA_V], wb16[GLA_V:GLA_V + SB_W], wb16[GLA_V + SB_W:]
    tm = TOKEN_TILE
    tpb = seq // tm
    return pl.pallas_call(
        _outproj_kernel,
        out_shape=jax.ShapeDtypeStruct((N, D), F32),
        grid=(N // tm,),
        in_specs=[
            _row_spec(tm, D), _row_spec(tm, GLA_V), _row_spec(tm, SB_W), _row_spec(tm, CONV_CH),
            _batch_vec_spec(tpb, D),
            _const_spec(wa.shape), _const_spec(wb.shape), _const_spec(wc.shape),
        ],
        out_specs=_row_spec(tm, D),
        compiler_params=pltpu.CompilerParams(
            dimension_semantics=("parallel",), vmem_limit_bytes=VMEM_LIMIT),
        name="outproj",
    )(x2, oa, ob, oc, gate, wa, wb, wc)


def kernel(x, c, w_ada, b_ada, norm_ffn1, ffn1_w_in, ffn1_w_out, norm_mix, w_in, w_out, gla_w_gate_up, gla_b_gate, gla_out_norm, sb_q_norm, sb_k_norm, sb_out_norm, conv_w, conv_b, conv_ln_g, conv_ln_b, norm_ffn2, ffn2_w_in, ffn2_w_out):
    B, T, D = x.shape
    L = w_ada.shape[0]
    mod = _adaln(c, w_ada, b_ada).reshape(L, B, N_MOD, 1, D)
    x2 = x.reshape(B * T, D)
    ffn_w = (ffn1_w_in[0].astype(BF16), ffn1_w_out[0].astype(BF16))
    for l in range(L):
        sh1, sc1, gt1, sh2, sc2, gt2, sh3, sc3, gt3 = (mod[l, :, i] for i in range(N_MOD))
        x2, *ffn_w = _ffn(x2, norm_ffn1[l], sh1, sc1, gt1, *ffn_w, T, (ffn2_w_in[l], ffn2_w_out[l]))
        gq, gk, ga, gv, gg, sq, sk, sv, cu = _inproj(
            x2, norm_mix[l], sh2, sc2, w_in[l], gla_w_gate_up[l], gla_b_gate[l], sb_q_norm[l], sb_k_norm[l], T)
        seq = lambda a: a.reshape(B, T, a.shape[-1])
        oa = _gla(seq(gq), seq(gk), seq(ga), seq(gv), seq(gg), gla_out_norm[l])
        ob = _sb(seq(sq), seq(sk), seq(sv), sb_out_norm[l])
        oc = _conv(seq(cu), conv_w[l], conv_b[l], conv_ln_g[l], conv_ln_b[l])
        x2 = _outproj(x2, oa.reshape(B * T, GLA_V), ob.reshape(B * T, SB_W), oc.reshape(B * T, CONV_CH),
                      gt2, w_out[l], T)
        if l + 1 < L:
            x2, *ffn_w = _ffn(x2, norm_ffn2[l], sh3, sc3, gt3, *ffn_w, T, (ffn1_w_in[l + 1], ffn1_w_out[l + 1]))
        else:
            x2 = _ffn(x2, norm_ffn2[l], sh3, sc3, gt3, *ffn_w, T)
    return x2.reshape(B, T, D)
```

```python
import functools

import numpy as np
import jax
import jax.numpy as jnp
from jax import lax
from jax.experimental import pallas as pl
from jax.experimental.pallas import tpu as pltpu

F32 = jnp.float32
BF16 = jnp.bfloat16

EPS = 1e-6
CHUNK = 64
GLA_HEADS, GLA_DK, GLA_DV, GLA_RANK, GLA_TAU = 4, 48, 96, 16, 16.0
SB_HEADS, SB_DH = 6, 64
CONV_CH, CONV_WIDTH = 256, 31
D_FF = 2816
N_MOD = 9
GLA_QK = GLA_HEADS * GLA_DK
GLA_V = GLA_HEADS * GLA_DV
SB_W = SB_HEADS * SB_DH

LANES = 128
SUBLANES = 8
BF16_ROWS = 16
QK_PAD = 256
RANK_PAD = 128
FF_CHUNK = 256
TOKEN_TILE = 512
SB_BLOCK = 128
SB_GROUP = 8
SB_SKIP = 104.0
GLA_TILE = 512
GLA_GROUP = 4
CONV_TILE = 256
CONV_PAD = 32
VMEM_LIMIT = 52 * 1024 * 1024

_OFF_GQ, _OFF_GK, _OFF_GV, _OFF_GG, _OFF_GR = 0, 256, 512, 896, 1280
_OFF_SQ, _OFF_SK, _OFF_SV, _OFF_CA, _OFF_CG = 1408, 1792, 2176, 2560, 2816
PROJ_W = 3072


def _dot(a, b):
    return jnp.dot(a, b, preferred_element_type=F32)


def _dot_nt(a, b):
    return lax.dot_general(a, b, (((1,), (1,)), ((), ())), preferred_element_type=F32)


def _dot_tn(a, b):
    return lax.dot_general(a, b, (((0,), (0,)), ((), ())), preferred_element_type=F32)


def _hi_lo(x, axis):
    hi = x.astype(BF16)
    lo = (x - hi.astype(F32)).astype(BF16)
    return jnp.concatenate([hi, lo], axis=axis)


def _split_dot(x, m2):
    return _dot(_hi_lo(x, 1), m2)


def _split_dot_left(m2, x):
    return _dot(m2, _hi_lo(x, 0))


def _sigmoid(x):
    return 1.0 / (1.0 + jnp.exp(-x))


def _silu(x):
    return x * _sigmoid(x)


def _log_sigmoid(x):
    return jnp.minimum(x, 0.0) - jnp.log(1.0 + jnp.exp(-jnp.abs(x)))


def _modulated_norm(x, g, scale, shift):
    ms = jnp.mean(x * x, axis=-1, keepdims=True)
    return (x * lax.rsqrt(ms + EPS) * g) * (1.0 + scale) + shift


def _adaln_kernel(c_ref, w_ref, b_ref, o_ref):
    ca = _silu(c_ref[...]).astype(BF16)
    o_ref[0] = _dot(ca, w_ref[0].astype(BF16)) + b_ref[0]


def _adaln(c, w_ada, b_ada):
    L, D, W = w_ada.shape
    B = c.shape[0]
    tn = 1536
    return pl.pallas_call(
        _adaln_kernel,
        out_shape=jax.ShapeDtypeStruct((L, B, W), F32),
        grid=(L, W // tn),
        in_specs=[
            pl.BlockSpec((B, D), lambda l, j: (0, 0)),
            pl.BlockSpec((1, D, tn), lambda l, j: (l, 0, j)),
            pl.BlockSpec((1, 1, tn), lambda l, j: (l, 0, j)),
        ],
        out_specs=pl.BlockSpec((1, B, tn), lambda l, j: (l, 0, j)),
        compiler_params=pltpu.CompilerParams(
            dimension_semantics=("parallel", "parallel"), vmem_limit_bytes=VMEM_LIMIT),
        name="adaln",
    )(c, w_ada, b_ada.reshape(L, 1, W))


def _ffn_kernel(has_mix, has_next, *refs):
    refs = list(refs)
    x_ref, g_ref, sh_ref, sc_ref, gt_ref, win_ref, wout_ref = refs[:7]
    del refs[:7]
    if has_mix:
        a_ref, b_ref, c_ref, gt2_ref, wa_ref, wb_ref, wc_ref = refs[:7]
        del refs[:7]
    if has_next:
        nwi_ref, nwo_ref, o_ref, nwi_out_ref, nwo_out_ref, acc_ref = refs
        nwi_out_ref[...] = nwi_ref[...].astype(BF16)
        nwo_out_ref[...] = nwo_ref[...].astype(BF16)
    else:
        o_ref, acc_ref = refs
    x = x_ref[...]
    if has_mix:
        y = _dot(a_ref[...], wa_ref[...]) + _dot(b_ref[...], wb_ref[...]) + _dot(c_ref[...], wc_ref[...])
        x = x + gt2_ref[0] * y
    hb = _modulated_norm(x, g_ref[...], sc_ref[0], sh_ref[0]).astype(BF16)
    d_ff = wout_ref.shape[0]
    for j in range(d_ff // FF_CHUNK):
        cols = slice(j * FF_CHUNK, (j + 1) * FF_CHUNK)
        a = _dot(hb, win_ref[:, cols])
        b = _dot(hb, win_ref[:, d_ff + j * FF_CHUNK:d_ff + (j + 1) * FF_CHUNK])
        y = _dot((_silu(a) * b).astype(BF16), wout_ref[cols, :])
        if j == 0:
            acc_ref[...] = y
        else:
            acc_ref[...] += y
    o_ref[...] = x + (0.5 * gt_ref[0]) * acc_ref[...]


def _const_spec(shape):
    nd = len(shape)
    return pl.BlockSpec(shape, lambda *_: (0,) * nd, pipeline_mode=pl.Buffered(1))


def _row_spec(tm, w):
    return pl.BlockSpec((tm, w), lambda i: (i, 0))


def _batch_vec_spec(tiles_per_batch, w):
    return pl.BlockSpec((1, 1, w), lambda i: (i // tiles_per_batch, 0, 0))


def _ffn(x2, g, shift, scale, gate, win, wout, seq, mix=None, next_weights=None):
    N, D = x2.shape
    tm = TOKEN_TILE
    tpb = seq // tm
    steps = N // tm
    in_specs = [
        _row_spec(tm, D),
        _const_spec((1, D)),
        _batch_vec_spec(tpb, D), _batch_vec_spec(tpb, D), _batch_vec_spec(tpb, D),
        _const_spec(win.shape), _const_spec(wout.shape),
    ]
    out_shape = [jax.ShapeDtypeStruct((N, D), F32)]
    out_specs = [_row_spec(tm, D)]
    args = [x2, g.reshape(1, D), shift, scale, gate, win, wout]
    if mix is not None:
        a, b, c, gate2, w_mix = mix
        parts, o = [], 0
        for t in (a, b, c):
            parts.append(w_mix[o:o + t.shape[1]])
            o += t.shape[1]
        in_specs += [_row_spec(tm, t.shape[1]) for t in (a, b, c)] + [_batch_vec_spec(tpb, D)]
        in_specs += [_const_spec(p.shape) for p in parts]
        args += [a, b, c, gate2] + parts
    if next_weights is not None:
        nwi, nwo, layer = next_weights
        in_rows = nwi.shape[1] // steps
        out_rows = LANES
        last = nwo.shape[1] // out_rows - 1
        assert nwi.shape[1] % steps == 0 and in_rows % BF16_ROWS == 0 and nwo.shape[1] % out_rows == 0
        assert last < steps
        in_specs += [pl.BlockSpec((None, in_rows, nwi.shape[2]), lambda i: (layer, i, 0)),
                     pl.BlockSpec((None, out_rows, nwo.shape[2]), lambda i: (layer, jnp.minimum(i, last), 0))]
        out_specs += [pl.BlockSpec((in_rows, nwi.shape[2]), lambda i: (i, 0)),
                      pl.BlockSpec((out_rows, nwo.shape[2]), lambda i: (jnp.minimum(i, last), 0))]
        out_shape += [jax.ShapeDtypeStruct(nwi.shape[1:], BF16), jax.ShapeDtypeStruct(nwo.shape[1:], BF16)]
        args += [nwi, nwo]
    outs = pl.pallas_call(
        functools.partial(_ffn_kernel, mix is not None, next_weights is not None),
        out_shape=out_shape,
        grid=(steps,),
        in_specs=in_specs,
        out_specs=out_specs,
        scratch_shapes=[pltpu.VMEM((tm, D), F32)],
        compiler_params=pltpu.CompilerParams(
            dimension_semantics=("arbitrary",), vmem_limit_bytes=VMEM_LIMIT),
        name="ffn",
    )(*args)
    return outs[0] if next_weights is None else tuple(outs)


def _head_rms(x, seg2, gain, head_dim):
    ms = _split_dot(x * x, seg2) * (1.0 / head_dim)
    return x * lax.rsqrt(ms + EPS) * gain


def _inproj_kernel(x_ref, g_ref, sh_ref, sc_ref, w_ref, wup_ref, bg_ref, seg_ref, qn_ref, kn_ref,
                   gq_ref, gk_ref, ga_ref, gv_ref, gg_ref, sq_ref, sk_ref, sv_ref, cu_ref):
    hb = _modulated_norm(x_ref[...], g_ref[...], sc_ref[0], sh_ref[0]).astype(BF16)
    p = _dot(hb, w_ref[...])
    gq_ref[...] = p[:, _OFF_GQ:_OFF_GQ + QK_PAD]
    gk_ref[...] = p[:, _OFF_GK:_OFF_GK + QK_PAD]
    gv_ref[...] = p[:, _OFF_GV:_OFF_GV + GLA_V]
    gg_ref[...] = p[:, _OFF_GG:_OFF_GG + GLA_V]
    r = p[:, _OFF_GR:_OFF_GR + RANK_PAD].astype(BF16)
    ga_ref[...] = _log_sigmoid(_dot(r, wup_ref[...]) + bg_ref[...]) * (1.0 / GLA_TAU)
    seg = seg_ref[...]
    sq = _head_rms(p[:, _OFF_SQ:_OFF_SQ + SB_W], seg, qn_ref[...], SB_DH)
    sq_ref[...] = (sq * (SB_DH ** -0.5)).astype(BF16)
    sk_ref[...] = _head_rms(p[:, _OFF_SK:_OFF_SK + SB_W], seg, kn_ref[...], SB_DH).astype(BF16)
    sv_ref[...] = p[:, _OFF_SV:_OFF_SV + SB_W].astype(BF16)
    cu_ref[...] = p[:, _OFF_CA:_OFF_CA + CONV_CH] * _sigmoid(p[:, _OFF_CG:_OFF_CG + CONV_CH])


def _same_head(width, head_dim):
    h = np.arange(width) // head_dim
    m = (h[:, None] == h[None, :]).astype(np.float32)
    return jnp.asarray(np.concatenate([m, m], axis=0), BF16)


def _pad_cols(w, width):
    return jnp.pad(w, ((0, 0), (0, width - w.shape[1])))


def _inproj(x2, g, shift, scale, w_in, w_up, b_gate, q_norm, k_norm, seq):
    N, D = x2.shape
    o = 0
    parts = []
    for width, pad in ((GLA_QK, QK_PAD), (GLA_QK, QK_PAD), (GLA_V, GLA_V), (GLA_V, GLA_V),
                       (GLA_RANK, RANK_PAD), (SB_W, SB_W), (SB_W, SB_W), (SB_W, SB_W),
                       (CONV_CH, CONV_CH), (CONV_CH, CONV_CH)):
        parts.append(_pad_cols(w_in[:, o:o + width], pad))
        o += width
    w = jnp.concatenate(parts, axis=1).astype(BF16)
    wup = jnp.pad(w_up, ((0, RANK_PAD - GLA_RANK), (0, QK_PAD - GLA_QK))).astype(BF16)
    bg = jnp.pad(b_gate, (0, QK_PAD - GLA_QK)).reshape(1, QK_PAD)
    tm = TOKEN_TILE
    tpb = seq // tm
    seg = _same_head(SB_W, SB_DH)
    heads = lambda gn: jnp.tile(gn, SB_HEADS).reshape(1, SB_W)
    outs = ((QK_PAD, F32), (QK_PAD, F32), (QK_PAD, F32), (GLA_V, F32), (GLA_V, F32),
            (SB_W, BF16), (SB_W, BF16), (SB_W, BF16), (CONV_CH, F32))
    return pl.pallas_call(
        _inproj_kernel,
        out_shape=[jax.ShapeDtypeStruct((N, wd), dt) for wd, dt in outs],
        grid=(N // tm,),
        in_specs=[
            _row_spec(tm, D),
            _const_spec((1, D)),
            _batch_vec_spec(tpb, D), _batch_vec_spec(tpb, D),
            _const_spec(w.shape), _const_spec(wup.shape), _const_spec(bg.shape),
            _const_spec(seg.shape), _const_spec((1, SB_W)), _const_spec((1, SB_W)),
        ],
        out_specs=[_row_spec(tm, wd) for wd, _ in outs],
        compiler_params=pltpu.CompilerParams(
            dimension_semantics=("parallel",), vmem_limit_bytes=VMEM_LIMIT),
        name="inproj",
    )(x2, g.reshape(1, D), shift, scale, w, wup, bg, seg, heads(q_norm), heads(k_norm))


_GLA_LEVELS = 6


def _gla_constants():
    C = CHUNK
    t = np.arange(C)[:, None]
    j = np.arange(C)[None, :]
    mats = [j <= t, j > t]
    masks = [t == j]
    for l in range(1, _GLA_LEVELS + 1):
        n, m = 1 << l, 1 << (l - 1)
        ref = t // n * n + m - 1
        right = (t % n) >= m
        mats.append((right & (j > ref) & (j <= t)) | ((~right) & (j > t) & (j <= ref)))
        masks.append(((t // n) == (j // n)) & right & ((j % n) < m))
    prefix = np.concatenate(mats, axis=0).astype(np.float32)
    prefix = np.concatenate([prefix, prefix], axis=1)
    masks = np.stack([np.tile(mk, (1, GLA_HEADS)) for mk in masks]).astype(np.float32)
    kl = np.arange(QK_PAD)[None, :] // GLA_DK
    vl = np.arange(GLA_V)[:, None] // GLA_DV
    state_mask = (kl == vl).astype(np.float32)
    return prefix, masks, state_mask


def _gla_kernel(q_ref, k_ref, a_ref, v_ref, g_ref, pre_ref, msk_ref, smask_ref, seg_ref, on_ref,
                o_ref, st_ref):
    C = CHUNK

    @pl.when(pl.program_id(1) == 0)
    def _():
        st_ref[...] = jnp.zeros_like(st_ref)

    klane = lax.broadcasted_iota(jnp.int32, (1, QK_PAD), 1) // GLA_DK
    vlane = lax.broadcasted_iota(jnp.int32, (1, GLA_V), 1) // GLA_DV
    n_grp = GLA_GROUP

    def group(i, carry):
        base = i * (n_grp * C)
        rows = [pl.ds(pl.multiple_of(base + j * C, C), C) for j in range(n_grp)]
        qs = [q_ref[0, r, :] * (GLA_DK ** -0.5) for r in rows]
        ks = [k_ref[0, r, :] for r in rows]
        vs = [v_ref[0, r, :] for r in rows]

        es = []
        for r in rows:
            es.append(_split_dot_left(pre_ref[...], a_ref[0, r, :]))

        atts = []
        for q, k, e in zip(qs, ks, es):
            att = None
            for l in range(_GLA_LEVELS + 1):
                if l == 0:
                    qt, kt = q, k
                else:
                    f = jnp.exp(e[(l + 1) * C:(l + 2) * C])
                    qt, kt = q * f, k * f
                kst = jnp.concatenate(
                    [jnp.where(klane == h, kt, 0.0) for h in range(GLA_HEADS)], axis=0).astype(BF16)
                term = _dot_nt(qt.astype(BF16), kst) * msk_ref[l]
                att = term if att is None else att + term
            atts.append(att)

        intra, upds, qbs, decs = [], [], [], []
        for q, k, v, e, att in zip(qs, ks, vs, es, atts):
            vst = jnp.concatenate(
                [jnp.where(vlane == h, v, 0.0) for h in range(GLA_HEADS)], axis=0).astype(BF16)
            intra.append(_dot(att.astype(BF16), vst))
            kdec = (k * jnp.exp(e[C:2 * C])).astype(BF16)
            upds.append(_dot_tn(v.astype(BF16), kdec) * smask_ref[...])
            qbs.append((q * jnp.exp(e[0:C])).astype(BF16))
            decs.append(jnp.exp(e[C - 1:C]))

        st = st_ref[...]
        outs = []
        for o_intra, upd, qb, dec in zip(intra, upds, qbs, decs):
            outs.append(o_intra + _dot_nt(qb, st.astype(BF16)))
            st = st * dec + upd
        st_ref[...] = st

        o = jnp.concatenate(outs, axis=0)
        out_rows = pl.ds(pl.multiple_of(base, n_grp * C), n_grp * C)
        y = _head_rms(o, seg_ref[...], on_ref[...], GLA_DV)
        o_ref[0, out_rows, :] = (y * _silu(g_ref[0, out_rows, :])).astype(o_ref.dtype)
        return carry

    lax.fori_loop(0, q_ref.shape[1] // (n_grp * C), group, 0)


def _gla(gq, gk, ga, gv, gg, out_norm):
    B, T, _ = gq.shape
    prefix, masks, state_mask = _gla_constants()
    tt = GLA_TILE
    seq_spec = lambda w: pl.BlockSpec((1, tt, w), lambda b, i: (b, i, 0))
    consts = (jnp.asarray(prefix, BF16), jnp.asarray(masks), jnp.asarray(state_mask),
              _same_head(GLA_V, GLA_DV), jnp.tile(out_norm, GLA_HEADS).reshape(1, GLA_V))
    return pl.pallas_call(
        _gla_kernel,
        out_shape=jax.ShapeDtypeStruct((B, T, GLA_V), BF16),
        grid=(B, T // tt),
        in_specs=[seq_spec(QK_PAD), seq_spec(QK_PAD), seq_spec(QK_PAD), seq_spec(GLA_V), seq_spec(GLA_V)]
                 + [_const_spec(cst.shape) for cst in consts],
        out_specs=seq_spec(GLA_V),
        scratch_shapes=[pltpu.VMEM((GLA_V, QK_PAD), F32)],
        compiler_params=pltpu.CompilerParams(
            dimension_semantics=("parallel", "arbitrary"), vmem_limit_bytes=VMEM_LIMIT),
        name="gla",
    )(gq, gk, ga, gv, gg, *consts)


def _sb_kernel(q_ref, k_ref, v_ref, og_ref, seg_ref, causal_ref, suf_ref, o_ref,
               qs_ref, kn_ref, v0_ref, v1_ref, acc_ref, carry_ref, alive_ref):
    T = q_ref.shape[1]
    BLK, G = SB_BLOCK, SB_GROUP
    n_q = T // BLK
    n_groups = n_q // G
    first = lax.broadcasted_iota(jnp.int32, (1, 2 * SB_DH), 1) < SB_DH

    kn_ref[0:BLK, :] = jnp.zeros((BLK, 2 * SB_DH), BF16)
    kn_ref[BLK:, :] = k_ref[0]
    v0_ref[0:BLK, :] = jnp.zeros((BLK, SB_DH), BF16)
    v1_ref[0:BLK, :] = jnp.zeros((BLK, SB_DH), BF16)
    v0_ref[BLK:, :] = v_ref[0, :, 0:SB_DH]
    v1_ref[BLK:, :] = v_ref[0, :, SB_DH:]

    for qi in range(n_q):
        q2 = q_ref[0, qi * BLK:(qi + 1) * BLK, :]
        zq = jnp.zeros_like(q2)
        qs_ref[qi, 0:BLK, :] = jnp.where(first, q2, zq)
        qs_ref[qi, BLK:, :] = jnp.where(first, zq, q2)

    def group_step(grp, dist, causal):
        qis = [grp * G + g for g in range(G)]
        krows = [pl.ds(pl.multiple_of(jnp.maximum(qi - dist + 1, 0) * BLK, BLK), BLK) for qi in qis]
        zs = [_dot_nt(kn_ref[kr, :], qs_ref[qi]) for qi, kr in zip(qis, krows)]
        log_betas, drops = [], []
        for z in zs:
            sp = jnp.maximum(z, 0.0) + jnp.log(1.0 + jnp.exp(-jnp.abs(z)))
            log_betas.append(z - sp)
            drops.append(sp * causal_ref[...] if causal else sp)
        sufs = [_split_dot_left(suf_ref[...], d) for d in drops]
        ws, alive = [], None
        for qi, lb, d, suf in zip(qis, log_betas, drops, sufs):
            carry = jnp.zeros((1, 2 * BLK), F32) if causal else carry_ref[qi]
            w = jnp.exp(lb - suf + carry)
            ws.append((w * causal_ref[...] if causal else w).astype(BF16))
            carry = carry - (suf[0:1, :] + d[0:1, :])
            carry_ref[qi] = carry
            alive = carry if alive is None else jnp.maximum(alive, carry)
        for qi, kr, w in zip(qis, krows, ws):
            for h, v_ref_h in enumerate((v0_ref, v1_ref)):
                upd = _dot_tn(v_ref_h[kr, :], w[:, h * BLK:(h + 1) * BLK])
                out = (qi, slice(h * SB_DH, (h + 1) * SB_DH), slice(None))
                acc_ref[out] = upd if causal else acc_ref[out] + upd
        return jnp.max(alive) > -SB_SKIP

    def diagonal(grp, c0):
        alive_ref[grp] = group_step(grp, 0, True).astype(jnp.int32)
        return c0

    lax.fori_loop(0, n_groups, diagonal, 0)

    def any_alive():
        total = alive_ref[0]
        for g in range(1, n_groups):
            total = total + alive_ref[g]
        return total > 0

    def sweep(state):
        dist, _ = state

        def visit(grp, c0):
            run = jnp.logical_and(alive_ref[grp] > 0, grp * G + G - 1 >= dist)

            @pl.when(run)
            def _():
                alive_ref[grp] = group_step(grp, dist, False).astype(jnp.int32)

            @pl.when(jnp.logical_not(run))
            def _():
                alive_ref[grp] = 0

            return c0

        lax.fori_loop(0, n_groups, visit, 0)
        return dist + 1, any_alive()

    lax.while_loop(lambda s: jnp.logical_and(s[0] < n_q, s[1]), sweep, (1, any_alive()))

    def finish(grp, c0):
        cols = pl.ds(pl.multiple_of(grp * G * BLK, G * BLK), G * BLK)
        ot = jnp.concatenate([acc_ref[grp * G + g] for g in range(G)], axis=1)
        o_ref[0, cols, :] = _head_rms(ot.T, seg_ref[...], og_ref[...], SB_DH).astype(o_ref.dtype)
        return c0

    lax.fori_loop(0, n_groups, finish, 0)


def _sb_constants():
    BLK = SB_BLOCK
    s = np.arange(BLK)[:, None]
    j = np.arange(BLK)[None, :]
    suffix = (j > s).astype(np.float32)
    suffix = np.concatenate([suffix, suffix], axis=1)
    causal =(s < np.tile(np.arange(BLK), 2)[None, :]).astype(np.float32)
    return causal, suffix


def _sb(sq, sk, sv, out_norm):
    B, T, _ = sq.shape
    causal, suffix = _sb_constants()
    consts = (jnp.tile(out_norm, 2).reshape(1, 2 * SB_DH), _same_head(2 * SB_DH, SB_DH),
              jnp.asarray(causal), jnp.asarray(suffix, BF16))
    spec = pl.BlockSpec((1, T, 2 * SB_DH), lambda b, p: (b, 0, p))
    assert T % (SB_GROUP * SB_BLOCK) == 0, (T, SB_GROUP, SB_BLOCK)
    n_q = T // SB_BLOCK
    return pl.pallas_call(
        _sb_kernel,
        out_shape=jax.ShapeDtypeStruct((B, T, SB_W), BF16),
        grid=(B, SB_HEADS // 2),
        in_specs=[spec, spec, spec] + [_const_spec(cst.shape) for cst in consts],
        out_specs=spec,
        scratch_shapes=[pltpu.VMEM((n_q, 2 * SB_BLOCK, 2 * SB_DH), BF16),
                        pltpu.VMEM((T + SB_BLOCK, 2 * SB_DH), BF16),
                        pltpu.VMEM((T + SB_BLOCK, SB_DH), BF16),
                        pltpu.VMEM((T + SB_BLOCK, SB_DH), BF16),
                        pltpu.VMEM((n_q, 2 * SB_DH, SB_BLOCK), F32),
                        pltpu.VMEM((n_q, 1, 2 * SB_BLOCK), F32),
                        pltpu.SMEM((n_q // SB_GROUP,), jnp.int32)],
        compiler_params=pltpu.CompilerParams(
            dimension_semantics=("parallel", "parallel"), vmem_limit_bytes=VMEM_LIMIT),
        name="stickbreak",
    )(sq, sk, sv, *consts)


def _conv_kernel(u_ref, w_ref, b_ref, lg_ref, lb_ref, o_ref, pad_ref, sh_ref):
    T = u_ref.shape[1]
    S = SUBLANES
    pad_ref[0:CONV_PAD, :] = jnp.zeros((CONV_PAD, CONV_CH), F32)
    pad_ref[CONV_PAD:, :] = u_ref[0]
    for r in range(1, S):
        sh_ref[r - 1, S:, :] = pad_ref[S - r:T + CONV_PAD - r, :]
    for i in range(T // CONV_TILE):
        acc = jnp.zeros((CONV_TILE, CONV_CH), F32)
        for j in range(CONV_WIDTH):
            a, r = divmod(CONV_WIDTH - 1 - j, S)
            start = i * CONV_TILE + CONV_PAD - S * a
            src = pad_ref[start:start + CONV_TILE, :] if r == 0 else sh_ref[r - 1, start:start + CONV_TILE, :]
            acc = acc + w_ref[j:j + 1, :] * src
        acc = acc + b_ref[...]
        mu = jnp.mean(acc, axis=-1, keepdims=True)
        xc = acc - mu
        var = jnp.mean(xc * xc, axis=-1, keepdims=True)
        y = xc * lax.rsqrt(var + EPS) * lg_ref[...] + lb_ref[...]
        o_ref[0, i * CONV_TILE:(i + 1) * CONV_TILE, :] = _silu(y).astype(o_ref.dtype)


def _conv(cu, w, b, ln_g, ln_b):
    B, T, C = cu.shape
    spec = pl.BlockSpec((1, T, C), lambda bi: (bi, 0, 0))
    vec = lambda a: a.reshape(1, C)
    return pl.pallas_call(
        _conv_kernel,
        out_shape=jax.ShapeDtypeStruct((B, T, C), BF16),
        grid=(B,),
        in_specs=[spec, _const_spec(w.shape), _const_spec((1, C)), _const_spec((1, C)), _const_spec((1, C))],
        out_specs=spec,
        scratch_shapes=[pltpu.VMEM((T + CONV_PAD, C), F32), pltpu.VMEM((SUBLANES - 1, T + CONV_PAD, C), F32)],
        compiler_params=pltpu.CompilerParams(
            dimension_semantics=("parallel",), vmem_limit_bytes=VMEM_LIMIT),
        name="conv",
    )(cu, w, vec(b), vec(ln_g), vec(ln_b))


def kernel(x, c, w_ada, b_ada, norm_ffn1, ffn1_w_in, ffn1_w_out, norm_mix, w_in, w_out, gla_w_gate_up, gla_b_gate, gla_out_norm, sb_q_norm, sb_k_norm, sb_out_norm, conv_w, conv_b, conv_ln_g, conv_ln_b, norm_ffn2, ffn2_w_in, ffn2_w_out):
    B, T, D = x.shape
    L = w_ada.shape[0]
    mod = _adaln(c, w_ada, b_ada).reshape(L, B, N_MOD, 1, D)
    x2 = x.reshape(B * T, D)
    ffn_w = (ffn1_w_in[0].astype(BF16), ffn1_w_out[0].astype(BF16))
    for l in range(L):
        sh1, sc1, gt1, sh2, sc2, gt2, sh3, sc3, gt3 = (mod[l, :, i] for i in range(N_MOD))
        x2, *ffn_w = _ffn(x2, norm_ffn1[l], sh1, sc1, gt1, *ffn_w, T, next_weights=(ffn2_w_in, ffn2_w_out, l))
        gq, gk, ga, gv, gg, sq, sk, sv, cu = _inproj(
            x2, norm_mix[l], sh2, sc2, w_in[l], gla_w_gate_up[l], gla_b_gate[l], sb_q_norm[l], sb_k_norm[l], T)
        seq = lambda a: a.reshape(B, T, a.shape[-1])
        flat = lambda a: a.reshape(B * T, a.shape[-1])
        oa = _gla(seq(gq), seq(gk), seq(ga), seq(gv), seq(gg), gla_out_norm[l])
        ob = _sb(seq(sq), seq(sk), seq(sv), sb_out_norm[l])
        oc = _conv(seq(cu), conv_w[l], conv_b[l], conv_ln_g[l], conv_ln_b[l])
        mix = (flat(oa), flat(ob), flat(oc), gt2, w_out[l].astype(BF16))
        if l + 1 < L:
            x2, *ffn_w = _ffn(x2, norm_ffn2[l], sh3, sc3, gt3, *ffn_w, T, mix=mix,
                              next_weights=(ffn1_w_in, ffn1_w_out, l + 1))
        else:
            x2 = _ffn(x2, norm_ffn2[l], sh3, sc3, gt3, *ffn_w, T, mix=mix)
    return x2.reshape(B, T, D)
```

```python
import functools

import numpy as np
import jax
import jax.numpy as jnp
from jax import lax
from jax.experimental import pallas as pl
from jax.experimental.pallas import tpu as pltpu

F32 = jnp.float32
BF16 = jnp.bfloat16

EPS = 1e-6
CHUNK = 64
GLA_HEADS, GLA_DK, GLA_DV, GLA_RANK, GLA_TAU = 4, 48, 96, 16, 16.0
SB_HEADS, SB_DH = 6, 64
CONV_CH, CONV_WIDTH = 256, 31
D_FF = 2816
N_MOD = 9
GLA_QK = GLA_HEADS * GLA_DK
GLA_V = GLA_HEADS * GLA_DV
SB_W = SB_HEADS * SB_DH

LANES = 128
SUBLANES = 8
BF16_ROWS = 16
QK_PAD = 256
RANK_PAD = 128
FF_CHUNK = 256
TOKEN_TILE = 512
SB_BLOCK = 128
SB_GROUP = 8
SB_SKIP = 104.0
GLA_TILE = 512
GLA_GROUP = 4
CONV_TILE = 256
CONV_PAD = 32
VMEM_LIMIT = 52 * 1024 * 1024

_OFF_GQ, _OFF_GK, _OFF_GV, _OFF_GG, _OFF_GR = 0, 256, 512, 896, 1280
_OFF_SQ, _OFF_SK, _OFF_SV, _OFF_CA, _OFF_CG = 1408, 1792, 2176, 2560, 2816
PROJ_W = 3072


def _dot(a, b):
    return jnp.dot(a, b, preferred_element_type=F32)


def _dot_nt(a, b):
    return lax.dot_general(a, b, (((1,), (1,)), ((), ())), preferred_element_type=F32)


def _dot_tn(a, b):
    return lax.dot_general(a, b, (((0,), (0,)), ((), ())), preferred_element_type=F32)


def _hi_lo(x, axis):
    hi = x.astype(BF16)
    lo = (x - hi.astype(F32)).astype(BF16)
    return jnp.concatenate([hi, lo], axis=axis)


def _split_dot(x, m2):
    return _dot(_hi_lo(x, 1), m2)


def _split_dot_left(m2, x):
    return _dot(m2, _hi_lo(x, 0))


def _sigmoid(x):
    return 1.0 / (1.0 + jnp.exp(-x))


def _silu(x):
    return x * _sigmoid(x)


def _log_sigmoid(x):
    return jnp.minimum(x, 0.0) - jnp.log(1.0 + jnp.exp(-jnp.abs(x)))


def _modulated_norm(x, g, scale, shift):
    ms = jnp.mean(x * x, axis=-1, keepdims=True)
    return (x * lax.rsqrt(ms + EPS) * g) * (1.0 + scale) + shift


def _adaln_kernel(c_ref, w_ref, b_ref, o_ref):
    ca = _silu(c_ref[...]).astype(BF16)
    o_ref[0] = _dot(ca, w_ref[0].astype(BF16)) + b_ref[0]


def _adaln(c, w_ada, b_ada):
    L, D, W = w_ada.shape
    B = c.shape[0]
    tn = 1536
    return pl.pallas_call(
        _adaln_kernel,
        out_shape=jax.ShapeDtypeStruct((L, B, W), F32),
        grid=(L, W // tn),
        in_specs=[
            pl.BlockSpec((B, D), lambda l, j: (0, 0)),
            pl.BlockSpec((1, D, tn), lambda l, j: (l, 0, j)),
            pl.BlockSpec((1, 1, tn), lambda l, j: (l, 0, j)),
        ],
        out_specs=pl.BlockSpec((1, B, tn), lambda l, j: (l, 0, j)),
        compiler_params=pltpu.CompilerParams(
            dimension_semantics=("parallel", "parallel"), vmem_limit_bytes=VMEM_LIMIT),
        name="adaln",
    )(c, w_ada, b_ada.reshape(L, 1, W))


def _ffn_kernel(has_mix, has_next, *refs):
    refs = list(refs)
    x_ref, g_ref, sh_ref, sc_ref, gt_ref, win_ref, wout_ref = refs[:7]
    del refs[:7]
    if has_mix:
        a_ref, b_ref, c_ref, gt2_ref, wa_ref, wb_ref, wc_ref = refs[:7]
        del refs[:7]
    if has_next:
        nwi_ref, nwo_ref, o_ref, nwi_out_ref, nwo_out_ref, acc_ref = refs
        nwi_out_ref[...] = nwi_ref[...].astype(BF16)
        nwo_out_ref[...] = nwo_ref[...].astype(BF16)
    else:
        o_ref, acc_ref = refs
    x = x_ref[...]
    if has_mix:
        y = _dot(a_ref[...], wa_ref[...]) + _dot(b_ref[...], wb_ref[...]) + _dot(c_ref[...], wc_ref[...])
        x = x + gt2_ref[0] * y
    hb = _modulated_norm(x, g_ref[...], sc_ref[0], sh_ref[0]).astype(BF16)
    d_ff = wout_ref.shape[0]
    for j in range(d_ff // FF_CHUNK):
        cols = slice(j * FF_CHUNK, (j + 1) * FF_CHUNK)
        a = _dot(hb, win_ref[:, cols])
        b = _dot(hb, win_ref[:, d_ff + j * FF_CHUNK:d_ff + (j + 1) * FF_CHUNK])
        y = _dot((_silu(a) * b).astype(BF16), wout_ref[cols, :])
        if j == 0:
            acc_ref[...] = y
        else:
            acc_ref[...] += y
    o_ref[...] = x + (0.5 * gt_ref[0]) * acc_ref[...]


def _const_spec(shape):
    nd = len(shape)
    return pl.BlockSpec(shape, lambda *_: (0,) * nd, pipeline_mode=pl.Buffered(1))


def _row_spec(tm, w):
    return pl.BlockSpec((tm, w), lambda i: (i, 0))


def _batch_vec_spec(tiles_per_batch, w):
    return pl.BlockSpec((1, 1, w), lambda i: (i // tiles_per_batch, 0, 0))


def _ffn(x2, g, shift, scale, gate, win, wout, seq, mix=None, next_weights=None):
    N, D = x2.shape
    tm = TOKEN_TILE
    tpb = seq // tm
    steps = N // tm
    in_specs = [
        _row_spec(tm, D),
        _const_spec((1, D)),
        _batch_vec_spec(tpb, D), _batch_vec_spec(tpb, D), _batch_vec_spec(tpb, D),
        _const_spec(win.shape), _const_spec(wout.shape),
    ]
    out_shape = [jax.ShapeDtypeStruct((N, D), F32)]
    out_specs = [_row_spec(tm, D)]
    args = [x2, g.reshape(1, D), shift, scale, gate, win, wout]
    if mix is not None:
        a, b, c, gate2, w_mix = mix
        parts, o = [], 0
        for t in (a, b, c):
            parts.append(w_mix[o:o + t.shape[1]])
            o += t.shape[1]
        in_specs += [_row_spec(tm, t.shape[1]) for t in (a, b, c)] + [_batch_vec_spec(tpb, D)]
        in_specs += [_const_spec(p.shape) for p in parts]
        args += [a, b, c, gate2] + parts
    if next_weights is not None:
        nwi, nwo, layer = next_weights
        in_rows = nwi.shape[1] // steps
        out_rows = LANES
        last = nwo.shape[1] // out_rows - 1
        assert nwi.shape[1] % steps == 0 and in_rows % BF16_ROWS == 0 and nwo.shape[1] % out_rows == 0
        assert last < steps
        in_specs += [pl.BlockSpec((None, in_rows, nwi.shape[2]), lambda i: (layer, i, 0)),
                     pl.BlockSpec((None, out_rows, nwo.shape[2]), lambda i: (layer, jnp.minimum(i, last), 0))]
        out_specs += [pl.BlockSpec((in_rows, nwi.shape[2]), lambda i: (i, 0)),
                      pl.BlockSpec((out_rows, nwo.shape[2]), lambda i: (jnp.minimum(i, last), 0))]
        out_shape += [jax.ShapeDtypeStruct(nwi.shape[1:], BF16), jax.ShapeDtypeStruct(nwo.shape[1:], BF16)]
        args += [nwi, nwo]
    outs = pl.pallas_call(
        functools.partial(_ffn_kernel, mix is not None, next_weights is not None),
        out_shape=out_shape,
        grid=(steps,),
        in_specs=in_specs,
        out_specs=out_specs,
        scratch_shapes=[pltpu.VMEM((tm, D), F32)],
        compiler_params=pltpu.CompilerParams(
            dimension_semantics=("arbitrary",), vmem_limit_bytes=VMEM_LIMIT),
        name="ffn",
    )(*args)
    return outs[0] if next_weights is None else tuple(outs)


def _head_rms(x, seg2, gain, head_dim):
    ms = _split_dot(x * x, seg2) * (1.0 / head_dim)
    return x * lax.rsqrt(ms + EPS) * gain


def _conv_module(u, first_of_batch, pad_ref, sh_ref, w_ref, b_ref, lg_ref, lb_ref, o_ref):
    tm = u.shape[0]
    S = SUBLANES
    tail = pad_ref[tm:tm + CONV_PAD, :]
    pad_ref[0:CONV_PAD, :] = jnp.where(first_of_batch, 0.0, tail)
    pad_ref[CONV_PAD:, :] = u
    for r in range(1, S):
        sh_ref[r - 1, S:, :] = pad_ref[S - r:tm + CONV_PAD - r, :]
    for i in range(tm // CONV_TILE):
        acc = jnp.zeros((CONV_TILE, CONV_CH), F32)
        for j in range(CONV_WIDTH):
            a, r = divmod(CONV_WIDTH - 1 - j, S)
            start = i * CONV_TILE + CONV_PAD - S * a
            src = pad_ref[start:start + CONV_TILE, :] if r == 0 else sh_ref[r - 1, start:start + CONV_TILE, :]
            acc = acc + w_ref[j:j + 1, :] * src
        acc = acc + b_ref[...]
        mu = jnp.mean(acc, axis=-1, keepdims=True)
        xc = acc - mu
        var = jnp.mean(xc * xc, axis=-1, keepdims=True)
        y = xc * lax.rsqrt(var + EPS) * lg_ref[...] + lb_ref[...]
        o_ref[i * CONV_TILE:(i + 1) * CONV_TILE, :] = _silu(y).astype(o_ref.dtype)


def _inproj_kernel(tiles_per_batch, x_ref, g_ref, sh_ref, sc_ref, w_ref, wup_ref, bg_ref, seg_ref, qn_ref, kn_ref,
                   cw_ref, cb_ref, lg_ref, lb_ref,
                   gq_ref, gk_ref, ga_ref, gv_ref, gg_ref, sq_ref, sk_ref, sv_ref, oc_ref, pad_ref, shift_ref):
    hb = _modulated_norm(x_ref[...], g_ref[...], sc_ref[0], sh_ref[0]).astype(BF16)
    pc = _dot(hb, w_ref[:, _OFF_CA:])
    u = pc[:, :CONV_CH] * _sigmoid(pc[:, CONV_CH:])
    first_of_batch = pl.program_id(0) % tiles_per_batch == 0
    _conv_module(u, first_of_batch, pad_ref, shift_ref, cw_ref, cb_ref, lg_ref, lb_ref, oc_ref)

    p = _dot(hb, w_ref[:, :_OFF_CA])
    gq_ref[...] = p[:, _OFF_GQ:_OFF_GQ + QK_PAD]
    gk_ref[...] = p[:, _OFF_GK:_OFF_GK + QK_PAD]
    gv_ref[...] = p[:, _OFF_GV:_OFF_GV + GLA_V]
    gg_ref[...] = p[:, _OFF_GG:_OFF_GG + GLA_V]
    r = p[:, _OFF_GR:_OFF_GR + RANK_PAD].astype(BF16)
    ga_ref[...] = _log_sigmoid(_dot(r, wup_ref[...]) + bg_ref[...]) * (1.0 / GLA_TAU)
    seg = seg_ref[...]
    sq = _head_rms(p[:, _OFF_SQ:_OFF_SQ + SB_W], seg, qn_ref[...], SB_DH)
    sq_ref[...] = (sq * (SB_DH ** -0.5)).astype(BF16)
    sk_ref[...] = _head_rms(p[:, _OFF_SK:_OFF_SK + SB_W], seg, kn_ref[...], SB_DH).astype(BF16)
    sv_ref[...] = p[:, _OFF_SV:_OFF_SV + SB_W].astype(BF16)


def _same_head(width, head_dim):
    h = np.arange(width) // head_dim
    m = (h[:, None] == h[None, :]).astype(np.float32)
    return jnp.asarray(np.concatenate([m, m], axis=0), BF16)


def _pad_cols(w, width):
    return jnp.pad(w, ((0, 0), (0, width - w.shape[1])))


def _inproj(x2, g, shift, scale, w_in, w_up, b_gate, q_norm, k_norm, conv_w, conv_b, conv_ln_g, conv_ln_b, seq):
    N, D = x2.shape
    o = 0
    parts = []
    for width, pad in ((GLA_QK, QK_PAD), (GLA_QK, QK_PAD), (GLA_V, GLA_V), (GLA_V, GLA_V),
                       (GLA_RANK, RANK_PAD), (SB_W, SB_W), (SB_W, SB_W), (SB_W, SB_W),
                       (CONV_CH, CONV_CH), (CONV_CH, CONV_CH)):
        parts.append(_pad_cols(w_in[:, o:o + width], pad))
        o += width
    w = jnp.concatenate(parts, axis=1).astype(BF16)
    wup = jnp.pad(w_up, ((0, RANK_PAD - GLA_RANK), (0, QK_PAD - GLA_QK))).astype(BF16)
    bg = jnp.pad(b_gate, (0, QK_PAD - GLA_QK)).reshape(1, QK_PAD)
    tm = TOKEN_TILE
    tpb = seq // tm
    seg = _same_head(SB_W, SB_DH)
    heads = lambda gn: jnp.tile(gn, SB_HEADS).reshape(1, SB_W)
    outs = ((QK_PAD, F32), (QK_PAD, F32), (QK_PAD, F32), (GLA_V, F32), (GLA_V, F32),
            (SB_W, BF16), (SB_W, BF16), (SB_W, BF16), (CONV_CH, BF16))
    chan = lambda a: a.reshape(1, CONV_CH)
    return pl.pallas_call(
        functools.partial(_inproj_kernel, tpb),
        out_shape=[jax.ShapeDtypeStruct((N, wd), dt) for wd, dt in outs],
        grid=(N // tm,),
        in_specs=[
            _row_spec(tm, D),
            _const_spec((1, D)),
            _batch_vec_spec(tpb, D), _batch_vec_spec(tpb, D),
            _const_spec(w.shape), _const_spec(wup.shape), _const_spec(bg.shape),
            _const_spec(seg.shape), _const_spec((1, SB_W)), _const_spec((1, SB_W)),
            _const_spec(conv_w.shape), _const_spec((1, CONV_CH)), _const_spec((1, CONV_CH)), _const_spec((1, CONV_CH)),
        ],
        out_specs=[_row_spec(tm, wd) for wd, _ in outs],
        scratch_shapes=[pltpu.VMEM((tm + CONV_PAD, CONV_CH), F32),
                        pltpu.VMEM((SUBLANES - 1, tm + CONV_PAD, CONV_CH), F32)],
        compiler_params=pltpu.CompilerParams(
            dimension_semantics=("arbitrary",), vmem_limit_bytes=VMEM_LIMIT),
        name="inproj",
    )(x2, g.reshape(1, D), shift, scale, w, wup, bg, seg, heads(q_norm), heads(k_norm),
      conv_w, chan(conv_b), chan(conv_ln_g), chan(conv_ln_b))


_GLA_LEVELS = 6


def _gla_constants():
    C = CHUNK
    t = np.arange(C)[:, None]
    j = np.arange(C)[None, :]
    mats = [j <= t, j > t]
    masks = [t == j]
    for l in range(1, _GLA_LEVELS + 1):
        n, m = 1 << l, 1 << (l - 1)
        ref = t // n * n + m - 1
        right = (t % n) >= m
        mats.append((right & (j > ref) & (j <= t)) | ((~right) & (j > t) & (j <= ref)))
        masks.append(((t // n) == (j // n)) & right & ((j % n) < m))
    prefix = np.concatenate(mats, axis=0).astype(np.float32)
    prefix = np.concatenate([prefix, prefix], axis=1)
    masks = np.stack([np.tile(mk, (1, GLA_HEADS)) for mk in masks]).astype(np.float32)
    kl = np.arange(QK_PAD)[None, :] // GLA_DK
    vl = np.arange(GLA_V)[:, None] // GLA_DV
    state_mask = (kl == vl).astype(np.float32)
    return prefix, masks, state_mask


def _gla_kernel(q_ref, k_ref, a_ref, v_ref, g_ref, pre_ref, msk_ref, smask_ref, seg_ref, on_ref,
                o_ref, st_ref):
    C = CHUNK

    @pl.when(pl.program_id(1) == 0)
    def _():
        st_ref[...] = jnp.zeros_like(st_ref)

    klane = lax.broadcasted_iota(jnp.int32, (1, QK_PAD), 1) // GLA_DK
    vlane = lax.broadcasted_iota(jnp.int32, (1, GLA_V), 1) // GLA_DV
    n_grp = GLA_GROUP

    def group(i, carry):
        base = i * (n_grp * C)
        rows = [pl.ds(pl.multiple_of(base + j * C, C), C) for j in range(n_grp)]
        qs = [q_ref[0, r, :] * (GLA_DK ** -0.5) for r in rows]
        ks = [k_ref[0, r, :] for r in rows]
        vs = [v_ref[0, r, :] for r in rows]

        es = []
        for r in rows:
            es.append(_split_dot_left(pre_ref[...], a_ref[0, r, :]))

        atts = []
        for q, k, e in zip(qs, ks, es):
            att = None
            for l in range(_GLA_LEVELS + 1):
                if l == 0:
                    qt, kt = q, k
                else:
                    f = jnp.exp(e[(l + 1) * C:(l + 2) * C])
                    qt, kt = q * f, k * f
                kst = jnp.concatenate(
                    [jnp.where(klane == h, kt, 0.0) for h in range(GLA_HEADS)], axis=0).astype(BF16)
                term = _dot_nt(qt.astype(BF16), kst) * msk_ref[l]
                att = term if att is None else att + term
            atts.append(att)

        intra, upds, qbs, decs = [], [], [], []
        for q, k, v, e, att in zip(qs, ks, vs, es, atts):
            vst = jnp.concatenate(
                [jnp.where(vlane == h, v, 0.0) for h in range(GLA_HEADS)], axis=0).astype(BF16)
            intra.append(_dot(att.astype(BF16), vst))
            kdec = (k * jnp.exp(e[C:2 * C])).astype(BF16)
            upds.append(_dot_tn(v.astype(BF16), kdec) * smask_ref[...])
            qbs.append((q * jnp.exp(e[0:C])).astype(BF16))
            decs.append(jnp.exp(e[C - 1:C]))

        st = st_ref[...]
        outs = []
        for o_intra, upd, qb, dec in zip(intra, upds, qbs, decs):
            outs.append(o_intra + _dot_nt(qb, st.astype(BF16)))
            st = st * dec + upd
        st_ref[...] = st

        o = jnp.concatenate(outs, axis=0)
        out_rows = pl.ds(pl.multiple_of(base, n_grp * C), n_grp * C)
        y = _head_rms(o, seg_ref[...], on_ref[...], GLA_DV)
        o_ref[0, out_rows, :] = (y * _silu(g_ref[0, out_rows, :])).astype(o_ref.dtype)
        return carry

    lax.fori_loop(0, q_ref.shape[1] // (n_grp * C), group, 0)


def _gla(gq, gk, ga, gv, gg, out_norm):
    B, T, _ = gq.shape
    prefix, masks, state_mask = _gla_constants()
    tt = GLA_TILE
    seq_spec = lambda w: pl.BlockSpec((1, tt, w), lambda b, i: (b, i, 0))
    consts = (jnp.asarray(prefix, BF16), jnp.asarray(masks), jnp.asarray(state_mask),
              _same_head(GLA_V, GLA_DV), jnp.tile(out_norm, GLA_HEADS).reshape(1, GLA_V))
    return pl.pallas_call(
        _gla_kernel,
        out_shape=jax.ShapeDtypeStruct((B, T, GLA_V), BF16),
        grid=(B, T // tt),
        in_specs=[seq_spec(QK_PAD), seq_spec(QK_PAD), seq_spec(QK_PAD), seq_spec(GLA_V), seq_spec(GLA_V)]
                 + [_const_spec(cst.shape) for cst in consts],
        out_specs=seq_spec(GLA_V),
        scratch_shapes=[pltpu.VMEM((GLA_V, QK_PAD), F32)],
        compiler_params=pltpu.CompilerParams(
            dimension_semantics=("parallel", "arbitrary"), vmem_limit_bytes=VMEM_LIMIT),
        name="gla",
    )(gq, gk, ga, gv, gg, *consts)


def _sb_kernel(q_ref, k_ref, v_ref, og_ref, seg_ref, causal_ref, suf_ref, o_ref,
               qs_ref, kn_ref, v0_ref, v1_ref, acc_ref, carry_ref, alive_ref):
    T = q_ref.shape[1]
    BLK, G = SB_BLOCK, SB_GROUP
    n_q = T // BLK
    n_groups = n_q // G
    first = lax.broadcasted_iota(jnp.int32, (1, 2 * SB_DH), 1) < SB_DH

    kn_ref[0:BLK, :] = jnp.zeros((BLK, 2 * SB_DH), BF16)
    kn_ref[BLK:, :] = k_ref[0]
    v0_ref[0:BLK, :] = jnp.zeros((BLK, SB_DH), BF16)
    v1_ref[0:BLK, :] = jnp.zeros((BLK, SB_DH), BF16)
    v0_ref[BLK:, :] = v_ref[0, :, 0:SB_DH]
    v1_ref[BLK:, :] = v_ref[0, :, SB_DH:]

    for qi in range(n_q):
        q2 = q_ref[0, qi * BLK:(qi + 1) * BLK, :]
        zq = jnp.zeros_like(q2)
        qs_ref[qi, 0:BLK, :] = jnp.where(first, q2, zq)
        qs_ref[qi, BLK:, :] = jnp.where(first, zq, q2)

    def group_step(grp, dist, causal):
        qis = [grp * G + g for g in range(G)]
        krows = [pl.ds(pl.multiple_of(jnp.maximum(qi - dist + 1, 0) * BLK, BLK), BLK) for qi in qis]
        zs = [_dot_nt(kn_ref[kr, :], qs_ref[qi]) for qi, kr in zip(qis, krows)]
        log_betas, drops = [], []
        for z in zs:
            sp = jnp.maximum(z, 0.0) + jnp.log(1.0 + jnp.exp(-jnp.abs(z)))
            log_betas.append(z - sp)
            drops.append(sp * causal_ref[...] if causal else sp)
        sufs = [_split_dot_left(suf_ref[...], d) for d in drops]
        ws, alive = [], None
        for qi, lb, d, suf in zip(qis, log_betas, drops, sufs):
            carry = jnp.zeros((1, 2 * BLK), F32) if causal else carry_ref[qi]
            w = jnp.exp(lb - suf + carry)
            ws.append((w * causal_ref[...] if causal else w).astype(BF16))
            carry = carry - (suf[0:1, :] + d[0:1, :])
            carry_ref[qi] = carry
            alive = carry if alive is None else jnp.maximum(alive, carry)
        for qi, kr, w in zip(qis, krows, ws):
            for h, v_ref_h in enumerate((v0_ref, v1_ref)):
                upd = _dot_tn(v_ref_h[kr, :], w[:, h * BLK:(h + 1) * BLK])
                out = (qi, slice(h * SB_DH, (h + 1) * SB_DH), slice(None))
                acc_ref[out] = upd if causal else acc_ref[out] + upd
        return jnp.max(alive) > -SB_SKIP

    def diagonal(grp, c0):
        alive_ref[grp] = group_step(grp, 0, True).astype(jnp.int32)
        return c0

    lax.fori_loop(0, n_groups, diagonal, 0)

    def any_alive():
        total = alive_ref[0]
        for g in range(1, n_groups):
            total = total + alive_ref[g]
        return total > 0

    def sweep(state):
        dist, _ = state

        def visit(grp, c0):
            run = jnp.logical_and(alive_ref[grp] > 0, grp * G + G - 1 >= dist)

            @pl.when(run)
            def _():
                alive_ref[grp] = group_step(grp, dist, False).astype(jnp.int32)

            @pl.when(jnp.logical_not(run))
            def _():
                alive_ref[grp] = 0

            return c0

        lax.fori_loop(0, n_groups, visit, 0)
        return dist + 1, any_alive()

    lax.while_loop(lambda s: jnp.logical_and(s[0] < n_q, s[1]), sweep, (1, any_alive()))

    def finish(grp, c0):
        cols = pl.ds(pl.multiple_of(grp * G * BLK, G * BLK), G * BLK)
        ot = jnp.concatenate([acc_ref[grp * G + g] for g in range(G)], axis=1)
        o_ref[0, cols, :] = _head_rms(ot.T, seg_ref[...], og_ref[...], SB_DH).astype(o_ref.dtype)
        return c0

    lax.fori_loop(0, n_groups, finish, 0)


def _sb_constants():
    BLK = SB_BLOCK
    s = np.arange(BLK)[:, None]
    j = np.arange(BLK)[None, :]
    suffix = (j > s).astype(np.float32)
    suffix = np.concatenate([suffix, suffix], axis=1)
    causal =(s < np.tile(np.arange(BLK), 2)[None, :]).astype(np.float32)
    return causal, suffix


def _sb(sq, sk, sv, out_norm):
    B, T, _ = sq.shape
    causal, suffix = _sb_constants()
    consts = (jnp.tile(out_norm, 2).reshape(1, 2 * SB_DH), _same_head(2 * SB_DH, SB_DH),
              jnp.asarray(causal), jnp.asarray(suffix, BF16))
    spec = pl.BlockSpec((1, T, 2 * SB_DH), lambda b, p: (b, 0, p))
    assert T % (SB_GROUP * SB_BLOCK) == 0, (T, SB_GROUP, SB_BLOCK)
    n_q = T // SB_BLOCK
    return pl.pallas_call(
        _sb_kernel,
        out_shape=jax.ShapeDtypeStruct((B, T, SB_W), BF16),
        grid=(B, SB_HEADS // 2),
        in_specs=[spec, spec, spec] + [_const_spec(cst.shape) for cst in consts],
        out_specs=spec,
        scratch_shapes=[pltpu.VMEM((n_q, 2 * SB_BLOCK, 2 * SB_DH), BF16),
                        pltpu.VMEM((T + SB_BLOCK, 2 * SB_DH), BF16),
                        pltpu.VMEM((T + SB_BLOCK, SB_DH), BF16),
                        pltpu.VMEM((T + SB_BLOCK, SB_DH), BF16),
                        pltpu.VMEM((n_q, 2 * SB_DH, SB_BLOCK), F32),
                        pltpu.VMEM((n_q, 1, 2 * SB_BLOCK), F32),
                        pltpu.SMEM((n_q // SB_GROUP,), jnp.int32)],
        compiler_params=pltpu.CompilerParams(
            dimension_semantics=("parallel", "parallel"), vmem_limit_bytes=VMEM_LIMIT),
        name="stickbreak",
    )(sq, sk, sv, *consts)


def kernel(x, c, w_ada, b_ada, norm_ffn1, ffn1_w_in, ffn1_w_out, norm_mix, w_in, w_out, gla_w_gate_up, gla_b_gate, gla_out_norm, sb_q_norm, sb_k_norm, sb_out_norm, conv_w, conv_b, conv_ln_g, conv_ln_b, norm_ffn2, ffn2_w_in, ffn2_w_out):
    B, T, D = x.shape
    L = w_ada.shape[0]
    mod = _adaln(c, w_ada, b_ada).reshape(L, B, N_MOD, 1, D)
    x2 = x.reshape(B * T, D)
    ffn_w = (ffn1_w_in[0].astype(BF16), ffn1_w_out[0].astype(BF16))
    for l in range(L):
        sh1, sc1, gt1, sh2, sc2, gt2, sh3, sc3, gt3 = (mod[l, :, i] for i in range(N_MOD))
        x2, *ffn_w = _ffn(x2, norm_ffn1[l], sh1, sc1, gt1, *ffn_w, T, next_weights=(ffn2_w_in, ffn2_w_out, l))
        gq, gk, ga, gv, gg, sq, sk, sv, oc = _inproj(
            x2, norm_mix[l], sh2, sc2, w_in[l], gla_w_gate_up[l], gla_b_gate[l], sb_q_norm[l], sb_k_norm[l],
            conv_w[l], conv_b[l], conv_ln_g[l], conv_ln_b[l], T)
        seq = lambda a: a.reshape(B, T, a.shape[-1])
        flat = lambda a: a.reshape(B * T, a.shape[-1])
        oa = _gla(seq(gq), seq(gk), seq(ga), seq(gv), seq(gg), gla_out_norm[l])
        ob = _sb(seq(sq), seq(sk), seq(sv), sb_out_norm[l])
        mix = (flat(oa), flat(ob), oc, gt2, w_out[l].astype(BF16))
        if l + 1 < L:
            x2, *ffn_w = _ffn(x2, norm_ffn2[l], sh3, sc3, gt3, *ffn_w, T, mix=mix,
                              next_weights=(ffn1_w_in, ffn1_w_out, l + 1))
        else:
            x2 = _ffn(x2, norm_ffn2[l], sh3, sc3, gt3, *ffn_w, T, mix=mix)
    return x2.reshape(B, T, D)
```

```python
import functools

import numpy as np
import jax
import jax.numpy as jnp
from jax import lax
from jax.experimental import pallas as pl
from jax.experimental.pallas import tpu as pltpu

F32 = jnp.float32
BF16 = jnp.bfloat16

EPS = 1e-6
CHUNK = 64
GLA_HEADS, GLA_DK, GLA_DV, GLA_RANK, GLA_TAU = 4, 48, 96, 16, 16.0
SB_HEADS, SB_DH = 6, 64
CONV_CH, CONV_WIDTH = 256, 31
D_FF = 2816
N_MOD = 9
GLA_QK = GLA_HEADS * GLA_DK
GLA_V = GLA_HEADS * GLA_DV
SB_W = SB_HEADS * SB_DH

LANES = 128
SUBLANES = 8
BF16_ROWS = 16
QK_PAD = 256
RANK_PAD = 128
FF_CHUNK = 256
TOKEN_TILE = 512
SB_BLOCK = 128
SB_GROUP = 8
SB_SKIP = 104.0
GLA_TILE = 512
GLA_GROUP = 4
CONV_TILE = 256
CONV_PAD = 32
VMEM_LIMIT = 52 * 1024 * 1024

_OFF_GQ, _OFF_GK, _OFF_GV, _OFF_GG, _OFF_GR = 0, 256, 512, 896, 1280
_OFF_SQ, _OFF_SK, _OFF_SV, _OFF_CA, _OFF_CG = 1408, 1792, 2176, 2560, 2816
PROJ_W = 3072


def _dot(a, b):
    return jnp.dot(a, b, preferred_element_type=F32)


def _dot_nt(a, b):
    return lax.dot_general(a, b, (((1,), (1,)), ((), ())), preferred_element_type=F32)


def _dot_tn(a, b):
    return lax.dot_general(a, b, (((0,), (0,)), ((), ())), preferred_element_type=F32)


def _hi_lo(x, axis):
    hi = x.astype(BF16)
    lo = (x - hi.astype(F32)).astype(BF16)
    return jnp.concatenate([hi, lo], axis=axis)


def _split_dot(x, m2):
    return _dot(_hi_lo(x, 1), m2)


def _split_dot_left(m2, x):
    return _dot(m2, _hi_lo(x, 0))


def _sigmoid(x):
    return 1.0 / (1.0 + jnp.exp(-x))


def _silu(x):
    return x * _sigmoid(x)


def _log_sigmoid(x):
    return jnp.minimum(x, 0.0) - jnp.log(1.0 + jnp.exp(-jnp.abs(x)))


def _modulated_norm(x, g, scale, shift):
    ms = jnp.mean(x * x, axis=-1, keepdims=True)
    return (x * lax.rsqrt(ms + EPS) * g) * (1.0 + scale) + shift


def _adaln_kernel(c_ref, w_ref, b_ref, o_ref):
    ca = _silu(c_ref[...]).astype(BF16)
    o_ref[0] = _dot(ca, w_ref[0].astype(BF16)) + b_ref[0]


def _adaln(c, w_ada, b_ada):
    L, D, W = w_ada.shape
    B = c.shape[0]
    tn = 1536
    return pl.pallas_call(
        _adaln_kernel,
        out_shape=jax.ShapeDtypeStruct((L, B, W), F32),
        grid=(L, W // tn),
        in_specs=[
            pl.BlockSpec((B, D), lambda l, j: (0, 0)),
            pl.BlockSpec((1, D, tn), lambda l, j: (l, 0, j)),
            pl.BlockSpec((1, 1, tn), lambda l, j: (l, 0, j)),
        ],
        out_specs=pl.BlockSpec((1, B, tn), lambda l, j: (l, 0, j)),
        compiler_params=pltpu.CompilerParams(
            dimension_semantics=("parallel", "parallel"), vmem_limit_bytes=VMEM_LIMIT),
        name="adaln",
    )(c, w_ada, b_ada.reshape(L, 1, W))


def _ffn_kernel(has_mix, has_next, *refs):
    refs = list(refs)
    x_ref, g_ref, sh_ref, sc_ref, gt_ref, win_ref, wout_ref = refs[:7]
    del refs[:7]
    if has_mix:
        a_ref, b_ref, c_ref, gt2_ref, wa_ref, wb_ref, wc_ref = refs[:7]
        del refs[:7]
    if has_next:
        nwi_ref, nwo_ref, o_ref, nwi_out_ref, nwo_out_ref, acc_ref = refs
        nwi_out_ref[...] = nwi_ref[...].astype(BF16)
        nwo_out_ref[...] = nwo_ref[...].astype(BF16)
    else:
        o_ref, acc_ref = refs
    x = x_ref[...]
    if has_mix:
        y = _dot(a_ref[...], wa_ref[...]) + _dot(b_ref[...], wb_ref[...]) + _dot(c_ref[...], wc_ref[...])
        x = x + gt2_ref[0] * y
    hb = _modulated_norm(x, g_ref[...], sc_ref[0], sh_ref[0]).astype(BF16)
    d_ff = wout_ref.shape[0]
    for j in range(d_ff // FF_CHUNK):
        cols = slice(j * FF_CHUNK, (j + 1) * FF_CHUNK)
        a = _dot(hb, win_ref[:, cols])
        b = _dot(hb, win_ref[:, d_ff + j * FF_CHUNK:d_ff + (j + 1) * FF_CHUNK])
        y = _dot((_silu(a) * b).astype(BF16), wout_ref[cols, :])
        if j == 0:
            acc_ref[...] = y
        else:
            acc_ref[...] += y
    o_ref[...] = x + (0.5 * gt_ref[0]) * acc_ref[...]


def _const_spec(shape):
    nd = len(shape)
    return pl.BlockSpec(shape, lambda *_: (0,) * nd, pipeline_mode=pl.Buffered(1))


def _row_spec(tm, w):
    return pl.BlockSpec((tm, w), lambda i: (i, 0))


def _batch_vec_spec(tiles_per_batch, w):
    return pl.BlockSpec((1, 1, w), lambda i: (i // tiles_per_batch, 0, 0))


def _ffn(x2, g, shift, scale, gate, win, wout, seq, mix=None, next_weights=None):
    N, D = x2.shape
    tm = TOKEN_TILE
    tpb = seq // tm
    steps = N // tm
    in_specs = [
        _row_spec(tm, D),
        _const_spec((1, D)),
        _batch_vec_spec(tpb, D), _batch_vec_spec(tpb, D), _batch_vec_spec(tpb, D),
        _const_spec(win.shape), _const_spec(wout.shape),
    ]
    out_shape = [jax.ShapeDtypeStruct((N, D), F32)]
    out_specs = [_row_spec(tm, D)]
    args = [x2, g.reshape(1, D), shift, scale, gate, win, wout]
    if mix is not None:
        a, b, c, gate2, w_mix = mix
        parts, o = [], 0
        for t in (a, b, c):
            parts.append(w_mix[o:o + t.shape[1]])
            o += t.shape[1]
        in_specs += [_row_spec(tm, t.shape[1]) for t in (a, b, c)] + [_batch_vec_spec(tpb, D)]
        in_specs += [_const_spec(p.shape) for p in parts]
        args += [a, b, c, gate2] + parts
    if next_weights is not None:
        nwi, nwo, layer = next_weights
        in_rows = nwi.shape[1] // steps
        out_rows = LANES
        last = nwo.shape[1] // out_rows - 1
        assert nwi.shape[1] % steps == 0 and in_rows % BF16_ROWS == 0 and nwo.shape[1] % out_rows == 0
        assert last < steps
        in_specs += [pl.BlockSpec((None, in_rows, nwi.shape[2]), lambda i: (layer, i, 0)),
                     pl.BlockSpec((None, out_rows, nwo.shape[2]), lambda i: (layer, jnp.minimum(i, last), 0))]
        out_specs += [pl.BlockSpec((in_rows, nwi.shape[2]), lambda i: (i, 0)),
                      pl.BlockSpec((out_rows, nwo.shape[2]), lambda i: (jnp.minimum(i, last), 0))]
        out_shape += [jax.ShapeDtypeStruct(nwi.shape[1:], BF16), jax.ShapeDtypeStruct(nwo.shape[1:], BF16)]
        args += [nwi, nwo]
    outs = pl.pallas_call(
        functools.partial(_ffn_kernel, mix is not None, next_weights is not None),
        out_shape=out_shape,
        grid=(steps,),
        in_specs=in_specs,
        out_specs=out_specs,
        scratch_shapes=[pltpu.VMEM((tm, D), F32)],
        compiler_params=pltpu.CompilerParams(
            dimension_semantics=("arbitrary",), vmem_limit_bytes=VMEM_LIMIT),
        name="ffn",
    )(*args)
    return outs[0] if next_weights is None else tuple(outs)


def _head_rms(x, seg2, gain, head_dim):
    ms = _split_dot(x * x, seg2) * (1.0 / head_dim)
    return x * lax.rsqrt(ms + EPS) * gain


def _conv_module(pad_ref, sh_ref, w_ref, b_ref, lg_ref, lb_ref, o_ref):
    tm = pad_ref.shape[0] - CONV_PAD
    S = SUBLANES
    for r in range(1, S):
        sh_ref[r - 1, S:, :] = pad_ref[S - r:tm + CONV_PAD - r, :]
    for i in range(tm // CONV_TILE):
        acc = jnp.zeros((CONV_TILE, CONV_CH), F32)
        for j in range(CONV_WIDTH):
            a, r = divmod(CONV_WIDTH - 1 - j, S)
            start = i * CONV_TILE + CONV_PAD - S * a
            src = pad_ref[start:start + CONV_TILE, :] if r == 0 else sh_ref[r - 1, start:start + CONV_TILE, :]
            acc = acc + w_ref[j:j + 1, :] * src
        acc = acc + b_ref[...]
        mu = jnp.mean(acc, axis=-1, keepdims=True)
        xc = acc - mu
        var = jnp.mean(xc * xc, axis=-1, keepdims=True)
        y = xc * lax.rsqrt(var + EPS) * lg_ref[...] + lb_ref[...]
        o_ref[i * CONV_TILE:(i + 1) * CONV_TILE, :] = _silu(y).astype(o_ref.dtype)


def _inproj_kernel(tiles_per_batch, x_ref, g_ref, sh_ref, sc_ref, w_ref, wup_ref, bg_ref, seg_ref, qn_ref, kn_ref,
                   cw_ref, cb_ref, lg_ref, lb_ref,
                   gq_ref, gk_ref, ga_ref, gv_ref, gg_ref, sq_ref, sk_ref, sv_ref, oc_ref, pad_ref, shift_ref):
    hb = _modulated_norm(x_ref[...], g_ref[...], sc_ref[0], sh_ref[0]).astype(BF16)
    pc = _dot(hb, w_ref[:, _OFF_CA:])
    tm = x_ref.shape[0]
    tail = pad_ref[tm:tm + CONV_PAD, :]
    pad_ref[0:CONV_PAD, :] = jnp.where(pl.program_id(0) % tiles_per_batch == 0, 0.0, tail)
    pad_ref[CONV_PAD:, :] = pc[:, :CONV_CH] * _sigmoid(pc[:, CONV_CH:])
    _conv_module(pad_ref, shift_ref, cw_ref, cb_ref, lg_ref, lb_ref, oc_ref)

    p = _dot(hb, w_ref[:, :_OFF_CA])
    gq_ref[...] = p[:, _OFF_GQ:_OFF_GQ + QK_PAD]
    gk_ref[...] = p[:, _OFF_GK:_OFF_GK + QK_PAD]
    gv_ref[...] = p[:, _OFF_GV:_OFF_GV + GLA_V]
    gg_ref[...] = p[:, _OFF_GG:_OFF_GG + GLA_V]
    r = p[:, _OFF_GR:_OFF_GR + RANK_PAD].astype(BF16)
    ga_ref[...] = _log_sigmoid(_dot(r, wup_ref[...]) + bg_ref[...]) * (1.0 / GLA_TAU)
    seg = seg_ref[...]
    sq = _head_rms(p[:, _OFF_SQ:_OFF_SQ + SB_W], seg, qn_ref[...], SB_DH)
    sq_ref[...] = (sq * (SB_DH ** -0.5)).astype(BF16)
    sk_ref[...] = _head_rms(p[:, _OFF_SK:_OFF_SK + SB_W], seg, kn_ref[...], SB_DH).astype(BF16)
    sv_ref[...] = p[:, _OFF_SV:_OFF_SV + SB_W].astype(BF16)


def _same_head(width, head_dim):
    h = np.arange(width) // head_dim
    m = (h[:, None] == h[None, :]).astype(np.float32)
    return jnp.asarray(np.concatenate([m, m], axis=0), BF16)


def _pad_cols(w, width):
    return jnp.pad(w, ((0, 0), (0, width - w.shape[1])))


def _inproj(x2, g, shift, scale, w_in, w_up, b_gate, q_norm, k_norm, conv_w, conv_b, conv_ln_g, conv_ln_b, seq):
    N, D = x2.shape
    o = 0
    parts = []
    for width, pad in ((GLA_QK, QK_PAD), (GLA_QK, QK_PAD), (GLA_V, GLA_V), (GLA_V, GLA_V),
                       (GLA_RANK, RANK_PAD), (SB_W, SB_W), (SB_W, SB_W), (SB_W, SB_W),
                       (CONV_CH, CONV_CH), (CONV_CH, CONV_CH)):
        parts.append(_pad_cols(w_in[:, o:o + width], pad))
        o += width
    w = jnp.concatenate(parts, axis=1).astype(BF16)
    wup = jnp.pad(w_up, ((0, RANK_PAD - GLA_RANK), (0, QK_PAD - GLA_QK))).astype(BF16)
    bg = jnp.pad(b_gate, (0, QK_PAD - GLA_QK)).reshape(1, QK_PAD)
    tm = TOKEN_TILE
    tpb = seq // tm
    seg = _same_head(SB_W, SB_DH)
    heads = lambda gn: jnp.tile(gn, SB_HEADS).reshape(1, SB_W)
    outs = ((QK_PAD, F32), (QK_PAD, F32), (QK_PAD, F32), (GLA_V, F32), (GLA_V, F32),
            (SB_W, BF16), (SB_W, BF16), (SB_W, BF16), (CONV_CH, BF16))
    chan = lambda a: a.reshape(1, CONV_CH)
    return pl.pallas_call(
        functools.partial(_inproj_kernel, tpb),
        out_shape=[jax.ShapeDtypeStruct((N, wd), dt) for wd, dt in outs],
        grid=(N // tm,),
        in_specs=[
            _row_spec(tm, D),
            _const_spec((1, D)),
            _batch_vec_spec(tpb, D), _batch_vec_spec(tpb, D),
            _const_spec(w.shape), _const_spec(wup.shape), _const_spec(bg.shape),
            _const_spec(seg.shape), _const_spec((1, SB_W)), _const_spec((1, SB_W)),
            _const_spec(conv_w.shape), _const_spec((1, CONV_CH)), _const_spec((1, CONV_CH)), _const_spec((1, CONV_CH)),
        ],
        out_specs=[_row_spec(tm, wd) for wd, _ in outs],
        scratch_shapes=[pltpu.VMEM((tm + CONV_PAD, CONV_CH), F32),
                        pltpu.VMEM((SUBLANES - 1, tm + CONV_PAD, CONV_CH), F32)],
        compiler_params=pltpu.CompilerParams(
            dimension_semantics=("arbitrary",), vmem_limit_bytes=VMEM_LIMIT),
        name="inproj",
    )(x2, g.reshape(1, D), shift, scale, w, wup, bg, seg, heads(q_norm), heads(k_norm),
      conv_w, chan(conv_b), chan(conv_ln_g), chan(conv_ln_b))


_GLA_LEVELS = 6


def _gla_constants():
    C = CHUNK
    t = np.arange(C)[:, None]
    j = np.arange(C)[None, :]
    mats = [j <= t, j > t]
    masks = [t == j]
    for l in range(1, _GLA_LEVELS + 1):
        n, m = 1 << l, 1 << (l - 1)
        ref = t // n * n + m - 1
        right = (t % n) >= m
        mats.append((right & (j > ref) & (j <= t)) | ((~right) & (j > t) & (j <= ref)))
        masks.append(((t // n) == (j // n)) & right & ((j % n) < m))
    prefix = np.concatenate(mats, axis=0).astype(np.float32)
    prefix = np.concatenate([prefix, prefix], axis=1)
    masks = np.stack([np.tile(mk, (1, GLA_HEADS)) for mk in masks]).astype(np.float32)
    kl = np.arange(QK_PAD)[None, :] // GLA_DK
    vl = np.arange(GLA_V)[:, None] // GLA_DV
    state_mask = (kl == vl).astype(np.float32)
    return prefix, masks, state_mask


def _gla_kernel(q_ref, k_ref, a_ref, v_ref, g_ref, pre_ref, msk_ref, smask_ref, seg_ref, on_ref,
                o_ref, st_ref):
    C = CHUNK

    @pl.when(pl.program_id(1) == 0)
    def _():
        st_ref[...] = jnp.zeros_like(st_ref)

    klane = lax.broadcasted_iota(jnp.int32, (1, QK_PAD), 1) // GLA_DK
    vlane = lax.broadcasted_iota(jnp.int32, (1, GLA_V), 1) // GLA_DV
    n_grp = GLA_GROUP

    def group(i, carry):
        base = i * (n_grp * C)
        rows = [pl.ds(pl.multiple_of(base + j * C, C), C) for j in range(n_grp)]
        qs = [q_ref[0, r, :] * (GLA_DK ** -0.5) for r in rows]
        ks = [k_ref[0, r, :] for r in rows]
        vs = [v_ref[0, r, :] for r in rows]

        es = []
        for r in rows:
            es.append(_split_dot_left(pre_ref[...], a_ref[0, r, :]))

        atts = []
        for q, k, e in zip(qs, ks, es):
            att = None
            for l in range(_GLA_LEVELS + 1):
                if l == 0:
                    qt, kt = q, k
                else:
                    f = jnp.exp(e[(l + 1) * C:(l + 2) * C])
                    qt, kt = q * f, k * f
                kst = jnp.concatenate(
                    [jnp.where(klane == h, kt, 0.0) for h in range(GLA_HEADS)], axis=0).astype(BF16)
                term = _dot_nt(qt.astype(BF16), kst) * msk_ref[l]
                att = term if att is None else att + term
            atts.append(att)

        intra, upds, qbs, decs = [], [], [], []
        for q, k, v, e, att in zip(qs, ks, vs, es, atts):
            vst = jnp.concatenate(
                [jnp.where(vlane == h, v, 0.0) for h in range(GLA_HEADS)], axis=0).astype(BF16)
            intra.append(_dot(att.astype(BF16), vst))
            kdec = (k * jnp.exp(e[C:2 * C])).astype(BF16)
            upds.append(_dot_tn(v.astype(BF16), kdec) * smask_ref[...])
            qbs.append((q * jnp.exp(e[0:C])).astype(BF16))
            decs.append(jnp.exp(e[C - 1:C]))

        st = st_ref[...]
        outs = []
        for o_intra, upd, qb, dec in zip(intra, upds, qbs, decs):
            outs.append(o_intra + _dot_nt(qb, st.astype(BF16)))
            st = st * dec + upd
        st_ref[...] = st

        o = jnp.concatenate(outs, axis=0)
        out_rows = pl.ds(pl.multiple_of(base, n_grp * C), n_grp * C)
        y = _head_rms(o, seg_ref[...], on_ref[...], GLA_DV)
        o_ref[0, out_rows, :] = (y * _silu(g_ref[0, out_rows, :])).astype(o_ref.dtype)
        return carry

    lax.fori_loop(0, q_ref.shape[1] // (n_grp * C), group, 0)


def _gla(gq, gk, ga, gv, gg, out_norm):
    B, T, _ = gq.shape
    prefix, masks, state_mask = _gla_constants()
    tt = GLA_TILE
    seq_spec = lambda w: pl.BlockSpec((1, tt, w), lambda b, i: (b, i, 0))
    consts = (jnp.asarray(prefix, BF16), jnp.asarray(masks), jnp.asarray(state_mask),
              _same_head(GLA_V, GLA_DV), jnp.tile(out_norm, GLA_HEADS).reshape(1, GLA_V))
    return pl.pallas_call(
        _gla_kernel,
        out_shape=jax.ShapeDtypeStruct((B, T, GLA_V), BF16),
        grid=(B, T // tt),
        in_specs=[seq_spec(QK_PAD), seq_spec(QK_PAD), seq_spec(QK_PAD), seq_spec(GLA_V), seq_spec(GLA_V)]
                 + [_const_spec(cst.shape) for cst in consts],
        out_specs=seq_spec(GLA_V),
        scratch_shapes=[pltpu.VMEM((GLA_V, QK_PAD), F32)],
        compiler_params=pltpu.CompilerParams(
            dimension_semantics=("parallel", "arbitrary"), vmem_limit_bytes=VMEM_LIMIT),
        name="gla",
    )(gq, gk, ga, gv, gg, *consts)


def _sb_kernel(q_ref, k_ref, v_ref, og_ref, seg_ref, causal_ref, suf_ref, o_ref,
               qs_ref, kn_ref, v0_ref, v1_ref, acc_ref, carry_ref, alive_ref):
    T = q_ref.shape[1]
    BLK, G = SB_BLOCK, SB_GROUP
    n_q = T // BLK
    n_groups = n_q // G
    first = lax.broadcasted_iota(jnp.int32, (1, 2 * SB_DH), 1) < SB_DH

    kn_ref[0:BLK, :] = jnp.zeros((BLK, 2 * SB_DH), BF16)
    kn_ref[BLK:, :] = k_ref[0]
    v0_ref[0:BLK, :] = jnp.zeros((BLK, SB_DH), BF16)
    v1_ref[0:BLK, :] = jnp.zeros((BLK, SB_DH), BF16)
    v0_ref[BLK:, :] = v_ref[0, :, 0:SB_DH]
    v1_ref[BLK:, :] = v_ref[0, :, SB_DH:]

    for qi in range(n_q):
        q2 = q_ref[0, qi * BLK:(qi + 1) * BLK, :]
        zq = jnp.zeros_like(q2)
        qs_ref[qi, 0:BLK, :] = jnp.where(first, q2, zq)
        qs_ref[qi, BLK:, :] = jnp.where(first, zq, q2)

    def group_step(grp, dist, causal):
        qis = [grp * G + g for g in range(G)]
        krows = [pl.ds(pl.multiple_of(jnp.maximum(qi - dist + 1, 0) * BLK, BLK), BLK) for qi in qis]
        zs = [_dot_nt(kn_ref[kr, :], qs_ref[qi]) for qi, kr in zip(qis, krows)]
        log_betas, drops = [], []
        for z in zs:
            sp = jnp.maximum(z, 0.0) + jnp.log(1.0 + jnp.exp(-jnp.abs(z)))
            log_betas.append(z - sp)
            drops.append(sp * causal_ref[...] if causal else sp)
        sufs = [_split_dot_left(suf_ref[...], d) for d in drops]
        ws, alive = [], None
        for qi, lb, d, suf in zip(qis, log_betas, drops, sufs):
            carry = jnp.zeros((1, 2 * BLK), F32) if causal else carry_ref[qi]
            w = jnp.exp(lb - suf + carry)
            ws.append((w * causal_ref[...] if causal else w).astype(BF16))
            carry = carry - (suf[0:1, :] + d[0:1, :])
            carry_ref[qi] = carry
            alive = carry if alive is None else jnp.maximum(alive, carry)
        for qi, kr, w in zip(qis, krows, ws):
            for h, v_ref_h in enumerate((v0_ref, v1_ref)):
                upd = _dot_tn(v_ref_h[kr, :], w[:, h * BLK:(h + 1) * BLK])
                out = (qi, slice(h * SB_DH, (h + 1) * SB_DH), slice(None))
                acc_ref[out] = upd if causal else acc_ref[out] + upd
        return jnp.max(alive) > -SB_SKIP

    def diagonal(grp, c0):
        alive_ref[grp] = group_step(grp, 0, True).astype(jnp.int32)
        return c0

    lax.fori_loop(0, n_groups, diagonal, 0)

    def any_alive():
        total = alive_ref[0]
        for g in range(1, n_groups):
            total = total + alive_ref[g]
        return total > 0

    def sweep(state):
        dist, _ = state

        def visit(grp, c0):
            run = jnp.logical_and(alive_ref[grp] > 0, grp * G + G - 1 >= dist)

            @pl.when(run)
            def _():
                alive_ref[grp] = group_step(grp, dist, False).astype(jnp.int32)

            @pl.when(jnp.logical_not(run))
            def _():
                alive_ref[grp] = 0

            return c0

        lax.fori_loop(0, n_groups, visit, 0)
        return dist + 1, any_alive()

    lax.while_loop(lambda s: jnp.logical_and(s[0] < n_q, s[1]), sweep, (1, any_alive()))

    def finish(grp, c0):
        cols = pl.ds(pl.multiple_of(grp * G * BLK, G * BLK), G * BLK)
        ot = jnp.concatenate([acc_ref[grp * G + g] for g in range(G)], axis=1)
        o_ref[0, cols, :] = _head_rms(ot.T, seg_ref[...], og_ref[...], SB_DH).astype(o_ref.dtype)
        return c0

    lax.fori_loop(0, n_groups, finish, 0)


def _sb_constants():
    BLK = SB_BLOCK
    s = np.arange(BLK)[:, None]
    j = np.arange(BLK)[None, :]
    suffix = (j > s).astype(np.float32)
    suffix = np.concatenate([suffix, suffix], axis=1)
    causal =(s < np.tile(np.arange(BLK), 2)[None, :]).astype(np.float32)
    return causal, suffix


def _sb(sq, sk, sv, out_norm):
    B, T, _ = sq.shape
    causal, suffix = _sb_constants()
    consts = (jnp.tile(out_norm, 2).reshape(1, 2 * SB_DH), _same_head(2 * SB_DH, SB_DH),
              jnp.asarray(causal), jnp.asarray(suffix, BF16))
    spec = pl.BlockSpec((1, T, 2 * SB_DH), lambda b, p: (b, 0, p))
    assert T % (SB_GROUP * SB_BLOCK) == 0, (T, SB_GROUP, SB_BLOCK)
    n_q = T // SB_BLOCK
    return pl.pallas_call(
        _sb_kernel,
        out_shape=jax.ShapeDtypeStruct((B, T, SB_W), BF16),
        grid=(B, SB_HEADS // 2),
        in_specs=[spec, spec, spec] + [_const_spec(cst.shape) for cst in consts],
        out_specs=spec,
        scratch_shapes=[pltpu.VMEM((n_q, 2 * SB_BLOCK, 2 * SB_DH), BF16),
                        pltpu.VMEM((T + SB_BLOCK, 2 * SB_DH), BF16),
                        pltpu.VMEM((T + SB_BLOCK, SB_DH), BF16),
                        pltpu.VMEM((T + SB_BLOCK, SB_DH), BF16),
                        pltpu.VMEM((n_q, 2 * SB_DH, SB_BLOCK), F32),
                        pltpu.VMEM((n_q, 1, 2 * SB_BLOCK), F32),
                        pltpu.SMEM((n_q // SB_GROUP,), jnp.int32)],
        compiler_params=pltpu.CompilerParams(
            dimension_semantics=("parallel", "parallel"), vmem_limit_bytes=VMEM_LIMIT),
        name="stickbreak",
    )(sq, sk, sv, *consts)


def kernel(x, c, w_ada, b_ada, norm_ffn1, ffn1_w_in, ffn1_w_out, norm_mix, w_in, w_out, gla_w_gate_up, gla_b_gate, gla_out_norm, sb_q_norm, sb_k_norm, sb_out_norm, conv_w, conv_b, conv_ln_g, conv_ln_b, norm_ffn2, ffn2_w_in, ffn2_w_out):
    B, T, D = x.shape
    L = w_ada.shape[0]
    mod = _adaln(c, w_ada, b_ada).reshape(L, B, N_MOD, 1, D)
    x2 = x.reshape(B * T, D)
    ffn_w = (ffn1_w_in[0].astype(BF16), ffn1_w_out[0].astype(BF16))
    for l in range(L):
        sh1, sc1, gt1, sh2, sc2, gt2, sh3, sc3, gt3 = (mod[l, :, i] for i in range(N_MOD))
        x2, *ffn_w = _ffn(x2, norm_ffn1[l], sh1, sc1, gt1, *ffn_w, T, next_weights=(ffn2_w_in, ffn2_w_out, l))
        gq, gk, ga, gv, gg, sq, sk, sv, oc = _inproj(
            x2, norm_mix[l], sh2, sc2, w_in[l], gla_w_gate_up[l], gla_b_gate[l], sb_q_norm[l], sb_k_norm[l],
            conv_w[l], conv_b[l], conv_ln_g[l], conv_ln_b[l], T)
        seq = lambda a: a.reshape(B, T, a.shape[-1])
        flat = lambda a: a.reshape(B * T, a.shape[-1])
        oa = _gla(seq(gq), seq(gk), seq(ga), seq(gv), seq(gg), gla_out_norm[l])
        ob = _sb(seq(sq), seq(sk), seq(sv), sb_out_norm[l])
        mix = (flat(oa), flat(ob), oc, gt2, w_out[l].astype(BF16))
        if l + 1 < L:
            x2, *ffn_w = _ffn(x2, norm_ffn2[l], sh3, sc3, gt3, *ffn_w, T, mix=mix,
                              next_weights=(ffn1_w_in, ffn1_w_out, l + 1))
        else:
            x2 = _ffn(x2, norm_ffn2[l], sh3, sc3, gt3, *ffn_w, T, mix=mix)
    return x2.reshape(B, T, D)
```

```python
import functools

import numpy as np
import jax
import jax.numpy as jnp
from jax import lax
from jax.experimental import pallas as pl
from jax.experimental.pallas import tpu as pltpu

F32 = jnp.float32
BF16 = jnp.bfloat16

EPS = 1e-6
CHUNK = 64
GLA_HEADS, GLA_DK, GLA_DV, GLA_RANK, GLA_TAU = 4, 48, 96, 16, 16.0
SB_HEADS, SB_DH = 6, 64
CONV_CH, CONV_WIDTH = 256, 31
D_FF = 2816
N_MOD = 9
GLA_QK = GLA_HEADS * GLA_DK
GLA_V = GLA_HEADS * GLA_DV
SB_W = SB_HEADS * SB_DH

LANES = 128
SUBLANES = 8
BF16_ROWS = 16
QK_PAD = 256
RANK_PAD = 128
FF_CHUNK = 256
TOKEN_TILE = 512
SB_BLOCK = 128
SB_GROUP = 8
SB_SKIP = 104.0
GLA_TILE = 512
GLA_GROUP = 4
CONV_TILE = 256
ADALN_TILE = 1536
CONV_PAD = 32
VMEM_LIMIT = 52 * 1024 * 1024

_OFF_GQ, _OFF_GK, _OFF_GV, _OFF_GG, _OFF_GR = 0, 256, 512, 896, 1280
_OFF_SQ, _OFF_SK, _OFF_SV, _OFF_CA, _OFF_CG = 1408, 1792, 2176, 2560, 2816
PROJ_W = 3072


def _dot(a, b):
    return jnp.dot(a, b, preferred_element_type=F32)


def _dot_nt(a, b):
    return lax.dot_general(a, b, (((1,), (1,)), ((), ())), preferred_element_type=F32)


def _dot_tn(a, b):
    return lax.dot_general(a, b, (((0,), (0,)), ((), ())), preferred_element_type=F32)


def _hi_lo(x, axis):
    hi = x.astype(BF16)
    lo = (x - hi.astype(F32)).astype(BF16)
    return jnp.concatenate([hi, lo], axis=axis)


def _split_dot(x, m2):
    return _dot(_hi_lo(x, 1), m2)


def _split_dot_left(m2, x):
    return _dot(m2, _hi_lo(x, 0))


def _sigmoid(x):
    return 1.0 / (1.0 + jnp.exp(-x))


def _silu(x):
    return x * _sigmoid(x)


def _log_sigmoid(x):
    return jnp.minimum(x, 0.0) - jnp.log(1.0 + jnp.exp(-jnp.abs(x)))


def _modulated_norm(x, g, scale, shift):
    ms = jnp.mean(x * x, axis=-1, keepdims=True)
    return (x * lax.rsqrt(ms + EPS) * g) * (1.0 + scale) + shift


def _adaln_kernel(c_ref, w_ref, b_ref, o_ref):
    ca = _silu(c_ref[...]).astype(BF16)
    o_ref[0] = _dot(ca, w_ref[0].astype(BF16)) + b_ref[0]


def _adaln(c, w_ada, b_ada):
    L, D, W = w_ada.shape
    B = c.shape[0]
    tn = ADALN_TILE
    assert W % tn == 0, (W, tn)
    return pl.pallas_call(
        _adaln_kernel,
        out_shape=jax.ShapeDtypeStruct((L, B, W), F32),
        grid=(L, W // tn),
        in_specs=[
            pl.BlockSpec((B, D), lambda l, j: (0, 0)),
            pl.BlockSpec((1, D, tn), lambda l, j: (l, 0, j)),
            pl.BlockSpec((1, 1, tn), lambda l, j: (l, 0, j)),
        ],
        out_specs=pl.BlockSpec((1, B, tn), lambda l, j: (l, 0, j)),
        compiler_params=pltpu.CompilerParams(
            dimension_semantics=("parallel", "parallel"), vmem_limit_bytes=VMEM_LIMIT),
        name="adaln",
    )(c, w_ada, b_ada.reshape(L, 1, W))


def _ffn_kernel(has_mix, has_next, *refs):
    refs = list(refs)
    x_ref, g_ref, sh_ref, sc_ref, gt_ref, win_ref, wout_ref = refs[:7]
    del refs[:7]
    if has_mix:
        a_ref, b_ref, c_ref, gt2_ref, wa_ref, wb_ref, wc_ref = refs[:7]
        del refs[:7]
    if has_next:
        nwi_ref, nwo_ref, o_ref, nwi_out_ref, nwo_out_ref, acc_ref = refs
        nwi_out_ref[...] = nwi_ref[...].astype(BF16)
        nwo_out_ref[...] = nwo_ref[...].astype(BF16)
    else:
        o_ref, acc_ref = refs
    x = x_ref[...]
    if has_mix:
        y = _dot(a_ref[...], wa_ref[...]) + _dot(b_ref[...], wb_ref[...]) + _dot(c_ref[...], wc_ref[...])
        x = x + gt2_ref[0] * y
    hb = _modulated_norm(x, g_ref[...], sc_ref[0], sh_ref[0]).astype(BF16)
    d_ff = wout_ref.shape[0]
    for j in range(d_ff // FF_CHUNK):
        cols = slice(j * FF_CHUNK, (j + 1) * FF_CHUNK)
        a = _dot(hb, win_ref[:, cols])
        b = _dot(hb, win_ref[:, d_ff + j * FF_CHUNK:d_ff + (j + 1) * FF_CHUNK])
        y = _dot((_silu(a) * b).astype(BF16), wout_ref[cols, :])
        if j == 0:
            acc_ref[...] = y
        else:
            acc_ref[...] += y
    o_ref[...] = x + (0.5 * gt_ref[0]) * acc_ref[...]


def _const_spec(shape):
    nd = len(shape)
    return pl.BlockSpec(shape, lambda *_: (0,) * nd, pipeline_mode=pl.Buffered(1))


def _row_spec(tm, w):
    return pl.BlockSpec((tm, w), lambda i: (i, 0))


def _batch_vec_spec(tiles_per_batch, w):
    return pl.BlockSpec((1, 1, w), lambda i: (i // tiles_per_batch, 0, 0))


def _ffn(x2, g, shift, scale, gate, win, wout, seq, mix=None, next_weights=None):
    N, D = x2.shape
    tm = TOKEN_TILE
    tpb = seq // tm
    steps = N // tm
    in_specs = [
        _row_spec(tm, D),
        _const_spec((1, D)),
        _batch_vec_spec(tpb, D), _batch_vec_spec(tpb, D), _batch_vec_spec(tpb, D),
        _const_spec(win.shape), _const_spec(wout.shape),
    ]
    out_shape = [jax.ShapeDtypeStruct((N, D), F32)]
    out_specs = [_row_spec(tm, D)]
    args = [x2, g.reshape(1, D), shift, scale, gate, win, wout]
    if mix is not None:
        a, b, c, gate2, w_mix = mix
        parts, o = [], 0
        for t in (a, b, c):
            parts.append(w_mix[o:o + t.shape[1]])
            o += t.shape[1]
        in_specs += [_row_spec(tm, t.shape[1]) for t in (a, b, c)] + [_batch_vec_spec(tpb, D)]
        in_specs += [_const_spec(p.shape) for p in parts]
        args += [a, b, c, gate2] + parts
    if next_weights is not None:
        nwi, nwo, layer = next_weights
        in_rows = nwi.shape[1] // steps
        out_rows = LANES
        last = nwo.shape[1] // out_rows - 1
        assert nwi.shape[1] % steps == 0 and in_rows % BF16_ROWS == 0 and nwo.shape[1] % out_rows == 0
        assert last < steps
        in_specs += [pl.BlockSpec((None, in_rows, nwi.shape[2]), lambda i: (layer, i, 0)),
                     pl.BlockSpec((None, out_rows, nwo.shape[2]), lambda i: (layer, jnp.minimum(i, last), 0))]
        out_specs += [pl.BlockSpec((in_rows, nwi.shape[2]), lambda i: (i, 0)),
                      pl.BlockSpec((out_rows, nwo.shape[2]), lambda i: (jnp.minimum(i, last), 0))]
        out_shape += [jax.ShapeDtypeStruct(nwi.shape[1:], BF16), jax.ShapeDtypeStruct(nwo.shape[1:], BF16)]
        args += [nwi, nwo]
    outs = pl.pallas_call(
        functools.partial(_ffn_kernel, mix is not None, next_weights is not None),
        out_shape=out_shape,
        grid=(steps,),
        in_specs=in_specs,
        out_specs=out_specs,
        scratch_shapes=[pltpu.VMEM((tm, D), F32)],
        compiler_params=pltpu.CompilerParams(
            dimension_semantics=("arbitrary",), vmem_limit_bytes=VMEM_LIMIT),
        name="ffn",
    )(*args)
    return outs[0] if next_weights is None else tuple(outs)


def _head_rms(x, seg2, gain, head_dim):
    ms = _split_dot(x * x, seg2) * (1.0 / head_dim)
    return x * lax.rsqrt(ms + EPS) * gain


def _conv_module(pad_ref, sh_ref, w_ref, b_ref, lg_ref, lb_ref, o_ref):
    tm = pad_ref.shape[0] - CONV_PAD
    S = SUBLANES
    for r in range(1, S):
        sh_ref[r - 1, S:, :] = pad_ref[S - r:tm + CONV_PAD - r, :]
    for i in range(tm // CONV_TILE):
        acc = jnp.zeros((CONV_TILE, CONV_CH), F32)
        for j in range(CONV_WIDTH):
            a, r = divmod(CONV_WIDTH - 1 - j, S)
            start = i * CONV_TILE + CONV_PAD - S * a
            src = pad_ref[start:start + CONV_TILE, :] if r == 0 else sh_ref[r - 1, start:start + CONV_TILE, :]
            acc = acc + w_ref[j:j + 1, :] * src
        acc = acc + b_ref[...]
        mu = jnp.mean(acc, axis=-1, keepdims=True)
        xc = acc - mu
        var = jnp.mean(xc * xc, axis=-1, keepdims=True)
        y = xc * lax.rsqrt(var + EPS) * lg_ref[...] + lb_ref[...]
        o_ref[i * CONV_TILE:(i + 1) * CONV_TILE, :] = _silu(y).astype(o_ref.dtype)


def _inproj_kernel(tiles_per_batch, x_ref, g_ref, sh_ref, sc_ref, w_ref, wup_ref, bg_ref, seg_ref, qn_ref, kn_ref,
                   cw_ref, cb_ref, lg_ref, lb_ref,
                   gq_ref, gk_ref, ga_ref, gv_ref, gg_ref, sq_ref, sk_ref, sv_ref, oc_ref, pad_ref, shift_ref):
    hb = _modulated_norm(x_ref[...], g_ref[...], sc_ref[0], sh_ref[0]).astype(BF16)
    pc = _dot(hb, w_ref[:, _OFF_CA:])
    tm = x_ref.shape[0]
    tail = pad_ref[tm:tm + CONV_PAD, :]
    pad_ref[0:CONV_PAD, :] = jnp.where(pl.program_id(0) % tiles_per_batch == 0, 0.0, tail)
    pad_ref[CONV_PAD:, :] = pc[:, :CONV_CH] * _sigmoid(pc[:, CONV_CH:])
    _conv_module(pad_ref, shift_ref, cw_ref, cb_ref, lg_ref, lb_ref, oc_ref)

    p = _dot(hb, w_ref[:, :_OFF_CA])
    gq_ref[...] = p[:, _OFF_GQ:_OFF_GQ + QK_PAD]
    gk_ref[...] = p[:, _OFF_GK:_OFF_GK + QK_PAD]
    gv_ref[...] = p[:, _OFF_GV:_OFF_GV + GLA_V]
    gg_ref[...] = p[:, _OFF_GG:_OFF_GG + GLA_V]
    r = p[:, _OFF_GR:_OFF_GR + RANK_PAD].astype(BF16)
    ga_ref[...] = _log_sigmoid(_dot(r, wup_ref[...]) + bg_ref[...]) * (1.0 / GLA_TAU)
    seg = seg_ref[...]
    sq = _head_rms(p[:, _OFF_SQ:_OFF_SQ + SB_W], seg, qn_ref[...], SB_DH)
    sq_ref[...] = (sq * (SB_DH ** -0.5)).astype(BF16)
    sk_ref[...] = _head_rms(p[:, _OFF_SK:_OFF_SK + SB_W], seg, kn_ref[...], SB_DH).astype(BF16)
    sv_ref[...] = p[:, _OFF_SV:_OFF_SV + SB_W].astype(BF16)


def _same_head(width, head_dim):
    h = np.arange(width) // head_dim
    m = (h[:, None] == h[None, :]).astype(np.float32)
    return jnp.asarray(np.concatenate([m, m], axis=0), BF16)


def _pad_cols(w, width):
    return jnp.pad(w, ((0, 0), (0, width - w.shape[1])))


def _inproj(x2, g, shift, scale, w_in, w_up, b_gate, q_norm, k_norm, conv_w, conv_b, conv_ln_g, conv_ln_b, seq):
    N, D = x2.shape
    o = 0
    parts = []
    for width, pad in ((GLA_QK, QK_PAD), (GLA_QK, QK_PAD), (GLA_V, GLA_V), (GLA_V, GLA_V),
                       (GLA_RANK, RANK_PAD), (SB_W, SB_W), (SB_W, SB_W), (SB_W, SB_W),
                       (CONV_CH, CONV_CH), (CONV_CH, CONV_CH)):
        parts.append(_pad_cols(w_in[:, o:o + width], pad))
        o += width
    w = jnp.concatenate(parts, axis=1).astype(BF16)
    wup = jnp.pad(w_up, ((0, RANK_PAD - GLA_RANK), (0, QK_PAD - GLA_QK))).astype(BF16)
    bg = jnp.pad(b_gate, (0, QK_PAD - GLA_QK)).reshape(1, QK_PAD)
    tm = TOKEN_TILE
    tpb = seq // tm
    seg = _same_head(SB_W, SB_DH)
    heads = lambda gn: jnp.tile(gn, SB_HEADS).reshape(1, SB_W)
    outs = ((QK_PAD, F32), (QK_PAD, F32), (QK_PAD, F32), (GLA_V, F32), (GLA_V, F32),
            (SB_W, BF16), (SB_W, BF16), (SB_W, BF16), (CONV_CH, BF16))
    chan = lambda a: a.reshape(1, CONV_CH)
    return pl.pallas_call(
        functools.partial(_inproj_kernel, tpb),
        out_shape=[jax.ShapeDtypeStruct((N, wd), dt) for wd, dt in outs],
        grid=(N // tm,),
        in_specs=[
            _row_spec(tm, D),
            _const_spec((1, D)),
            _batch_vec_spec(tpb, D), _batch_vec_spec(tpb, D),
            _const_spec(w.shape), _const_spec(wup.shape), _const_spec(bg.shape),
            _const_spec(seg.shape), _const_spec((1, SB_W)), _const_spec((1, SB_W)),
            _const_spec(conv_w.shape), _const_spec((1, CONV_CH)), _const_spec((1, CONV_CH)), _const_spec((1, CONV_CH)),
        ],
        out_specs=[_row_spec(tm, wd) for wd, _ in outs],
        scratch_shapes=[pltpu.VMEM((tm + CONV_PAD, CONV_CH), F32),
                        pltpu.VMEM((SUBLANES - 1, tm + CONV_PAD, CONV_CH), F32)],
        compiler_params=pltpu.CompilerParams(
            dimension_semantics=("arbitrary",), vmem_limit_bytes=VMEM_LIMIT),
        name="inproj",
    )(x2, g.reshape(1, D), shift, scale, w, wup, bg, seg, heads(q_norm), heads(k_norm),
      conv_w, chan(conv_b), chan(conv_ln_g), chan(conv_ln_b))


_GLA_LEVELS = 6


def _gla_constants():
    C = CHUNK
    t = np.arange(C)[:, None]
    j = np.arange(C)[None, :]
    mats = [j <= t, j > t]
    masks = [t == j]
    for l in range(1, _GLA_LEVELS + 1):
        n, m = 1 << l, 1 << (l - 1)
        ref = t // n * n + m - 1
        right = (t % n) >= m
        mats.append((right & (j > ref) & (j <= t)) | ((~right) & (j > t) & (j <= ref)))
        masks.append(((t // n) == (j // n)) & right & ((j % n) < m))
    prefix = np.concatenate(mats, axis=0).astype(np.float32)
    prefix = np.concatenate([prefix, prefix], axis=1)
    masks = np.stack([np.tile(mk, (1, GLA_HEADS)) for mk in masks]).astype(np.float32)
    kl = np.arange(QK_PAD)[None, :] // GLA_DK
    vl = np.arange(GLA_V)[:, None] // GLA_DV
    state_mask = (kl == vl).astype(np.float32)
    return prefix, masks, state_mask


def _gla_kernel(q_ref, k_ref, a_ref, v_ref, g_ref, pre_ref, msk_ref, smask_ref, seg_ref, on_ref,
                o_ref, st_ref):
    C = CHUNK

    @pl.when(pl.program_id(1) == 0)
    def _():
        st_ref[...] = jnp.zeros_like(st_ref)

    klane = lax.broadcasted_iota(jnp.int32, (1, QK_PAD), 1) // GLA_DK
    vlane = lax.broadcasted_iota(jnp.int32, (1, GLA_V), 1) // GLA_DV
    n_grp = GLA_GROUP

    def group(i, carry):
        base = i * (n_grp * C)
        rows = [pl.ds(pl.multiple_of(base + j * C, C), C) for j in range(n_grp)]
        qs = [q_ref[0, r, :] * (GLA_DK ** -0.5) for r in rows]
        ks = [k_ref[0, r, :] for r in rows]
        vs = [v_ref[0, r, :] for r in rows]

        es = []
        for r in rows:
            es.append(_split_dot_left(pre_ref[...], a_ref[0, r, :]))

        atts = []
        for q, k, e in zip(qs, ks, es):
            att = None
            for l in range(_GLA_LEVELS + 1):
                if l == 0:
                    qt, kt = q, k
                else:
                    f = jnp.exp(e[(l + 1) * C:(l + 2) * C])
                    qt, kt = q * f, k * f
                kst = jnp.concatenate(
                    [jnp.where(klane == h, kt, 0.0) for h in range(GLA_HEADS)], axis=0).astype(BF16)
                term = _dot_nt(qt.astype(BF16), kst) * msk_ref[l]
                att = term if att is None else att + term
            atts.append(att)

        intra, upds, qbs, decs = [], [], [], []
        for q, k, v, e, att in zip(qs, ks, vs, es, atts):
            vst = jnp.concatenate(
                [jnp.where(vlane == h, v, 0.0) for h in range(GLA_HEADS)], axis=0).astype(BF16)
            intra.append(_dot(att.astype(BF16), vst))
            kdec = (k * jnp.exp(e[C:2 * C])).astype(BF16)
            upds.append(_dot_tn(v.astype(BF16), kdec) * smask_ref[...])
            qbs.append((q * jnp.exp(e[0:C])).astype(BF16))
            decs.append(jnp.exp(e[C - 1:C]))

        st = st_ref[...]
        outs = []
        for o_intra, upd, qb, dec in zip(intra, upds, qbs, decs):
            outs.append(o_intra + _dot_nt(qb, st.astype(BF16)))
            st = st * dec + upd
        st_ref[...] = st

        o = jnp.concatenate(outs, axis=0)
        out_rows = pl.ds(pl.multiple_of(base, n_grp * C), n_grp * C)
        y = _head_rms(o, seg_ref[...], on_ref[...], GLA_DV)
        o_ref[0, out_rows, :] = (y * _silu(g_ref[0, out_rows, :])).astype(o_ref.dtype)
        return carry

    lax.fori_loop(0, q_ref.shape[1] // (n_grp * C), group, 0)


def _gla(gq, gk, ga, gv, gg, out_norm):
    B, T, _ = gq.shape
    prefix, masks, state_mask = _gla_constants()
    tt = GLA_TILE
    seq_spec = lambda w: pl.BlockSpec((1, tt, w), lambda b, i: (b, i, 0))
    consts = (jnp.asarray(prefix, BF16), jnp.asarray(masks), jnp.asarray(state_mask),
              _same_head(GLA_V, GLA_DV), jnp.tile(out_norm, GLA_HEADS).reshape(1, GLA_V))
    return pl.pallas_call(
        _gla_kernel,
        out_shape=jax.ShapeDtypeStruct((B, T, GLA_V), BF16),
        grid=(B, T // tt),
        in_specs=[seq_spec(QK_PAD), seq_spec(QK_PAD), seq_spec(QK_PAD), seq_spec(GLA_V), seq_spec(GLA_V)]
                 + [_const_spec(cst.shape) for cst in consts],
        out_specs=seq_spec(GLA_V),
        scratch_shapes=[pltpu.VMEM((GLA_V, QK_PAD), F32)],
        compiler_params=pltpu.CompilerParams(
            dimension_semantics=("parallel", "arbitrary"), vmem_limit_bytes=VMEM_LIMIT),
        name="gla",
    )(gq, gk, ga, gv, gg, *consts)


def _sb_kernel(q_ref, k_ref, v_ref, og_ref, seg_ref, causal_ref, suf_ref, o_ref,
               qs_ref, kn_ref, v0_ref, v1_ref, acc_ref, carry_ref, alive_ref):
    T = q_ref.shape[1]
    BLK, G = SB_BLOCK, SB_GROUP
    n_q = T // BLK
    n_groups = n_q // G
    first = lax.broadcasted_iota(jnp.int32, (1, 2 * SB_DH), 1) < SB_DH

    kn_ref[0:BLK, :] = jnp.zeros((BLK, 2 * SB_DH), BF16)
    kn_ref[BLK:, :] = k_ref[0]
    v0_ref[0:BLK, :] = jnp.zeros((BLK, SB_DH), BF16)
    v1_ref[0:BLK, :] = jnp.zeros((BLK, SB_DH), BF16)
    v0_ref[BLK:, :] = v_ref[0, :, 0:SB_DH]
    v1_ref[BLK:, :] = v_ref[0, :, SB_DH:]

    for qi in range(n_q):
        q2 = q_ref[0, qi * BLK:(qi + 1) * BLK, :]
        zq = jnp.zeros_like(q2)
        qs_ref[qi, 0:BLK, :] = jnp.where(first, q2, zq)
        qs_ref[qi, BLK:, :] = jnp.where(first, zq, q2)

    def group_step(grp, dist, causal):
        qis = [grp * G + g for g in range(G)]
        krows = [pl.ds(pl.multiple_of(jnp.maximum(qi - dist + 1, 0) * BLK, BLK), BLK) for qi in qis]
        zs = [_dot_nt(kn_ref[kr, :], qs_ref[qi]) for qi, kr in zip(qis, krows)]
        log_betas, drops = [], []
        for z in zs:
            sp = jnp.maximum(z, 0.0) + jnp.log(1.0 + jnp.exp(-jnp.abs(z)))
            log_betas.append(z - sp)
            drops.append(sp * causal_ref[...] if causal else sp)
        sufs = [_split_dot_left(suf_ref[...], d) for d in drops]
        ws, alive = [], None
        for qi, lb, d, suf in zip(qis, log_betas, drops, sufs):
            carry = jnp.zeros((1, 2 * BLK), F32) if causal else carry_ref[qi]
            w = jnp.exp(lb - suf + carry)
            ws.append((w * causal_ref[...] if causal else w).astype(BF16))
            carry = carry - (suf[0:1, :] + d[0:1, :])
            carry_ref[qi] = carry
            alive = carry if alive is None else jnp.maximum(alive, carry)
        for qi, kr, w in zip(qis, krows, ws):
            for h, v_ref_h in enumerate((v0_ref, v1_ref)):
                upd = _dot_tn(v_ref_h[kr, :], w[:, h * BLK:(h + 1) * BLK])
                out = (qi, slice(h * SB_DH, (h + 1) * SB_DH), slice(None))
                acc_ref[out] = upd if causal else acc_ref[out] + upd
        return jnp.max(alive) > -SB_SKIP

    def diagonal(grp, c0):
        alive_ref[grp] = group_step(grp, 0, True).astype(jnp.int32)
        return c0

    lax.fori_loop(0, n_groups, diagonal, 0)

    def any_alive():
        total = alive_ref[0]
        for g in range(1, n_groups):
            total = total + alive_ref[g]
        return total > 0

    def sweep(state):
        dist, _ = state

        def visit(grp, c0):
            run = jnp.logical_and(alive_ref[grp] > 0, grp * G + G - 1 >= dist)

            @pl.when(run)
            def _():
                alive_ref[grp] = group_step(grp, dist, False).astype(jnp.int32)

            @pl.when(jnp.logical_not(run))
            def _():
                alive_ref[grp] = 0

            return c0

        lax.fori_loop(0, n_groups, visit, 0)
        return dist + 1, any_alive()

    lax.while_loop(lambda s: jnp.logical_and(s[0] < n_q, s[1]), sweep, (1, any_alive()))

    def finish(grp, c0):
        cols = pl.ds(pl.multiple_of(grp * G * BLK, G * BLK), G * BLK)
        ot = jnp.concatenate([acc_ref[grp * G + g] for g in range(G)], axis=1)
        o_ref[0, cols, :] = _head_rms(ot.T, seg_ref[...], og_ref[...], SB_DH).astype(o_ref.dtype)
        return c0

    lax.fori_loop(0, n_groups, finish, 0)


def _sb_constants():
    BLK = SB_BLOCK
    s = np.arange(BLK)[:, None]
    j = np.arange(BLK)[None, :]
    suffix = (j > s).astype(np.float32)
    suffix = np.concatenate([suffix, suffix], axis=1)
    causal =(s < np.tile(np.arange(BLK), 2)[None, :]).astype(np.float32)
    return causal, suffix


def _sb(sq, sk, sv, out_norm):
    B, T, _ = sq.shape
    causal, suffix = _sb_constants()
    consts = (jnp.tile(out_norm, 2).reshape(1, 2 * SB_DH), _same_head(2 * SB_DH, SB_DH),
              jnp.asarray(causal), jnp.asarray(suffix, BF16))
    spec = pl.BlockSpec((1, T, 2 * SB_DH), lambda b, p: (b, 0, p))
    assert T % (SB_GROUP * SB_BLOCK) == 0, (T, SB_GROUP, SB_BLOCK)
    n_q = T // SB_BLOCK
    return pl.pallas_call(
        _sb_kernel,
        out_shape=jax.ShapeDtypeStruct((B, T, SB_W), BF16),
        grid=(B, SB_HEADS // 2),
        in_specs=[spec, spec, spec] + [_const_spec(cst.shape) for cst in consts],
        out_specs=spec,
        scratch_shapes=[pltpu.VMEM((n_q, 2 * SB_BLOCK, 2 * SB_DH), BF16),
                        pltpu.VMEM((T + SB_BLOCK, 2 * SB_DH), BF16),
                        pltpu.VMEM((T + SB_BLOCK, SB_DH), BF16),
                        pltpu.VMEM((T + SB_BLOCK, SB_DH), BF16),
                        pltpu.VMEM((n_q, 2 * SB_DH, SB_BLOCK), F32),
                        pltpu.VMEM((n_q, 1, 2 * SB_BLOCK), F32),
                        pltpu.SMEM((n_q // SB_GROUP,), jnp.int32)],
        compiler_params=pltpu.CompilerParams(
            dimension_semantics=("parallel", "parallel"), vmem_limit_bytes=VMEM_LIMIT),
        name="stickbreak",
    )(sq, sk, sv, *consts)


def kernel(x, c, w_ada, b_ada, norm_ffn1, ffn1_w_in, ffn1_w_out, norm_mix, w_in, w_out, gla_w_gate_up, gla_b_gate, gla_out_norm, sb_q_norm, sb_k_norm, sb_out_norm, conv_w, conv_b, conv_ln_g, conv_ln_b, norm_ffn2, ffn2_w_in, ffn2_w_out):
    B, T, D = x.shape
    L = w_ada.shape[0]
    mod = _adaln(c, w_ada, b_ada).reshape(L, B, N_MOD, 1, D)
    x2 = x.reshape(B * T, D)
    ffn_w = (ffn1_w_in[0].astype(BF16), ffn1_w_out[0].astype(BF16))
    for l in range(L):
        sh1, sc1, gt1, sh2, sc2, gt2, sh3, sc3, gt3 = (mod[l, :, i] for i in range(N_MOD))
        x2, *ffn_w = _ffn(x2, norm_ffn1[l], sh1, sc1, gt1, *ffn_w, T, next_weights=(ffn2_w_in, ffn2_w_out, l))
        gq, gk, ga, gv, gg, sq, sk, sv, oc = _inproj(
            x2, norm_mix[l], sh2, sc2, w_in[l], gla_w_gate_up[l], gla_b_gate[l], sb_q_norm[l], sb_k_norm[l],
            conv_w[l], conv_b[l], conv_ln_g[l], conv_ln_b[l], T)
        seq = lambda a: a.reshape(B, T, a.shape[-1])
        flat = lambda a: a.reshape(B * T, a.shape[-1])
        oa = _gla(seq(gq), seq(gk), seq(ga), seq(gv), seq(gg), gla_out_norm[l])
        ob = _sb(seq(sq), seq(sk), seq(sv), sb_out_norm[l])
        mix = (flat(oa), flat(ob), oc, gt2, w_out[l].astype(BF16))
        if l + 1 < L:
            x2, *ffn_w = _ffn(x2, norm_ffn2[l], sh3, sc3, gt3, *ffn_w, T, mix=mix,
                              next_weights=(ffn1_w_in, ffn1_w_out, l + 1))
        else:
            x2 = _ffn(x2, norm_ffn2[l], sh3, sc3, gt3, *ffn_w, T, mix=mix)
    return x2.reshape(B, T, D)
```

```python
import functools

import numpy as np
import jax
import jax.numpy as jnp
from jax import lax
from jax.experimental import pallas as pl
from jax.experimental.pallas import tpu as pltpu

F32 = jnp.float32
BF16 = jnp.bfloat16

EPS = 1e-6
CHUNK = 64
GLA_HEADS, GLA_DK, GLA_DV, GLA_RANK, GLA_TAU = 4, 48, 96, 16, 16.0
SB_HEADS, SB_DH = 6, 64
CONV_CH, CONV_WIDTH = 256, 31
D_FF = 2816
N_MOD = 9
GLA_QK = GLA_HEADS * GLA_DK
GLA_V = GLA_HEADS * GLA_DV
SB_W = SB_HEADS * SB_DH

LANES = 128
SUBLANES = 8
BF16_ROWS = 16
QK_PAD = 256
RANK_PAD = 128
FF_CHUNK = 256
TOKEN_TILE = 512
SB_BLOCK = 128
SB_GROUP = 8
SB_SKIP = 104.0
GLA_TILE = 512
GLA_GROUP = 4
CONV_TILE = 256
ADALN_TILE = 1536
CONV_PAD = 32
VMEM_LIMIT = 52 * 1024 * 1024

_OFF_GQ, _OFF_GK, _OFF_GV, _OFF_GG, _OFF_GR = 0, 256, 512, 896, 1280
_OFF_SQ, _OFF_SK, _OFF_SV, _OFF_CA, _OFF_CG = 1408, 1792, 2176, 2560, 2816
PROJ_W = 3072


def _dot(a, b):
    return jnp.dot(a, b, preferred_element_type=F32)


def _dot_nt(a, b):
    return lax.dot_general(a, b, (((1,), (1,)), ((), ())), preferred_element_type=F32)


def _dot_tn(a, b):
    return lax.dot_general(a, b, (((0,), (0,)), ((), ())), preferred_element_type=F32)


def _hi_lo(x, axis):
    hi = x.astype(BF16)
    lo = (x - hi.astype(F32)).astype(BF16)
    return jnp.concatenate([hi, lo], axis=axis)


def _split_dot(x, m2):
    return _dot(_hi_lo(x, 1), m2)


def _split_dot_left(m2, x):
    return _dot(m2, _hi_lo(x, 0))


def _sigmoid(x):
    return 1.0 / (1.0 + jnp.exp(-x))


def _silu(x):
    return x * _sigmoid(x)


def _log_sigmoid(x):
    return jnp.minimum(x, 0.0) - jnp.log(1.0 + jnp.exp(-jnp.abs(x)))


def _modulated_norm(x, g, scale, shift):
    ms = jnp.mean(x * x, axis=-1, keepdims=True)
    return (x * lax.rsqrt(ms + EPS) * g) * (1.0 + scale) + shift


def _adaln_kernel(c_ref, w_ref, b_ref, o_ref):
    ca = _silu(c_ref[...]).astype(BF16)
    o_ref[0] = _dot(ca, w_ref[0].astype(BF16)) + b_ref[0]


def _adaln(c, w_ada, b_ada):
    L, D, W = w_ada.shape
    B = c.shape[0]
    tn = ADALN_TILE
    assert W % tn == 0, (W, tn)
    return pl.pallas_call(
        _adaln_kernel,
        out_shape=jax.ShapeDtypeStruct((L, B, W), F32),
        grid=(L, W // tn),
        in_specs=[
            pl.BlockSpec((B, D), lambda l, j: (0, 0)),
            pl.BlockSpec((1, D, tn), lambda l, j: (l, 0, j)),
            pl.BlockSpec((1, 1, tn), lambda l, j: (l, 0, j)),
        ],
        out_specs=pl.BlockSpec((1, B, tn), lambda l, j: (l, 0, j)),
        compiler_params=pltpu.CompilerParams(
            dimension_semantics=("parallel", "parallel"), vmem_limit_bytes=VMEM_LIMIT),
        name="adaln",
    )(c, w_ada, b_ada.reshape(L, 1, W))


def _ffn_kernel(has_mix, n_round, *refs):
    refs = list(refs)
    x_ref, g_ref, sh_ref, sc_ref, gt_ref, win_ref, wout_ref = refs[:7]
    del refs[:7]
    if has_mix:
        a_ref, b_ref, c_ref, gt2_ref, wmix_ref = refs[:5]
        del refs[:5]
    slabs = refs[:n_round]
    o_ref = refs[n_round]
    rounded = refs[n_round + 1:2 * n_round + 1]
    acc_ref = refs[2 * n_round + 1]
    for src_ref, dst_ref in zip(slabs, rounded):
        dst_ref[...] = src_ref[...].astype(BF16)
    x = x_ref[...]
    if has_mix:
        abc = jnp.concatenate([a_ref[...], b_ref[...], c_ref[...]], axis=1)
        x = x + gt2_ref[0] * _dot(abc, wmix_ref[...])
    hb = _modulated_norm(x, g_ref[...], sc_ref[0], sh_ref[0]).astype(BF16)
    d_ff = wout_ref.shape[0]
    for j in range(d_ff // FF_CHUNK):
        cols = slice(j * FF_CHUNK, (j + 1) * FF_CHUNK)
        a = _dot(hb, win_ref[:, cols])
        b = _dot(hb, win_ref[:, d_ff + j * FF_CHUNK:d_ff + (j + 1) * FF_CHUNK])
        y = _dot((_silu(a) * b).astype(BF16), wout_ref[cols, :])
        if j == 0:
            acc_ref[...] = y
        else:
            acc_ref[...] += y
    o_ref[...] = x + (0.5 * gt_ref[0]) * acc_ref[...]


def _const_spec(shape):
    nd = len(shape)
    return pl.BlockSpec(shape, lambda *_: (0,) * nd, pipeline_mode=pl.Buffered(1))


def _row_spec(tm, w):
    return pl.BlockSpec((tm, w), lambda i: (i, 0))


def _batch_vec_spec(tiles_per_batch, w):
    return pl.BlockSpec((1, 1, w), lambda i: (i // tiles_per_batch, 0, 0))


def _ffn(x2, g, shift, scale, gate, win, wout, seq, mix=None, to_round=()):
    N, D = x2.shape
    tm = TOKEN_TILE
    tpb = seq // tm
    steps = N // tm
    in_specs = [
        _row_spec(tm, D),
        _const_spec((1, D)),
        _batch_vec_spec(tpb, D), _batch_vec_spec(tpb, D), _batch_vec_spec(tpb, D),
        _const_spec(win.shape), _const_spec(wout.shape),
    ]
    out_shape = [jax.ShapeDtypeStruct((N, D), F32)]
    out_specs = [_row_spec(tm, D)]
    args = [x2, g.reshape(1, D), shift, scale, gate, win, wout]
    if mix is not None:
        a, b, c, gate2, w_mix = mix
        in_specs += [_row_spec(tm, t.shape[1]) for t in (a, b, c)] + [_batch_vec_spec(tpb, D)]
        in_specs += [_const_spec(w_mix.shape)]
        args += [a, b, c, gate2, w_mix]
    for stack, layer in to_round:
        _, rows, cols = stack.shape
        if rows % steps == 0 and (rows // steps) % BF16_ROWS == 0:
            slab, last = rows // steps, steps - 1
        else:
            slab = LANES
            last = rows // slab - 1
            assert rows % slab == 0 and last < steps, (rows, steps)
        in_specs.append(pl.BlockSpec((None, slab, cols),
                                     lambda i, layer=layer, last=last: (layer, jnp.minimum(i, last), 0)))
        out_specs.append(pl.BlockSpec((slab, cols), lambda i, last=last: (jnp.minimum(i, last), 0)))
        out_shape.append(jax.ShapeDtypeStruct((rows, cols), BF16))
        args.append(stack)
    outs = pl.pallas_call(
        functools.partial(_ffn_kernel, mix is not None, len(to_round)),
        out_shape=out_shape,
        grid=(steps,),
        in_specs=in_specs,
        out_specs=out_specs,
        scratch_shapes=[pltpu.VMEM((tm, D), F32)],
        compiler_params=pltpu.CompilerParams(
            dimension_semantics=("arbitrary",), vmem_limit_bytes=VMEM_LIMIT),
        name="ffn",
    )(*args)
    return tuple(outs)


def _head_rms(x, seg2, gain, head_dim):
    ms = _split_dot(x * x, seg2) * (1.0 / head_dim)
    return x * lax.rsqrt(ms + EPS) * gain


def _conv_module(pad_ref, sh_ref, w_ref, b_ref, lg_ref, lb_ref, o_ref):
    tm = pad_ref.shape[0] - CONV_PAD
    S = SUBLANES
    for r in range(1, S):
        sh_ref[r - 1, S:, :] = pad_ref[S - r:tm + CONV_PAD - r, :]
    for i in range(tm // CONV_TILE):
        acc = jnp.zeros((CONV_TILE, CONV_CH), F32)
        for j in range(CONV_WIDTH):
            a, r = divmod(CONV_WIDTH - 1 - j, S)
            start = i * CONV_TILE + CONV_PAD - S * a
            src = pad_ref[start:start + CONV_TILE, :] if r == 0 else sh_ref[r - 1, start:start + CONV_TILE, :]
            acc = acc + w_ref[j:j + 1, :] * src
        acc = acc + b_ref[...]
        mu = jnp.mean(acc, axis=-1, keepdims=True)
        xc = acc - mu
        var = jnp.mean(xc * xc, axis=-1, keepdims=True)
        y = xc * lax.rsqrt(var + EPS) * lg_ref[...] + lb_ref[...]
        o_ref[i * CONV_TILE:(i + 1) * CONV_TILE, :] = _silu(y).astype(o_ref.dtype)


def _inproj_kernel(tiles_per_batch, x_ref, g_ref, sh_ref, sc_ref, w_ref, wup_ref, bg_ref, seg_ref, qn_ref, kn_ref,
                   cw_ref, cb_ref, lg_ref, lb_ref,
                   gq_ref, gk_ref, ga_ref, gv_ref, gg_ref, sq_ref, sk_ref, sv_ref, oc_ref, pad_ref, shift_ref):
    hb = _modulated_norm(x_ref[...], g_ref[...], sc_ref[0], sh_ref[0]).astype(BF16)
    pc = _dot(hb, w_ref[:, _OFF_CA:])
    tm = x_ref.shape[0]
    tail = pad_ref[tm:tm + CONV_PAD, :]
    pad_ref[0:CONV_PAD, :] = jnp.where(pl.program_id(0) % tiles_per_batch == 0, 0.0, tail)
    pad_ref[CONV_PAD:, :] = pc[:, :CONV_CH] * _sigmoid(pc[:, CONV_CH:])
    _conv_module(pad_ref, shift_ref, cw_ref, cb_ref, lg_ref, lb_ref, oc_ref)

    p = _dot(hb, w_ref[:, :_OFF_CA])
    gq_ref[...] = p[:, _OFF_GQ:_OFF_GQ + QK_PAD]
    gk_ref[...] = p[:, _OFF_GK:_OFF_GK + QK_PAD]
    gv_ref[...] = p[:, _OFF_GV:_OFF_GV + GLA_V]
    gg_ref[...] = p[:, _OFF_GG:_OFF_GG + GLA_V]
    r = p[:, _OFF_GR:_OFF_GR + RANK_PAD].astype(BF16)
    ga_ref[...] = _log_sigmoid(_dot(r, wup_ref[...]) + bg_ref[...]) * (1.0 / GLA_TAU)
    seg = seg_ref[...]
    sq = _head_rms(p[:, _OFF_SQ:_OFF_SQ + SB_W], seg, qn_ref[...], SB_DH)
    sq_ref[...] = (sq * (SB_DH ** -0.5)).astype(BF16)
    sk_ref[...] = _head_rms(p[:, _OFF_SK:_OFF_SK + SB_W], seg, kn_ref[...], SB_DH).astype(BF16)
    sv_ref[...] = p[:, _OFF_SV:_OFF_SV + SB_W].astype(BF16)


def _same_head(width, head_dim):
    h = np.arange(width) // head_dim
    m = (h[:, None] == h[None, :]).astype(np.float32)
    return jnp.asarray(np.concatenate([m, m], axis=0), BF16)


def _pad_cols(w, width):
    return jnp.pad(w, ((0, 0), (0, width - w.shape[1])))


def _inproj(x2, g, shift, scale, w_in, w_up, b_gate, q_norm, k_norm, conv_w, conv_b, conv_ln_g, conv_ln_b, seq):
    N, D = x2.shape
    o = 0
    parts = []
    for width, pad in ((GLA_QK, QK_PAD), (GLA_QK, QK_PAD), (GLA_V, GLA_V), (GLA_V, GLA_V),
                       (GLA_RANK, RANK_PAD), (SB_W, SB_W), (SB_W, SB_W), (SB_W, SB_W),
                       (CONV_CH, CONV_CH), (CONV_CH, CONV_CH)):
        parts.append(_pad_cols(w_in[:, o:o + width], pad))
        o += width
    w = jnp.concatenate(parts, axis=1).astype(BF16)
    wup = jnp.pad(w_up, ((0, RANK_PAD - GLA_RANK), (0, QK_PAD - GLA_QK))).astype(BF16)
    bg = jnp.pad(b_gate, (0, QK_PAD - GLA_QK)).reshape(1, QK_PAD)
    tm = TOKEN_TILE
    tpb = seq // tm
    seg = _same_head(SB_W, SB_DH)
    heads = lambda gn: jnp.tile(gn, SB_HEADS).reshape(1, SB_W)
    outs = ((QK_PAD, F32), (QK_PAD, F32), (QK_PAD, F32), (GLA_V, F32), (GLA_V, F32),
            (SB_W, BF16), (SB_W, BF16), (SB_W, BF16), (CONV_CH, BF16))
    chan = lambda a: a.reshape(1, CONV_CH)
    return pl.pallas_call(
        functools.partial(_inproj_kernel, tpb),
        out_shape=[jax.ShapeDtypeStruct((N, wd), dt) for wd, dt in outs],
        grid=(N // tm,),
        in_specs=[
            _row_spec(tm, D),
            _const_spec((1, D)),
            _batch_vec_spec(tpb, D), _batch_vec_spec(tpb, D),
            _const_spec(w.shape), _const_spec(wup.shape), _const_spec(bg.shape),
            _const_spec(seg.shape), _const_spec((1, SB_W)), _const_spec((1, SB_W)),
            _const_spec(conv_w.shape), _const_spec((1, CONV_CH)), _const_spec((1, CONV_CH)), _const_spec((1, CONV_CH)),
        ],
        out_specs=[_row_spec(tm, wd) for wd, _ in outs],
        scratch_shapes=[pltpu.VMEM((tm + CONV_PAD, CONV_CH), F32),
                        pltpu.VMEM((SUBLANES - 1, tm + CONV_PAD, CONV_CH), F32)],
        compiler_params=pltpu.CompilerParams(
            dimension_semantics=("arbitrary",), vmem_limit_bytes=VMEM_LIMIT),
        name="inproj",
    )(x2, g.reshape(1, D), shift, scale, w, wup, bg, seg, heads(q_norm), heads(k_norm),
      conv_w, chan(conv_b), chan(conv_ln_g), chan(conv_ln_b))


_GLA_LEVELS = 6


def _gla_constants():
    C = CHUNK
    t = np.arange(C)[:, None]
    j = np.arange(C)[None, :]
    mats = [j <= t, j > t]
    masks = [t == j]
    for l in range(1, _GLA_LEVELS + 1):
        n, m = 1 << l, 1 << (l - 1)
        ref = t // n * n + m - 1
        right = (t % n) >= m
        mats.append((right & (j > ref) & (j <= t)) | ((~right) & (j > t) & (j <= ref)))
        masks.append(((t // n) == (j // n)) & right & ((j % n) < m))
    prefix = np.concatenate(mats, axis=0).astype(np.float32)
    prefix = np.concatenate([prefix, prefix], axis=1)
    masks = np.stack([np.tile(mk, (1, GLA_HEADS)) for mk in masks]).astype(np.float32)
    kl = np.arange(QK_PAD)[None, :] // GLA_DK
    vl = np.arange(GLA_V)[:, None] // GLA_DV
    state_mask = (kl == vl).astype(np.float32)
    return prefix, masks, state_mask


def _gla_kernel(q_ref, k_ref, a_ref, v_ref, g_ref, pre_ref, msk_ref, smask_ref, seg_ref, on_ref,
                o_ref, st_ref):
    C = CHUNK

    @pl.when(pl.program_id(1) == 0)
    def _():
        st_ref[...] = jnp.zeros_like(st_ref)

    klane = lax.broadcasted_iota(jnp.int32, (1, QK_PAD), 1) // GLA_DK
    vlane = lax.broadcasted_iota(jnp.int32, (1, GLA_V), 1) // GLA_DV
    n_grp = GLA_GROUP

    def group(i, carry):
        base = i * (n_grp * C)
        rows = [pl.ds(pl.multiple_of(base + j * C, C), C) for j in range(n_grp)]
        qs = [q_ref[0, r, :] * (GLA_DK ** -0.5) for r in rows]
        ks = [k_ref[0, r, :] for r in rows]
        vs = [v_ref[0, r, :] for r in rows]

        es = []
        for r in rows:
            es.append(_split_dot_left(pre_ref[...], a_ref[0, r, :]))

        atts = []
        for q, k, e in zip(qs, ks, es):
            att = None
            for l in range(_GLA_LEVELS + 1):
                if l == 0:
                    qt, kt = q, k
                else:
                    f = jnp.exp(e[(l + 1) * C:(l + 2) * C])
                    qt, kt = q * f, k * f
                kst = jnp.concatenate(
                    [jnp.where(klane == h, kt, 0.0) for h in range(GLA_HEADS)], axis=0).astype(BF16)
                term = _dot_nt(qt.astype(BF16), kst) * msk_ref[l]
                att = term if att is None else att + term
            atts.append(att)

        intra, upds, qbs, decs = [], [], [], []
        for q, k, v, e, att in zip(qs, ks, vs, es, atts):
            vst = jnp.concatenate(
                [jnp.where(vlane == h, v, 0.0) for h in range(GLA_HEADS)], axis=0).astype(BF16)
            intra.append(_dot(att.astype(BF16), vst))
            kdec = (k * jnp.exp(e[C:2 * C])).astype(BF16)
            upds.append(_dot_tn(v.astype(BF16), kdec) * smask_ref[...])
            qbs.append((q * jnp.exp(e[0:C])).astype(BF16))
            decs.append(jnp.exp(e[C - 1:C]))

        st = st_ref[...]
        outs = []
        for o_intra, upd, qb, dec in zip(intra, upds, qbs, decs):
            outs.append(o_intra + _dot_nt(qb, st.astype(BF16)))
            st = st * dec + upd
        st_ref[...] = st

        o = jnp.concatenate(outs, axis=0)
        out_rows = pl.ds(pl.multiple_of(base, n_grp * C), n_grp * C)
        y = _head_rms(o, seg_ref[...], on_ref[...], GLA_DV)
        o_ref[0, out_rows, :] = (y * _silu(g_ref[0, out_rows, :])).astype(o_ref.dtype)
        return carry

    lax.fori_loop(0, q_ref.shape[1] // (n_grp * C), group, 0)


def _gla(gq, gk, ga, gv, gg, out_norm):
    B, T, _ = gq.shape
    prefix, masks, state_mask = _gla_constants()
    tt = GLA_TILE
    seq_spec = lambda w: pl.BlockSpec((1, tt, w), lambda b, i: (b, i, 0))
    consts = (jnp.asarray(prefix, BF16), jnp.asarray(masks), jnp.asarray(state_mask),
              _same_head(GLA_V, GLA_DV), jnp.tile(out_norm, GLA_HEADS).reshape(1, GLA_V))
    return pl.pallas_call(
        _gla_kernel,
        out_shape=jax.ShapeDtypeStruct((B, T, GLA_V), BF16),
        grid=(B, T // tt),
        in_specs=[seq_spec(QK_PAD), seq_spec(QK_PAD), seq_spec(QK_PAD), seq_spec(GLA_V), seq_spec(GLA_V)]
                 + [_const_spec(cst.shape) for cst in consts],
        out_specs=seq_spec(GLA_V),
        scratch_shapes=[pltpu.VMEM((GLA_V, QK_PAD), F32)],
        compiler_params=pltpu.CompilerParams(
            dimension_semantics=("parallel", "arbitrary"), vmem_limit_bytes=VMEM_LIMIT),
        name="gla",
    )(gq, gk, ga, gv, gg, *consts)


def _sb_kernel(q_ref, k_ref, v_ref, og_ref, seg_ref, causal_ref, suf_ref, o_ref,
               qs_ref, kn_ref, v0_ref, v1_ref, acc_ref, carry_ref, alive_ref):
    T = q_ref.shape[1]
    BLK, G = SB_BLOCK, SB_GROUP
    n_q = T // BLK
    n_groups = n_q // G
    first = lax.broadcasted_iota(jnp.int32, (1, 2 * SB_DH), 1) < SB_DH

    kn_ref[0:BLK, :] = jnp.zeros((BLK, 2 * SB_DH), BF16)
    kn_ref[BLK:, :] = k_ref[0]
    v0_ref[0:BLK, :] = jnp.zeros((BLK, SB_DH), BF16)
    v1_ref[0:BLK, :] = jnp.zeros((BLK, SB_DH), BF16)
    v0_ref[BLK:, :] = v_ref[0, :, 0:SB_DH]
    v1_ref[BLK:, :] = v_ref[0, :, SB_DH:]

    for qi in range(n_q):
        q2 = q_ref[0, qi * BLK:(qi + 1) * BLK, :]
        zq = jnp.zeros_like(q2)
        qs_ref[qi, 0:BLK, :] = jnp.where(first, q2, zq)
        qs_ref[qi, BLK:, :] = jnp.where(first, zq, q2)

    def group_step(grp, dist, causal):
        qis = [grp * G + g for g in range(G)]
        krows = [pl.ds(pl.multiple_of(jnp.maximum(qi - dist + 1, 0) * BLK, BLK), BLK) for qi in qis]
        zs = [_dot_nt(kn_ref[kr, :], qs_ref[qi]) for qi, kr in zip(qis, krows)]
        log_betas, drops = [], []
        for z in zs:
            sp = jnp.maximum(z, 0.0) + jnp.log(1.0 + jnp.exp(-jnp.abs(z)))
            log_betas.append(z - sp)
            drops.append(sp * causal_ref[...] if causal else sp)
        sufs = [_split_dot_left(suf_ref[...], d) for d in drops]
        ws, alive = [], None
        for qi, lb, d, suf in zip(qis, log_betas, drops, sufs):
            carry = jnp.zeros((1, 2 * BLK), F32) if causal else carry_ref[qi]
            w = jnp.exp(lb - suf + carry)
            ws.append((w * causal_ref[...] if causal else w).astype(BF16))
            carry = carry - (suf[0:1, :] + d[0:1, :])
            carry_ref[qi] = carry
            alive = carry if alive is None else jnp.maximum(alive, carry)
        for qi, kr, w in zip(qis, krows, ws):
            for h, v_ref_h in enumerate((v0_ref, v1_ref)):
                upd = _dot_tn(v_ref_h[kr, :], w[:, h * BLK:(h + 1) * BLK])
                out = (qi, slice(h * SB_DH, (h + 1) * SB_DH), slice(None))
                acc_ref[out] = upd if causal else acc_ref[out] + upd
        return jnp.max(alive) > -SB_SKIP

    def diagonal(grp, c0):
        alive_ref[grp] = group_step(grp, 0, True).astype(jnp.int32)
        return c0

    lax.fori_loop(0, n_groups, diagonal, 0)

    def any_alive():
        total = alive_ref[0]
        for g in range(1, n_groups):
            total = total + alive_ref[g]
        return total > 0

    def sweep(state):
        dist, _ = state

        def visit(grp, c0):
            run = jnp.logical_and(alive_ref[grp] > 0, grp * G + G - 1 >= dist)

            @pl.when(run)
            def _():
                alive_ref[grp] = group_step(grp, dist, False).astype(jnp.int32)

            @pl.when(jnp.logical_not(run))
            def _():
                alive_ref[grp] = 0

            return c0

        lax.fori_loop(0, n_groups, visit, 0)
        return dist + 1, any_alive()

    lax.while_loop(lambda s: jnp.logical_and(s[0] < n_q, s[1]), sweep, (1, any_alive()))

    def finish(grp, c0):
        cols = pl.ds(pl.multiple_of(grp * G * BLK, G * BLK), G * BLK)
        ot = jnp.concatenate([acc_ref[grp * G + g] for g in range(G)], axis=1)
        o_ref[0, cols, :] = _head_rms(ot.T, seg_ref[...], og_ref[...], SB_DH).astype(o_ref.dtype)
        return c0

    lax.fori_loop(0, n_groups, finish, 0)


def _sb_constants():
    BLK = SB_BLOCK
    s = np.arange(BLK)[:, None]
    j = np.arange(BLK)[None, :]
    suffix = (j > s).astype(np.float32)
    suffix = np.concatenate([suffix, suffix], axis=1)
    causal =(s < np.tile(np.arange(BLK), 2)[None, :]).astype(np.float32)
    return causal, suffix


def _sb(sq, sk, sv, out_norm):
    B, T, _ = sq.shape
    causal, suffix = _sb_constants()
    consts = (jnp.tile(out_norm, 2).reshape(1, 2 * SB_DH), _same_head(2 * SB_DH, SB_DH),
              jnp.asarray(causal), jnp.asarray(suffix, BF16))
    spec = pl.BlockSpec((1, T, 2 * SB_DH), lambda b, p: (b, 0, p))
    assert T % (SB_GROUP * SB_BLOCK) == 0, (T, SB_GROUP, SB_BLOCK)
    n_q = T // SB_BLOCK
    return pl.pallas_call(
        _sb_kernel,
        out_shape=jax.ShapeDtypeStruct((B, T, SB_W), BF16),
        grid=(B, SB_HEADS // 2),
        in_specs=[spec, spec, spec] + [_const_spec(cst.shape) for cst in consts],
        out_specs=spec,
        scratch_shapes=[pltpu.VMEM((n_q, 2 * SB_BLOCK, 2 * SB_DH), BF16),
                        pltpu.VMEM((T + SB_BLOCK, 2 * SB_DH), BF16),
                        pltpu.VMEM((T + SB_BLOCK, SB_DH), BF16),
                        pltpu.VMEM((T + SB_BLOCK, SB_DH), BF16),
                        pltpu.VMEM((n_q, 2 * SB_DH, SB_BLOCK), F32),
                        pltpu.VMEM((n_q, 1, 2 * SB_BLOCK), F32),
                        pltpu.SMEM((n_q // SB_GROUP,), jnp.int32)],
        compiler_params=pltpu.CompilerParams(
            dimension_semantics=("parallel", "parallel"), vmem_limit_bytes=VMEM_LIMIT),
        name="stickbreak",
    )(sq, sk, sv, *consts)


def kernel(x, c, w_ada, b_ada, norm_ffn1, ffn1_w_in, ffn1_w_out, norm_mix, w_in, w_out, gla_w_gate_up, gla_b_gate, gla_out_norm, sb_q_norm, sb_k_norm, sb_out_norm, conv_w, conv_b, conv_ln_g, conv_ln_b, norm_ffn2, ffn2_w_in, ffn2_w_out):
    B, T, D = x.shape
    L = w_ada.shape[0]
    mod = _adaln(c, w_ada, b_ada).reshape(L, B, N_MOD, 1, D)
    x2 = x.reshape(B * T, D)
    ffn_w = (ffn1_w_in[0].astype(BF16), ffn1_w_out[0].astype(BF16))
    for l in range(L):
        sh1, sc1, gt1, sh2, sc2, gt2, sh3, sc3, gt3 = (mod[l, :, i] for i in range(N_MOD))
        x2, *ffn_w, w_mix = _ffn(x2, norm_ffn1[l], sh1, sc1, gt1, *ffn_w, T,
                                 to_round=((ffn2_w_in, l), (ffn2_w_out, l), (w_out, l)))
        gq, gk, ga, gv, gg, sq, sk, sv, oc = _inproj(
            x2, norm_mix[l], sh2, sc2, w_in[l], gla_w_gate_up[l], gla_b_gate[l], sb_q_norm[l], sb_k_norm[l],
            conv_w[l], conv_b[l], conv_ln_g[l], conv_ln_b[l], T)
        seq = lambda a: a.reshape(B, T, a.shape[-1])
        flat = lambda a: a.reshape(B * T, a.shape[-1])
        oa = _gla(seq(gq), seq(gk), seq(ga), seq(gv), seq(gg), gla_out_norm[l])
        ob = _sb(seq(sq), seq(sk), seq(sv), sb_out_norm[l])
        mix = (flat(oa), flat(ob), oc, gt2, w_mix)
        nxt = ((ffn1_w_in, l + 1), (ffn1_w_out, l + 1)) if l + 1 < L else ()
        x2, *ffn_w = _ffn(x2, norm_ffn2[l], sh3, sc3, gt3, *ffn_w, T, mix=mix, to_round=nxt)
    return x2.reshape(B, T, D)
```

```python
import functools

import numpy as np
import jax
import jax.numpy as jnp
from jax import lax
from jax.experimental import pallas as pl
from jax.experimental.pallas import tpu as pltpu

F32 = jnp.float32
BF16 = jnp.bfloat16

EPS = 1e-6
CHUNK = 64
GLA_HEADS, GLA_DK, GLA_DV, GLA_RANK, GLA_TAU = 4, 48, 96, 16, 16.0
SB_HEADS, SB_DH = 6, 64
CONV_CH, CONV_WIDTH = 256, 31
D_FF = 2816
N_MOD = 9
GLA_QK = GLA_HEADS * GLA_DK
GLA_V = GLA_HEADS * GLA_DV
SB_W = SB_HEADS * SB_DH

LANES = 128
SUBLANES = 8
BF16_ROWS = 16
QK_PAD = 256
RANK_PAD = 128
FF_CHUNK = 256
TOKEN_TILE = 512
SB_BLOCK = 128
SB_GROUP = 8
SB_SKIP = 104.0
GLA_TILE = 512
GLA_GROUP = 4
CONV_TILE = 256
ADALN_TILE = 1536
CONV_PAD = 32
VMEM_LIMIT = 52 * 1024 * 1024

_OFF_GQ, _OFF_GK, _OFF_GV, _OFF_GG, _OFF_GR = 0, 256, 512, 896, 1280
_OFF_SQ, _OFF_SK, _OFF_SV, _OFF_CA, _OFF_CG = 1408, 1792, 2176, 2560, 2816
PROJ_W = 3072


def _dot(a, b):
    return jnp.dot(a, b, preferred_element_type=F32)


def _dot_nt(a, b):
    return lax.dot_general(a, b, (((1,), (1,)), ((), ())), preferred_element_type=F32)


def _dot_tn(a, b):
    return lax.dot_general(a, b, (((0,), (0,)), ((), ())), preferred_element_type=F32)


def _hi_lo(x, axis):
    hi = x.astype(BF16)
    lo = (x - hi.astype(F32)).astype(BF16)
    return jnp.concatenate([hi, lo], axis=axis)


def _split_dot(x, m2):
    return _dot(_hi_lo(x, 1), m2)


def _split_dot_left(m2, x):
    return _dot(m2, _hi_lo(x, 0))


def _sigmoid(x):
    return 1.0 / (1.0 + jnp.exp(-x))


def _silu(x):
    return x * _sigmoid(x)


def _log_sigmoid(x):
    return jnp.minimum(x, 0.0) - jnp.log(1.0 + jnp.exp(-jnp.abs(x)))


def _modulated_norm(x, g, scale, shift):
    ms = jnp.mean(x * x, axis=-1, keepdims=True)
    return (x * lax.rsqrt(ms + EPS) * g) * (1.0 + scale) + shift


def _adaln_kernel(c_ref, w_ref, b_ref, o_ref):
    ca = _silu(c_ref[...]).astype(BF16)
    o_ref[0] = _dot(ca, w_ref[0].astype(BF16)) + b_ref[0]


def _adaln(c, w_ada, b_ada):
    L, D, W = w_ada.shape
    B = c.shape[0]
    tn = ADALN_TILE
    assert W % tn == 0, (W, tn)
    return pl.pallas_call(
        _adaln_kernel,
        out_shape=jax.ShapeDtypeStruct((L, B, W), F32),
        grid=(L, W // tn),
        in_specs=[
            pl.BlockSpec((B, D), lambda l, j: (0, 0)),
            pl.BlockSpec((1, D, tn), lambda l, j: (l, 0, j)),
            pl.BlockSpec((1, 1, tn), lambda l, j: (l, 0, j)),
        ],
        out_specs=pl.BlockSpec((1, B, tn), lambda l, j: (l, 0, j)),
        compiler_params=pltpu.CompilerParams(
            dimension_semantics=("parallel", "parallel"), vmem_limit_bytes=VMEM_LIMIT),
        name="adaln",
    )(c, w_ada, b_ada.reshape(L, 1, W))


def _ffn_kernel(has_mix, n_round, *refs):
    refs = list(refs)
    x_ref, g_ref, sh_ref, sc_ref, gt_ref, win_ref, wout_ref = refs[:7]
    del refs[:7]
    if has_mix:
        a_ref, b_ref, c_ref, gt2_ref, wmix_ref = refs[:5]
        del refs[:5]
    slabs = refs[:n_round]
    o_ref = refs[n_round]
    rounded = refs[n_round + 1:2 * n_round + 1]
    acc_ref = refs[2 * n_round + 1]
    for src_ref, dst_ref in zip(slabs, rounded):
        dst_ref[...] = src_ref[...].astype(BF16)
    x = x_ref[...]
    if has_mix:
        abc = jnp.concatenate([a_ref[...], b_ref[...], c_ref[...]], axis=1)
        x = x + gt2_ref[0] * _dot(abc, wmix_ref[...])
    hb = _modulated_norm(x, g_ref[...], sc_ref[0], sh_ref[0]).astype(BF16)
    d_ff = wout_ref.shape[0]
    for j in range(d_ff // FF_CHUNK):
        cols = slice(j * FF_CHUNK, (j + 1) * FF_CHUNK)
        a = _dot(hb, win_ref[:, cols])
        b = _dot(hb, win_ref[:, d_ff + j * FF_CHUNK:d_ff + (j + 1) * FF_CHUNK])
        y = _dot((_silu(a) * b).astype(BF16), wout_ref[cols, :])
        if j == 0:
            acc_ref[...] = y
        else:
            acc_ref[...] += y
    o_ref[...] = x + (0.5 * gt_ref[0]) * acc_ref[...]


def _const_spec(shape):
    nd = len(shape)
    return pl.BlockSpec(shape, lambda *_: (0,) * nd, pipeline_mode=pl.Buffered(1))


def _row_spec(tm, w):
    return pl.BlockSpec((tm, w), lambda i: (i, 0))


def _batch_vec_spec(tiles_per_batch, w):
    return pl.BlockSpec((1, 1, w), lambda i: (i // tiles_per_batch, 0, 0))


def _ffn(x2, g, shift, scale, gate, win, wout, seq, mix=None, to_round=()):
    N, D = x2.shape
    tm = TOKEN_TILE
    tpb = seq // tm
    steps = N // tm
    in_specs = [
        _row_spec(tm, D),
        _const_spec((1, D)),
        _batch_vec_spec(tpb, D), _batch_vec_spec(tpb, D), _batch_vec_spec(tpb, D),
        _const_spec(win.shape), _const_spec(wout.shape),
    ]
    out_shape = [jax.ShapeDtypeStruct((N, D), F32)]
    out_specs = [_row_spec(tm, D)]
    args = [x2, g.reshape(1, D), shift, scale, gate, win, wout]
    if mix is not None:
        a, b, c, gate2, w_mix = mix
        in_specs += [_row_spec(tm, t.shape[1]) for t in (a, b, c)] + [_batch_vec_spec(tpb, D)]
        in_specs += [_const_spec(w_mix.shape)]
        args += [a, b, c, gate2, w_mix]
    for stack, layer in to_round:
        _, rows, cols = stack.shape
        if rows % steps == 0 and (rows // steps) % BF16_ROWS == 0:
            slab, last = rows // steps, steps - 1
        else:
            slab = LANES
            last = rows // slab - 1
            assert rows % slab == 0 and last < steps, (rows, steps)
        in_specs.append(pl.BlockSpec((None, slab, cols),
                                     lambda i, layer=layer, last=last: (layer, jnp.minimum(i, last), 0)))
        out_specs.append(pl.BlockSpec((slab, cols), lambda i, last=last: (jnp.minimum(i, last), 0)))
        out_shape.append(jax.ShapeDtypeStruct((rows, cols), BF16))
        args.append(stack)
    outs = pl.pallas_call(
        functools.partial(_ffn_kernel, mix is not None, len(to_round)),
        out_shape=out_shape,
        grid=(steps,),
        in_specs=in_specs,
        out_specs=out_specs,
        scratch_shapes=[pltpu.VMEM((tm, D), F32)],
        compiler_params=pltpu.CompilerParams(
            dimension_semantics=("arbitrary",), vmem_limit_bytes=VMEM_LIMIT),
        name="ffn",
    )(*args)
    return tuple(outs)


def _head_rms(x, seg2, gain, head_dim):
    ms = _split_dot(x * x, seg2) * (1.0 / head_dim)
    return x * lax.rsqrt(ms + EPS) * gain


def _conv_module(pad_ref, sh_ref, w_ref, b_ref, lg_ref, lb_ref, o_ref):
    tm = pad_ref.shape[0] - CONV_PAD
    S = SUBLANES
    for r in range(1, S):
        sh_ref[r - 1, S:, :] = pad_ref[S - r:tm + CONV_PAD - r, :]
    for i in range(tm // CONV_TILE):
        acc = jnp.zeros((CONV_TILE, CONV_CH), F32)
        for j in range(CONV_WIDTH):
            a, r = divmod(CONV_WIDTH - 1 - j, S)
            start = i * CONV_TILE + CONV_PAD - S * a
            src = pad_ref[start:start + CONV_TILE, :] if r == 0 else sh_ref[r - 1, start:start + CONV_TILE, :]
            acc = acc + w_ref[j:j + 1, :] * src
        acc = acc + b_ref[...]
        mu = jnp.mean(acc, axis=-1, keepdims=True)
        xc = acc - mu
        var = jnp.mean(xc * xc, axis=-1, keepdims=True)
        y = xc * lax.rsqrt(var + EPS) * lg_ref[...] + lb_ref[...]
        o_ref[i * CONV_TILE:(i + 1) * CONV_TILE, :] = _silu(y).astype(o_ref.dtype)


def _inproj_kernel(tiles_per_batch, x_ref, g_ref, sh_ref, sc_ref, w_ref, wup_ref, bg_ref, seg_ref, qn_ref, kn_ref,
                   cw_ref, cb_ref, lg_ref, lb_ref,
                   gq_ref, gk_ref, ga_ref, gv_ref, gg_ref, sq_ref, sk_ref, sv_ref, oc_ref, pad_ref, shift_ref):
    hb = _modulated_norm(x_ref[...], g_ref[...], sc_ref[0], sh_ref[0]).astype(BF16)
    pc = _dot(hb, w_ref[:, _OFF_CA:])
    tm = x_ref.shape[0]
    tail = pad_ref[tm:tm + CONV_PAD, :]
    pad_ref[0:CONV_PAD, :] = jnp.where(pl.program_id(0) % tiles_per_batch == 0, 0.0, tail)
    pad_ref[CONV_PAD:, :] = pc[:, :CONV_CH] * _sigmoid(pc[:, CONV_CH:])
    _conv_module(pad_ref, shift_ref, cw_ref, cb_ref, lg_ref, lb_ref, oc_ref)

    p = _dot(hb, w_ref[:, :_OFF_CA])
    gq_ref[...] = p[:, _OFF_GQ:_OFF_GQ + QK_PAD]
    gk_ref[...] = p[:, _OFF_GK:_OFF_GK + QK_PAD]
    gv_ref[...] = p[:, _OFF_GV:_OFF_GV + GLA_V]
    gg_ref[...] = p[:, _OFF_GG:_OFF_GG + GLA_V]
    r = p[:, _OFF_GR:_OFF_GR + RANK_PAD].astype(BF16)
    ga_ref[...] = _log_sigmoid(_dot(r, wup_ref[...]) + bg_ref[...]) * (1.0 / GLA_TAU)
    seg = seg_ref[...]
    for t in range(SB_W // LANES):
        lanes = slice(t * LANES, (t + 1) * LANES)
        sq = _head_rms(p[:, _OFF_SQ + t * LANES:_OFF_SQ + (t + 1) * LANES], seg, qn_ref[:, lanes], SB_DH)
        sq_ref[:, lanes] = (sq * (SB_DH ** -0.5)).astype(BF16)
        sk = _head_rms(p[:, _OFF_SK + t * LANES:_OFF_SK + (t + 1) * LANES], seg, kn_ref[:, lanes], SB_DH)
        sk_ref[:, lanes] = sk.astype(BF16)
    sv_ref[...] = p[:, _OFF_SV:_OFF_SV + SB_W].astype(BF16)


def _same_head(width, head_dim):
    h = np.arange(width) // head_dim
    m = (h[:, None] == h[None, :]).astype(np.float32)
    return jnp.asarray(np.concatenate([m, m], axis=0), BF16)


def _pad_cols(w, width):
    return jnp.pad(w, ((0, 0), (0, width - w.shape[1])))


def _inproj(x2, g, shift, scale, w_in, w_up, b_gate, q_norm, k_norm, conv_w, conv_b, conv_ln_g, conv_ln_b, seq):
    N, D = x2.shape
    o = 0
    parts = []
    for width, pad in ((GLA_QK, QK_PAD), (GLA_QK, QK_PAD), (GLA_V, GLA_V), (GLA_V, GLA_V),
                       (GLA_RANK, RANK_PAD), (SB_W, SB_W), (SB_W, SB_W), (SB_W, SB_W),
                       (CONV_CH, CONV_CH), (CONV_CH, CONV_CH)):
        parts.append(_pad_cols(w_in[:, o:o + width], pad))
        o += width
    w = jnp.concatenate(parts, axis=1).astype(BF16)
    wup = jnp.pad(w_up, ((0, RANK_PAD - GLA_RANK), (0, QK_PAD - GLA_QK))).astype(BF16)
    bg = jnp.pad(b_gate, (0, QK_PAD - GLA_QK)).reshape(1, QK_PAD)
    tm = TOKEN_TILE
    tpb = seq // tm
    seg = _same_head(LANES, SB_DH)
    heads = lambda gn: jnp.tile(gn, SB_HEADS).reshape(1, SB_W)
    outs = ((QK_PAD, F32), (QK_PAD, F32), (QK_PAD, F32), (GLA_V, F32), (GLA_V, F32),
            (SB_W, BF16), (SB_W, BF16), (SB_W, BF16), (CONV_CH, BF16))
    chan = lambda a: a.reshape(1, CONV_CH)
    return pl.pallas_call(
        functools.partial(_inproj_kernel, tpb),
        out_shape=[jax.ShapeDtypeStruct((N, wd), dt) for wd, dt in outs],
        grid=(N // tm,),
        in_specs=[
            _row_spec(tm, D),
            _const_spec((1, D)),
            _batch_vec_spec(tpb, D), _batch_vec_spec(tpb, D),
            _const_spec(w.shape), _const_spec(wup.shape), _const_spec(bg.shape),
            _const_spec(seg.shape), _const_spec((1, SB_W)), _const_spec((1, SB_W)),
            _const_spec(conv_w.shape), _const_spec((1, CONV_CH)), _const_spec((1, CONV_CH)), _const_spec((1, CONV_CH)),
        ],
        out_specs=[_row_spec(tm, wd) for wd, _ in outs],
        scratch_shapes=[pltpu.VMEM((tm + CONV_PAD, CONV_CH), F32),
                        pltpu.VMEM((SUBLANES - 1, tm + CONV_PAD, CONV_CH), F32)],
        compiler_params=pltpu.CompilerParams(
            dimension_semantics=("arbitrary",), vmem_limit_bytes=VMEM_LIMIT),
        name="inproj",
    )(x2, g.reshape(1, D), shift, scale, w, wup, bg, seg, heads(q_norm), heads(k_norm),
      conv_w, chan(conv_b), chan(conv_ln_g), chan(conv_ln_b))


_GLA_LEVELS = 6


def _gla_constants():
    C = CHUNK
    t = np.arange(C)[:, None]
    j = np.arange(C)[None, :]
    mats = [j <= t, j > t]
    masks = [t == j]
    for l in range(1, _GLA_LEVELS + 1):
        n, m = 1 << l, 1 << (l - 1)
        ref = t // n * n + m - 1
        right = (t % n) >= m
        mats.append((right & (j > ref) & (j <= t)) | ((~right) & (j > t) & (j <= ref)))
        masks.append(((t // n) == (j // n)) & right & ((j % n) < m))
    prefix = np.concatenate(mats, axis=0).astype(np.float32)
    prefix = np.concatenate([prefix, prefix], axis=1)
    masks = np.stack([np.tile(mk, (1, GLA_HEADS)) for mk in masks]).astype(np.float32)
    kl = np.arange(QK_PAD)[None, :] // GLA_DK
    vl = np.arange(GLA_V)[:, None] // GLA_DV
    state_mask = (kl == vl).astype(np.float32)
    return prefix, masks, state_mask


def _gla_kernel(q_ref, k_ref, a_ref, v_ref, g_ref, pre_ref, msk_ref, smask_ref, seg_ref, on_ref,
                o_ref, st_ref):
    C = CHUNK

    @pl.when(pl.program_id(1) == 0)
    def _():
        st_ref[...] = jnp.zeros_like(st_ref)

    klane = lax.broadcasted_iota(jnp.int32, (1, QK_PAD), 1) // GLA_DK
    vlane = lax.broadcasted_iota(jnp.int32, (1, GLA_V), 1) // GLA_DV
    n_grp = GLA_GROUP

    def group(i, carry):
        base = i * (n_grp * C)
        rows = [pl.ds(pl.multiple_of(base + j * C, C), C) for j in range(n_grp)]
        qs = [q_ref[0, r, :] * (GLA_DK ** -0.5) for r in rows]
        ks = [k_ref[0, r, :] for r in rows]
        vs = [v_ref[0, r, :] for r in rows]

        es = []
        for r in rows:
            es.append(_split_dot_left(pre_ref[...], a_ref[0, r, :]))

        atts = []
        for q, k, e in zip(qs, ks, es):
            att = None
            for l in range(_GLA_LEVELS + 1):
                if l == 0:
                    qt, kt = q, k
                else:
                    f = jnp.exp(e[(l + 1) * C:(l + 2) * C])
                    qt, kt = q * f, k * f
                kst = jnp.concatenate(
                    [jnp.where(klane == h, kt, 0.0) for h in range(GLA_HEADS)], axis=0).astype(BF16)
                term = _dot_nt(qt.astype(BF16), kst) * msk_ref[l]
                att = term if att is None else att + term
            atts.append(att)

        intra, upds, qbs, decs = [], [], [], []
        for q, k, v, e, att in zip(qs, ks, vs, es, atts):
            vst = jnp.concatenate(
                [jnp.where(vlane == h, v, 0.0) for h in range(GLA_HEADS)], axis=0).astype(BF16)
            intra.append(_dot(att.astype(BF16), vst))
            kdec = (k * jnp.exp(e[C:2 * C])).astype(BF16)
            upds.append(_dot_tn(v.astype(BF16), kdec) * smask_ref[...])
            qbs.append((q * jnp.exp(e[0:C])).astype(BF16))
            decs.append(jnp.exp(e[C - 1:C]))

        st = st_ref[...]
        outs = []
        for o_intra, upd, qb, dec in zip(intra, upds, qbs, decs):
            outs.append(o_intra + _dot_nt(qb, st.astype(BF16)))
            st = st * dec + upd
        st_ref[...] = st

        o = jnp.concatenate(outs, axis=0)
        out_rows = pl.ds(pl.multiple_of(base, n_grp * C), n_grp * C)
        y = _head_rms(o, seg_ref[...], on_ref[...], GLA_DV)
        o_ref[0, out_rows, :] = (y * _silu(g_ref[0, out_rows, :])).astype(o_ref.dtype)
        return carry

    lax.fori_loop(0, q_ref.shape[1] // (n_grp * C), group, 0)


def _gla(gq, gk, ga, gv, gg, out_norm):
    B, T, _ = gq.shape
    prefix, masks, state_mask = _gla_constants()
    tt = GLA_TILE
    seq_spec = lambda w: pl.BlockSpec((1, tt, w), lambda b, i: (b, i, 0))
    consts = (jnp.asarray(prefix, BF16), jnp.asarray(masks), jnp.asarray(state_mask),
              _same_head(GLA_V, GLA_DV), jnp.tile(out_norm, GLA_HEADS).reshape(1, GLA_V))
    return pl.pallas_call(
        _gla_kernel,
        out_shape=jax.ShapeDtypeStruct((B, T, GLA_V), BF16),
        grid=(B, T // tt),
        in_specs=[seq_spec(QK_PAD), seq_spec(QK_PAD), seq_spec(QK_PAD), seq_spec(GLA_V), seq_spec(GLA_V)]
                 + [_const_spec(cst.shape) for cst in consts],
        out_specs=seq_spec(GLA_V),
        scratch_shapes=[pltpu.VMEM((GLA_V, QK_PAD), F32)],
        compiler_params=pltpu.CompilerParams(
            dimension_semantics=("parallel", "arbitrary"), vmem_limit_bytes=VMEM_LIMIT),
        name="gla",
    )(gq, gk, ga, gv, gg, *consts)


def _sb_kernel(q_ref, k_ref, v_ref, og_ref, seg_ref, causal_ref, suf_ref, o_ref,
               qs_ref, kn_ref, v0_ref, v1_ref, acc_ref, carry_ref, alive_ref):
    T = q_ref.shape[1]
    BLK, G = SB_BLOCK, SB_GROUP
    n_q = T // BLK
    n_groups = n_q // G
    first = lax.broadcasted_iota(jnp.int32, (1, 2 * SB_DH), 1) < SB_DH

    kn_ref[0:BLK, :] = jnp.zeros((BLK, 2 * SB_DH), BF16)
    kn_ref[BLK:, :] = k_ref[0]
    v0_ref[0:BLK, :] = jnp.zeros((BLK, SB_DH), BF16)
    v1_ref[0:BLK, :] = jnp.zeros((BLK, SB_DH), BF16)
    v0_ref[BLK:, :] = v_ref[0, :, 0:SB_DH]
    v1_ref[BLK:, :] = v_ref[0, :, SB_DH:]

    for qi in range(n_q):
        q2 = q_ref[0, qi * BLK:(qi + 1) * BLK, :]
        zq = jnp.zeros_like(q2)
        qs_ref[qi, 0:BLK, :] = jnp.where(first, q2, zq)
        qs_ref[qi, BLK:, :] = jnp.where(first, zq, q2)

    def group_step(grp, dist, causal):
        qis = [grp * G + g for g in range(G)]
        krows = [pl.ds(pl.multiple_of(jnp.maximum(qi - dist + 1, 0) * BLK, BLK), BLK) for qi in qis]
        zs = [_dot_nt(kn_ref[kr, :], qs_ref[qi]) for qi, kr in zip(qis, krows)]
        log_betas, drops = [], []
        for z in zs:
            sp = jnp.maximum(z, 0.0) + jnp.log(1.0 + jnp.exp(-jnp.abs(z)))
            log_betas.append(z - sp)
            drops.append(sp * causal_ref[...] if causal else sp)
        sufs = [_split_dot_left(suf_ref[...], d) for d in drops]
        ws, alive = [], None
        for qi, lb, d, suf in zip(qis, log_betas, drops, sufs):
            carry = jnp.zeros((1, 2 * BLK), F32) if causal else carry_ref[qi]
            w = jnp.exp(lb - suf + carry)
            ws.append((w * causal_ref[...] if causal else w).astype(BF16))
            carry = carry - (suf[0:1, :] + d[0:1, :])
            carry_ref[qi] = carry
            alive = carry if alive is None else jnp.maximum(alive, carry)
        for qi, kr, w in zip(qis, krows, ws):
            for h, v_ref_h in enumerate((v0_ref, v1_ref)):
                upd = _dot_tn(v_ref_h[kr, :], w[:, h * BLK:(h + 1) * BLK])
                out = (qi, slice(h * SB_DH, (h + 1) * SB_DH), slice(None))
                acc_ref[out] = upd if causal else acc_ref[out] + upd
        return jnp.max(alive) > -SB_SKIP

    def diagonal(grp, c0):
        alive_ref[grp] = group_step(grp, 0, True).astype(jnp.int32)
        return c0

    lax.fori_loop(0, n_groups, diagonal, 0)

    def any_alive():
        total = alive_ref[0]
        for g in range(1, n_groups):
            total = total + alive_ref[g]
        return total > 0

    def sweep(state):
        dist, _ = state

        def visit(grp, c0):
            run = jnp.logical_and(alive_ref[grp] > 0, grp * G + G - 1 >= dist)

            @pl.when(run)
            def _():
                alive_ref[grp] = group_step(grp, dist, False).astype(jnp.int32)

            @pl.when(jnp.logical_not(run))
            def _():
                alive_ref[grp] = 0

            return c0

        lax.fori_loop(0, n_groups, visit, 0)
        return dist + 1, any_alive()

    lax.while_loop(lambda s: jnp.logical_and(s[0] < n_q, s[1]), sweep, (1, any_alive()))

    def finish(grp, c0):
        cols = pl.ds(pl.multiple_of(grp * G * BLK, G * BLK), G * BLK)
        ot = jnp.concatenate([acc_ref[grp * G + g] for g in range(G)], axis=1)
        o_ref[0, cols, :] = _head_rms(ot.T, seg_ref[...], og_ref[...], SB_DH).astype(o_ref.dtype)
        return c0

    lax.fori_loop(0, n_groups, finish, 0)


def _sb_constants():
    BLK = SB_BLOCK
    s = np.arange(BLK)[:, None]
    j = np.arange(BLK)[None, :]
    suffix = (j > s).astype(np.float32)
    suffix = np.concatenate([suffix, suffix], axis=1)
    causal =(s < np.tile(np.arange(BLK), 2)[None, :]).astype(np.float32)
    return causal, suffix


def _sb(sq, sk, sv, out_norm):
    B, T, _ = sq.shape
    causal, suffix = _sb_constants()
    consts = (jnp.tile(out_norm, 2).reshape(1, 2 * SB_DH), _same_head(2 * SB_DH, SB_DH),
              jnp.asarray(causal), jnp.asarray(suffix, BF16))
    spec = pl.BlockSpec((1, T, 2 * SB_DH), lambda b, p: (b, 0, p))
    assert T % (SB_GROUP * SB_BLOCK) == 0, (T, SB_GROUP, SB_BLOCK)
    n_q = T // SB_BLOCK
    return pl.pallas_call(
        _sb_kernel,
        out_shape=jax.ShapeDtypeStruct((B, T, SB_W), BF16),
        grid=(B, SB_HEADS // 2),
        in_specs=[spec, spec, spec] + [_const_spec(cst.shape) for cst in consts],
        out_specs=spec,
        scratch_shapes=[pltpu.VMEM((n_q, 2 * SB_BLOCK, 2 * SB_DH), BF16),
                        pltpu.VMEM((T + SB_BLOCK, 2 * SB_DH), BF16),
                        pltpu.VMEM((T + SB_BLOCK, SB_DH), BF16),
                        pltpu.VMEM((T + SB_BLOCK, SB_DH), BF16),
                        pltpu.VMEM((n_q, 2 * SB_DH, SB_BLOCK), F32),
                        pltpu.VMEM((n_q, 1, 2 * SB_BLOCK), F32),
                        pltpu.SMEM((n_q // SB_GROUP,), jnp.int32)],
        compiler_params=pltpu.CompilerParams(
            dimension_semantics=("parallel", "parallel"), vmem_limit_bytes=VMEM_LIMIT),
        name="stickbreak",
    )(sq, sk, sv, *consts)


def kernel(x, c, w_ada, b_ada, norm_ffn1, ffn1_w_in, ffn1_w_out, norm_mix, w_in, w_out, gla_w_gate_up, gla_b_gate, gla_out_norm, sb_q_norm, sb_k_norm, sb_out_norm, conv_w, conv_b, conv_ln_g, conv_ln_b, norm_ffn2, ffn2_w_in, ffn2_w_out):
    B, T, D = x.shape
    L = w_ada.shape[0]
    mod = _adaln(c, w_ada, b_ada).reshape(L, B, N_MOD, 1, D)
    x2 = x.reshape(B * T, D)
    ffn_w = (ffn1_w_in[0].astype(BF16), ffn1_w_out[0].astype(BF16))
    for l in range(L):
        sh1, sc1, gt1, sh2, sc2, gt2, sh3, sc3, gt3 = (mod[l, :, i] for i in range(N_MOD))
        x2, *ffn_w, w_mix = _ffn(x2, norm_ffn1[l], sh1, sc1, gt1, *ffn_w, T,
                                 to_round=((ffn2_w_in, l), (ffn2_w_out, l), (w_out, l)))
        gq, gk, ga, gv, gg, sq, sk, sv, oc = _inproj(
            x2, norm_mix[l], sh2, sc2, w_in[l], gla_w_gate_up[l], gla_b_gate[l], sb_q_norm[l], sb_k_norm[l],
            conv_w[l], conv_b[l], conv_ln_g[l], conv_ln_b[l], T)
        seq = lambda a: a.reshape(B, T, a.shape[-1])
        flat = lambda a: a.reshape(B * T, a.shape[-1])
        oa = _gla(seq(gq), seq(gk), seq(ga), seq(gv), seq(gg), gla_out_norm[l])
        ob = _sb(seq(sq), seq(sk), seq(sv), sb_out_norm[l])
        mix = (flat(oa), flat(ob), oc, gt2, w_mix)
        nxt = ((ffn1_w_in, l + 1), (ffn1_w_out, l + 1)) if l + 1 < L else ()
        x2, *ffn_w = _ffn(x2, norm_ffn2[l], sh3, sc3, gt3, *ffn_w, T, mix=mix, to_round=nxt)
    return x2.reshape(B, T, D)
```

```python
import functools

import numpy as np
import jax
import jax.numpy as jnp
from jax import lax
from jax.experimental import pallas as pl
from jax.experimental.pallas import tpu as pltpu

F32 = jnp.float32
BF16 = jnp.bfloat16

EPS = 1e-6
CHUNK = 64
GLA_HEADS, GLA_DK, GLA_DV, GLA_RANK, GLA_TAU = 4, 48, 96, 16, 16.0
SB_HEADS, SB_DH = 6, 64
CONV_CH, CONV_WIDTH = 256, 31
D_FF = 2816
N_MOD = 9
GLA_QK = GLA_HEADS * GLA_DK
GLA_V = GLA_HEADS * GLA_DV
SB_W = SB_HEADS * SB_DH

LANES = 128
SUBLANES = 8
BF16_ROWS = 16
QK_PAD = 256
RANK_PAD = 128
FF_CHUNK = 256
TOKEN_TILE = 512
SB_BLOCK = 128
SB_GROUP = 8
SB_SKIP = 104.0
GLA_TILE = 512
GLA_GROUP = 4
CONV_TILE = 256
ADALN_TILE = 1536
CONV_PAD = 32
VMEM_LIMIT = 52 * 1024 * 1024

_OFF_GQ, _OFF_GK, _OFF_GV, _OFF_GG, _OFF_GR = 0, 256, 512, 896, 1280
_OFF_SQ, _OFF_SK, _OFF_SV, _OFF_CA, _OFF_CG = 1408, 1792, 2176, 2560, 2816
PROJ_W = 3072
_PROJ_GROUPS = []
_src = 0
for _width, _dst in ((GLA_QK, _OFF_GQ), (GLA_QK, _OFF_GK), (GLA_V, _OFF_GV), (GLA_V, _OFF_GG), (GLA_RANK, _OFF_GR),
                     (SB_W, _OFF_SQ), (SB_W, _OFF_SK), (SB_W, _OFF_SV), (CONV_CH, _OFF_CA), (CONV_CH, _OFF_CG)):
    _PROJ_GROUPS.append((_src, _width, _dst))
    _src += _width
_PROJ_GROUPS = tuple(_PROJ_GROUPS)


def _dot(a, b):
    return jnp.dot(a, b, preferred_element_type=F32)


def _dot_nt(a, b):
    return lax.dot_general(a, b, (((1,), (1,)), ((), ())), preferred_element_type=F32)


def _dot_tn(a, b):
    return lax.dot_general(a, b, (((0,), (0,)), ((), ())), preferred_element_type=F32)


def _hi_lo(x, axis):
    hi = x.astype(BF16)
    lo = (x - hi.astype(F32)).astype(BF16)
    return jnp.concatenate([hi, lo], axis=axis)


def _split_dot(x, m2):
    return _dot(_hi_lo(x, 1), m2)


def _split_dot_left(m2, x):
    return _dot(m2, _hi_lo(x, 0))


def _sigmoid(x):
    return 1.0 / (1.0 + jnp.exp(-x))


def _silu(x):
    return x * _sigmoid(x)


def _log_sigmoid(x):
    return jnp.minimum(x, 0.0) - jnp.log(1.0 + jnp.exp(-jnp.abs(x)))


def _modulated_norm(x, g, scale, shift):
    ms = jnp.mean(x * x, axis=-1, keepdims=True)
    return (x * lax.rsqrt(ms + EPS) * g) * (1.0 + scale) + shift


def _adaln_kernel(c_ref, w_ref, b_ref, o_ref):
    ca = _silu(c_ref[...]).astype(BF16)
    o_ref[0] = _dot(ca, w_ref[0].astype(BF16)) + b_ref[0]


def _adaln(c, w_ada, b_ada):
    L, D, W = w_ada.shape
    B = c.shape[0]
    tn = ADALN_TILE
    assert W % tn == 0, (W, tn)
    return pl.pallas_call(
        _adaln_kernel,
        out_shape=jax.ShapeDtypeStruct((L, B, W), F32),
        grid=(L, W // tn),
        in_specs=[
            pl.BlockSpec((B, D), lambda l, j: (0, 0)),
            pl.BlockSpec((1, D, tn), lambda l, j: (l, 0, j)),
            pl.BlockSpec((1, 1, tn), lambda l, j: (l, 0, j)),
        ],
        out_specs=pl.BlockSpec((1, B, tn), lambda l, j: (l, 0, j)),
        compiler_params=pltpu.CompilerParams(
            dimension_semantics=("parallel", "parallel"), vmem_limit_bytes=VMEM_LIMIT),
        name="adaln",
    )(c, w_ada, b_ada.reshape(L, 1, W))


def _ffn_kernel(has_mix, regroups, *refs):
    n_round = len(regroups)
    refs = list(refs)
    x_ref, g_ref, sh_ref, sc_ref, gt_ref, win_ref, wout_ref = refs[:7]
    del refs[:7]
    if has_mix:
        a_ref, b_ref, c_ref, gt2_ref, wmix_ref = refs[:5]
        del refs[:5]
    slabs = refs[:n_round]
    o_ref = refs[n_round]
    rounded = refs[n_round + 1:2 * n_round + 1]
    acc_ref = refs[2 * n_round + 1]
    for src_ref, dst_ref, groups in zip(slabs, rounded, regroups):
        if groups is None:
            dst_ref[...] = src_ref[...].astype(BF16)
        else:
            dst_ref[...] = jnp.zeros_like(dst_ref)
            for src_col, width, dst_col in groups:
                dst_ref[:, dst_col:dst_col + width] = src_ref[:, src_col:src_col + width].astype(BF16)
    x = x_ref[...]
    if has_mix:
        abc = jnp.concatenate([a_ref[...], b_ref[...], c_ref[...]], axis=1)
        x = x + gt2_ref[0] * _dot(abc, wmix_ref[...])
    hb = _modulated_norm(x, g_ref[...], sc_ref[0], sh_ref[0]).astype(BF16)
    d_ff = wout_ref.shape[0]
    for j in range(d_ff // FF_CHUNK):
        cols = slice(j * FF_CHUNK, (j + 1) * FF_CHUNK)
        a = _dot(hb, win_ref[:, cols])
        b = _dot(hb, win_ref[:, d_ff + j * FF_CHUNK:d_ff + (j + 1) * FF_CHUNK])
        y = _dot((_silu(a) * b).astype(BF16), wout_ref[cols, :])
        if j == 0:
            acc_ref[...] = y
        else:
            acc_ref[...] += y
    o_ref[...] = x + (0.5 * gt_ref[0]) * acc_ref[...]


def _const_spec(shape):
    nd = len(shape)
    return pl.BlockSpec(shape, lambda *_: (0,) * nd, pipeline_mode=pl.Buffered(1))


def _row_spec(tm, w):
    return pl.BlockSpec((tm, w), lambda i: (i, 0))


def _batch_vec_spec(tiles_per_batch, w):
    return pl.BlockSpec((1, 1, w), lambda i: (i // tiles_per_batch, 0, 0))


def _ffn(x2, g, shift, scale, gate, win, wout, seq, mix=None, to_round=()):
    N, D = x2.shape
    tm = TOKEN_TILE
    tpb = seq // tm
    steps = N // tm
    in_specs = [
        _row_spec(tm, D),
        _const_spec((1, D)),
        _batch_vec_spec(tpb, D), _batch_vec_spec(tpb, D), _batch_vec_spec(tpb, D),
        _const_spec(win.shape), _const_spec(wout.shape),
    ]
    out_shape = [jax.ShapeDtypeStruct((N, D), F32)]
    out_specs = [_row_spec(tm, D)]
    args = [x2, g.reshape(1, D), shift, scale, gate, win, wout]
    if mix is not None:
        a, b, c, gate2, w_mix = mix
        in_specs += [_row_spec(tm, t.shape[1]) for t in (a, b, c)] + [_batch_vec_spec(tpb, D)]
        in_specs += [_const_spec(w_mix.shape)]
        args += [a, b, c, gate2, w_mix]
    for stack, layer, regroup in to_round:
        _, rows, cols = stack.shape
        out_cols = cols if regroup is None else regroup[1]
        if rows % steps == 0 and (rows // steps) % BF16_ROWS == 0:
            slab, last = rows // steps, steps - 1
        else:
            slab = LANES
            last = rows // slab - 1
            assert rows % slab == 0 and last < steps, (rows, steps)
        in_specs.append(pl.BlockSpec((None, slab, cols),
                                     lambda i, layer=layer, last=last: (layer, jnp.minimum(i, last), 0)))
        out_specs.append(pl.BlockSpec((slab, out_cols), lambda i, last=last: (jnp.minimum(i, last), 0)))
        out_shape.append(jax.ShapeDtypeStruct((rows, out_cols), BF16))
        args.append(stack)
    regroups = tuple(None if r is None else r[0] for _, _, r in to_round)
    outs = pl.pallas_call(
        functools.partial(_ffn_kernel, mix is not None, regroups),
        out_shape=out_shape,
        grid=(steps,),
        in_specs=in_specs,
        out_specs=out_specs,
        scratch_shapes=[pltpu.VMEM((tm, D), F32)],
        compiler_params=pltpu.CompilerParams(
            dimension_semantics=("arbitrary",), vmem_limit_bytes=VMEM_LIMIT),
        name="ffn",
    )(*args)
    return tuple(outs)


def _head_rms(x, seg2, gain, head_dim):
    ms = _split_dot(x * x, seg2) * (1.0 / head_dim)
    return x * lax.rsqrt(ms + EPS) * gain


def _conv_module(pad_ref, sh_ref, w_ref, b_ref, lg_ref, lb_ref, o_ref):
    tm = pad_ref.shape[0] - CONV_PAD
    S = SUBLANES
    for r in range(1, S):
        sh_ref[r - 1, S:, :] = pad_ref[S - r:tm + CONV_PAD - r, :]
    for i in range(tm // CONV_TILE):
        acc = jnp.zeros((CONV_TILE, CONV_CH), F32)
        for j in range(CONV_WIDTH):
            a, r = divmod(CONV_WIDTH - 1 - j, S)
            start = i * CONV_TILE + CONV_PAD - S * a
            src = pad_ref[start:start + CONV_TILE, :] if r == 0 else sh_ref[r - 1, start:start + CONV_TILE, :]
            acc = acc + w_ref[j:j + 1, :] * src
        acc = acc + b_ref[...]
        mu = jnp.mean(acc, axis=-1, keepdims=True)
        xc = acc - mu
        var = jnp.mean(xc * xc, axis=-1, keepdims=True)
        y = xc * lax.rsqrt(var + EPS) * lg_ref[...] + lb_ref[...]
        o_ref[i * CONV_TILE:(i + 1) * CONV_TILE, :] = _silu(y).astype(o_ref.dtype)


def _inproj_kernel(tiles_per_batch, x_ref, g_ref, sh_ref, sc_ref, w_ref, wup_ref, bg_ref, seg_ref, qn_ref, kn_ref,
                   cw_ref, cb_ref, lg_ref, lb_ref,
                   gq_ref, gk_ref, ga_ref, gv_ref, gg_ref, sq_ref, sk_ref, sv_ref, oc_ref, pad_ref, shift_ref):
    hb = _modulated_norm(x_ref[...], g_ref[...], sc_ref[0], sh_ref[0]).astype(BF16)
    pc = _dot(hb, w_ref[:, _OFF_CA:])
    tm = x_ref.shape[0]
    tail = pad_ref[tm:tm + CONV_PAD, :]
    pad_ref[0:CONV_PAD, :] = jnp.where(pl.program_id(0) % tiles_per_batch == 0, 0.0, tail)
    pad_ref[CONV_PAD:, :] = pc[:, :CONV_CH] * _sigmoid(pc[:, CONV_CH:])
    _conv_module(pad_ref, shift_ref, cw_ref, cb_ref, lg_ref, lb_ref, oc_ref)

    p = _dot(hb, w_ref[:, :_OFF_CA])
    gq_ref[...] = p[:, _OFF_GQ:_OFF_GQ + QK_PAD]
    gk_ref[...] = p[:, _OFF_GK:_OFF_GK + QK_PAD]
    gv_ref[...] = p[:, _OFF_GV:_OFF_GV + GLA_V]
    gg_ref[...] = p[:, _OFF_GG:_OFF_GG + GLA_V]
    r = p[:, _OFF_GR:_OFF_GR + RANK_PAD].astype(BF16)
    ga_ref[...] = _log_sigmoid(_dot(r, wup_ref[...]) + bg_ref[...]) * (1.0 / GLA_TAU)
    seg = seg_ref[...]
    for t in range(SB_W // LANES):
        lanes = slice(t * LANES, (t + 1) * LANES)
        sq = _head_rms(p[:, _OFF_SQ + t * LANES:_OFF_SQ + (t + 1) * LANES], seg, qn_ref[:, lanes], SB_DH)
        sq_ref[:, lanes] = (sq * (SB_DH ** -0.5)).astype(BF16)
        sk = _head_rms(p[:, _OFF_SK + t * LANES:_OFF_SK + (t + 1) * LANES], seg, kn_ref[:, lanes], SB_DH)
        sk_ref[:, lanes] = sk.astype(BF16)
    sv_ref[...] = p[:, _OFF_SV:_OFF_SV + SB_W].astype(BF16)


def _same_head(width, head_dim):
    h = np.arange(width) // head_dim
    m = (h[:, None] == h[None, :]).astype(np.float32)
    return jnp.asarray(np.concatenate([m, m], axis=0), BF16)


def _inproj(x2, g, shift, scale, w, w_up, b_gate, q_norm, k_norm, conv_w, conv_b, conv_ln_g, conv_ln_b, seq):
    N, D = x2.shape
    wup = jnp.pad(w_up, ((0, RANK_PAD - GLA_RANK), (0, QK_PAD - GLA_QK))).astype(BF16)
    bg = jnp.pad(b_gate, (0, QK_PAD - GLA_QK)).reshape(1, QK_PAD)
    tm = TOKEN_TILE
    tpb = seq // tm
    seg = _same_head(LANES, SB_DH)
    heads = lambda gn: jnp.tile(gn, SB_HEADS).reshape(1, SB_W)
    outs = ((QK_PAD, F32), (QK_PAD, F32), (QK_PAD, F32), (GLA_V, F32), (GLA_V, F32),
            (SB_W, BF16), (SB_W, BF16), (SB_W, BF16), (CONV_CH, BF16))
    chan = lambda a: a.reshape(1, CONV_CH)
    return pl.pallas_call(
        functools.partial(_inproj_kernel, tpb),
        out_shape=[jax.ShapeDtypeStruct((N, wd), dt) for wd, dt in outs],
        grid=(N // tm,),
        in_specs=[
            _row_spec(tm, D),
            _const_spec((1, D)),
            _batch_vec_spec(tpb, D), _batch_vec_spec(tpb, D),
            _const_spec(w.shape), _const_spec(wup.shape), _const_spec(bg.shape),
            _const_spec(seg.shape), _const_spec((1, SB_W)), _const_spec((1, SB_W)),
            _const_spec(conv_w.shape), _const_spec((1, CONV_CH)), _const_spec((1, CONV_CH)), _const_spec((1, CONV_CH)),
        ],
        out_specs=[_row_spec(tm, wd) for wd, _ in outs],
        scratch_shapes=[pltpu.VMEM((tm + CONV_PAD, CONV_CH), F32),
                        pltpu.VMEM((SUBLANES - 1, tm + CONV_PAD, CONV_CH), F32)],
        compiler_params=pltpu.CompilerParams(
            dimension_semantics=("arbitrary",), vmem_limit_bytes=VMEM_LIMIT),
        name="inproj",
    )(x2, g.reshape(1, D), shift, scale, w, wup, bg, seg, heads(q_norm), heads(k_norm),
      conv_w, chan(conv_b), chan(conv_ln_g), chan(conv_ln_b))


_GLA_LEVELS = 6


def _gla_constants():
    C = CHUNK
    t = np.arange(C)[:, None]
    j = np.arange(C)[None, :]
    mats = [j <= t, j > t]
    masks = [t == j]
    for l in range(1, _GLA_LEVELS + 1):
        n, m = 1 << l, 1 << (l - 1)
        ref = t // n * n + m - 1
        right = (t % n) >= m
        mats.append((right & (j > ref) & (j <= t)) | ((~right) & (j > t) & (j <= ref)))
        masks.append(((t // n) == (j // n)) & right & ((j % n) < m))
    prefix = np.concatenate(mats, axis=0).astype(np.float32)
    prefix = np.concatenate([prefix, prefix], axis=1)
    masks = np.stack([np.tile(mk, (1, GLA_HEADS)) for mk in masks]).astype(np.float32)
    kl = np.arange(QK_PAD)[None, :] // GLA_DK
    vl = np.arange(GLA_V)[:, None] // GLA_DV
    state_mask = (kl == vl).astype(np.float32)
    return prefix, masks, state_mask


def _gla_kernel(q_ref, k_ref, a_ref, v_ref, g_ref, pre_ref, msk_ref, smask_ref, seg_ref, on_ref,
                o_ref, st_ref):
    C = CHUNK

    @pl.when(pl.program_id(1) == 0)
    def _():
        st_ref[...] = jnp.zeros_like(st_ref)

    klane = lax.broadcasted_iota(jnp.int32, (1, QK_PAD), 1) // GLA_DK
    vlane = lax.broadcasted_iota(jnp.int32, (1, GLA_V), 1) // GLA_DV
    n_grp = GLA_GROUP

    def group(i, carry):
        base = i * (n_grp * C)
        rows = [pl.ds(pl.multiple_of(base + j * C, C), C) for j in range(n_grp)]
        qs = [q_ref[0, r, :] * (GLA_DK ** -0.5) for r in rows]
        ks = [k_ref[0, r, :] for r in rows]
        vs = [v_ref[0, r, :] for r in rows]

        es = []
        for r in rows:
            es.append(_split_dot_left(pre_ref[...], a_ref[0, r, :]))

        atts = []
        for q, k, e in zip(qs, ks, es):
            att = None
            for l in range(_GLA_LEVELS + 1):
                if l == 0:
                    qt, kt = q, k
                else:
                    f = jnp.exp(e[(l + 1) * C:(l + 2) * C])
                    qt, kt = q * f, k * f
                kst = jnp.concatenate(
                    [jnp.where(klane == h, kt, 0.0) for h in range(GLA_HEADS)], axis=0).astype(BF16)
                term = _dot_nt(qt.astype(BF16), kst) * msk_ref[l]
                att = term if att is None else att + term
            atts.append(att)

        intra, upds, qbs, decs = [], [], [], []
        for q, k, v, e, att in zip(qs, ks, vs, es, atts):
            vst = jnp.concatenate(
                [jnp.where(vlane == h, v, 0.0) for h in range(GLA_HEADS)], axis=0).astype(BF16)
            intra.append(_dot(att.astype(BF16), vst))
            kdec = (k * jnp.exp(e[C:2 * C])).astype(BF16)
            upds.append(_dot_tn(v.astype(BF16), kdec) * smask_ref[...])
            qbs.append((q * jnp.exp(e[0:C])).astype(BF16))
            decs.append(jnp.exp(e[C - 1:C]))

        st = st_ref[...]
        outs = []
        for o_intra, upd, qb, dec in zip(intra, upds, qbs, decs):
            outs.append(o_intra + _dot_nt(qb, st.astype(BF16)))
            st = st * dec + upd
        st_ref[...] = st

        o = jnp.concatenate(outs, axis=0)
        out_rows = pl.ds(pl.multiple_of(base, n_grp * C), n_grp * C)
        y = _head_rms(o, seg_ref[...], on_ref[...], GLA_DV)
        o_ref[0, out_rows, :] = (y * _silu(g_ref[0, out_rows, :])).astype(o_ref.dtype)
        return carry

    lax.fori_loop(0, q_ref.shape[1] // (n_grp * C), group, 0)


def _gla(gq, gk, ga, gv, gg, out_norm):
    B, T, _ = gq.shape
    prefix, masks, state_mask = _gla_constants()
    tt = GLA_TILE
    seq_spec = lambda w: pl.BlockSpec((1, tt, w), lambda b, i: (b, i, 0))
    consts = (jnp.asarray(prefix, BF16), jnp.asarray(masks), jnp.asarray(state_mask),
              _same_head(GLA_V, GLA_DV), jnp.tile(out_norm, GLA_HEADS).reshape(1, GLA_V))
    return pl.pallas_call(
        _gla_kernel,
        out_shape=jax.ShapeDtypeStruct((B, T, GLA_V), BF16),
        grid=(B, T // tt),
        in_specs=[seq_spec(QK_PAD), seq_spec(QK_PAD), seq_spec(QK_PAD), seq_spec(GLA_V), seq_spec(GLA_V)]
                 + [_const_spec(cst.shape) for cst in consts],
        out_specs=seq_spec(GLA_V),
        scratch_shapes=[pltpu.VMEM((GLA_V, QK_PAD), F32)],
        compiler_params=pltpu.CompilerParams(
            dimension_semantics=("parallel", "arbitrary"), vmem_limit_bytes=VMEM_LIMIT),
        name="gla",
    )(gq, gk, ga, gv, gg, *consts)


def _sb_kernel(q_ref, k_ref, v_ref, og_ref, seg_ref, causal_ref, suf_ref, o_ref,
               qs_ref, kn_ref, v0_ref, v1_ref, acc_ref, carry_ref, alive_ref):
    T = q_ref.shape[1]
    BLK, G = SB_BLOCK, SB_GROUP
    n_q = T // BLK
    n_groups = n_q // G
    first = lax.broadcasted_iota(jnp.int32, (1, 2 * SB_DH), 1) < SB_DH

    kn_ref[0:BLK, :] = jnp.zeros((BLK, 2 * SB_DH), BF16)
    kn_ref[BLK:, :] = k_ref[0]
    v0_ref[0:BLK, :] = jnp.zeros((BLK, SB_DH), BF16)
    v1_ref[0:BLK, :] = jnp.zeros((BLK, SB_DH), BF16)
    v0_ref[BLK:, :] = v_ref[0, :, 0:SB_DH]
    v1_ref[BLK:, :] = v_ref[0, :, SB_DH:]

    for qi in range(n_q):
        q2 = q_ref[0, qi * BLK:(qi + 1) * BLK, :]
        zq = jnp.zeros_like(q2)
        qs_ref[qi, 0:BLK, :] = jnp.where(first, q2, zq)
        qs_ref[qi, BLK:, :] = jnp.where(first, zq, q2)

    def group_step(grp, dist, causal):
        qis = [grp * G + g for g in range(G)]
        krows = [pl.ds(pl.multiple_of(jnp.maximum(qi - dist + 1, 0) * BLK, BLK), BLK) for qi in qis]
        zs = [_dot_nt(kn_ref[kr, :], qs_ref[qi]) for qi, kr in zip(qis, krows)]
        log_betas, drops = [], []
        for z in zs:
            sp = jnp.maximum(z, 0.0) + jnp.log(1.0 + jnp.exp(-jnp.abs(z)))
            log_betas.append(z - sp)
            drops.append(sp * causal_ref[...] if causal else sp)
        sufs = [_split_dot_left(suf_ref[...], d) for d in drops]
        ws, alive = [], None
        for qi, lb, d, suf in zip(qis, log_betas, drops, sufs):
            carry = jnp.zeros((1, 2 * BLK), F32) if causal else carry_ref[qi]
            w = jnp.exp(lb - suf + carry)
            ws.append((w * causal_ref[...] if causal else w).astype(BF16))
            carry = carry - (suf[0:1, :] + d[0:1, :])
            carry_ref[qi] = carry
            alive = carry if alive is None else jnp.maximum(alive, carry)
        for qi, kr, w in zip(qis, krows, ws):
            for h, v_ref_h in enumerate((v0_ref, v1_ref)):
                upd = _dot_tn(v_ref_h[kr, :], w[:, h * BLK:(h + 1) * BLK])
                out = (qi, slice(h * SB_DH, (h + 1) * SB_DH), slice(None))
                acc_ref[out] = upd if causal else acc_ref[out] + upd
        return jnp.max(alive) > -SB_SKIP

    def diagonal(grp, c0):
        alive_ref[grp] = group_step(grp, 0, True).astype(jnp.int32)
        return c0

    lax.fori_loop(0, n_groups, diagonal, 0)

    def any_alive():
        total = alive_ref[0]
        for g in range(1, n_groups):
            total = total + alive_ref[g]
        return total > 0

    def sweep(state):
        dist, _ = state

        def visit(grp, c0):
            run = jnp.logical_and(alive_ref[grp] > 0, grp * G + G - 1 >= dist)

            @pl.when(run)
            def _():
                alive_ref[grp] = group_step(grp, dist, False).astype(jnp.int32)

            @pl.when(jnp.logical_not(run))
            def _():
                alive_ref[grp] = 0

            return c0

        lax.fori_loop(0, n_groups, visit, 0)
        return dist + 1, any_alive()

    lax.while_loop(lambda s: jnp.logical_and(s[0] < n_q, s[1]), sweep, (1, any_alive()))

    def finish(grp, c0):
        cols = pl.ds(pl.multiple_of(grp * G * BLK, G * BLK), G * BLK)
        ot = jnp.concatenate([acc_ref[grp * G + g] for g in range(G)], axis=1)
        o_ref[0, cols, :] = _head_rms(ot.T, seg_ref[...], og_ref[...], SB_DH).astype(o_ref.dtype)
        return c0

    lax.fori_loop(0, n_groups, finish, 0)


def _sb_constants():
    BLK = SB_BLOCK
    s = np.arange(BLK)[:, None]
    j = np.arange(BLK)[None, :]
    suffix = (j > s).astype(np.float32)
    suffix = np.concatenate([suffix, suffix], axis=1)
    causal =(s < np.tile(np.arange(BLK), 2)[None, :]).astype(np.float32)
    return causal, suffix


def _sb(sq, sk, sv, out_norm):
    B, T, _ = sq.shape
    causal, suffix = _sb_constants()
    consts = (jnp.tile(out_norm, 2).reshape(1, 2 * SB_DH), _same_head(2 * SB_DH, SB_DH),
              jnp.asarray(causal), jnp.asarray(suffix, BF16))
    spec = pl.BlockSpec((1, T, 2 * SB_DH), lambda b, p: (b, 0, p))
    assert T % (SB_GROUP * SB_BLOCK) == 0, (T, SB_GROUP, SB_BLOCK)
    n_q = T // SB_BLOCK
    return pl.pallas_call(
        _sb_kernel,
        out_shape=jax.ShapeDtypeStruct((B, T, SB_W), BF16),
        grid=(B, SB_HEADS // 2),
        in_specs=[spec, spec, spec] + [_const_spec(cst.shape) for cst in consts],
        out_specs=spec,
        scratch_shapes=[pltpu.VMEM((n_q, 2 * SB_BLOCK, 2 * SB_DH), BF16),
                        pltpu.VMEM((T + SB_BLOCK, 2 * SB_DH), BF16),
                        pltpu.VMEM((T + SB_BLOCK, SB_DH), BF16),
                        pltpu.VMEM((T + SB_BLOCK, SB_DH), BF16),
                        pltpu.VMEM((n_q, 2 * SB_DH, SB_BLOCK), F32),
                        pltpu.VMEM((n_q, 1, 2 * SB_BLOCK), F32),
                        pltpu.SMEM((n_q // SB_GROUP,), jnp.int32)],
        compiler_params=pltpu.CompilerParams(
            dimension_semantics=("parallel", "parallel"), vmem_limit_bytes=VMEM_LIMIT),
        name="stickbreak",
    )(sq, sk, sv, *consts)


def kernel(x, c, w_ada, b_ada, norm_ffn1, ffn1_w_in, ffn1_w_out, norm_mix, w_in, w_out, gla_w_gate_up, gla_b_gate, gla_out_norm, sb_q_norm, sb_k_norm, sb_out_norm, conv_w, conv_b, conv_ln_g, conv_ln_b, norm_ffn2, ffn2_w_in, ffn2_w_out):
    B, T, D = x.shape
    L = w_ada.shape[0]
    mod = _adaln(c, w_ada, b_ada).reshape(L, B, N_MOD, 1, D)
    x2 = x.reshape(B * T, D)
    ffn_w = (ffn1_w_in[0].astype(BF16), ffn1_w_out[0].astype(BF16))
    for l in range(L):
        sh1, sc1, gt1, sh2, sc2, gt2, sh3, sc3, gt3 = (mod[l, :, i] for i in range(N_MOD))
        x2, *ffn_w, w_mix, w_proj = _ffn(
            x2, norm_ffn1[l], sh1, sc1, gt1, *ffn_w, T,
            to_round=((ffn2_w_in, l, None), (ffn2_w_out, l, None), (w_out, l, None),
                      (w_in, l, (_PROJ_GROUPS, PROJ_W))))
        gq, gk, ga, gv, gg, sq, sk, sv, oc = _inproj(
            x2, norm_mix[l], sh2, sc2, w_proj, gla_w_gate_up[l], gla_b_gate[l], sb_q_norm[l], sb_k_norm[l],
            conv_w[l], conv_b[l], conv_ln_g[l], conv_ln_b[l], T)
        seq = lambda a: a.reshape(B, T, a.shape[-1])
        flat = lambda a: a.reshape(B * T, a.shape[-1])
        oa = _gla(seq(gq), seq(gk), seq(ga), seq(gv), seq(gg), gla_out_norm[l])
        ob = _sb(seq(sq), seq(sk), seq(sv), sb_out_norm[l])
        mix = (flat(oa), flat(ob), oc, gt2, w_mix)
        nxt = ((ffn1_w_in, l + 1, None), (ffn1_w_out, l + 1, None)) if l + 1 < L else ()
        x2, *ffn_w = _ffn(x2, norm_ffn2[l], sh3, sc3, gt3, *ffn_w, T, mix=mix, to_round=nxt)
    return x2.reshape(B, T, D)
```

```python
import functools

import numpy as np
import jax
import jax.numpy as jnp
from jax import lax
from jax.experimental import pallas as pl
from jax.experimental.pallas import tpu as pltpu

F32 = jnp.float32
BF16 = jnp.bfloat16

EPS = 1e-6
CHUNK = 64
GLA_HEADS, GLA_DK, GLA_DV, GLA_RANK, GLA_TAU = 4, 48, 96, 16, 16.0
SB_HEADS, SB_DH = 6, 64
CONV_CH, CONV_WIDTH = 256, 31
D_FF = 2816
N_MOD = 9
GLA_QK = GLA_HEADS * GLA_DK
GLA_V = GLA_HEADS * GLA_DV
SB_W = SB_HEADS * SB_DH

LANES = 128
SUBLANES = 8
BF16_ROWS = 16
QK_PAD = 256
RANK_PAD = 128
FF_CHUNK = 256
TOKEN_TILE = 512
SB_BLOCK = 128
SB_GROUP = 8
SB_SKIP = 104.0
GLA_TILE = 512
GLA_GROUP = 4
CONV_TILE = 256
ADALN_TILE = 1536
CONV_PAD = 32
VMEM_LIMIT = 52 * 1024 * 1024

_OFF_GQ, _OFF_GK, _OFF_GV, _OFF_GG, _OFF_GR = 0, 256, 512, 896, 1280
_OFF_SQ, _OFF_SK, _OFF_SV, _OFF_CA, _OFF_CG = 1408, 1792, 2176, 2560, 2816
PROJ_W = 3072
_PROJ_GROUPS = []
_src = 0
for _width, _dst in ((GLA_QK, _OFF_GQ), (GLA_QK, _OFF_GK), (GLA_V, _OFF_GV), (GLA_V, _OFF_GG), (GLA_RANK, _OFF_GR),
                     (SB_W, _OFF_SQ), (SB_W, _OFF_SK), (SB_W, _OFF_SV), (CONV_CH, _OFF_CA), (CONV_CH, _OFF_CG)):
    _PROJ_GROUPS.append((_src, _width, _dst))
    _src += _width
_PROJ_GROUPS = tuple(_PROJ_GROUPS)


def _dot(a, b):
    return jnp.dot(a, b, preferred_element_type=F32)


def _dot_nt(a, b):
    return lax.dot_general(a, b, (((1,), (1,)), ((), ())), preferred_element_type=F32)


def _dot_tn(a, b):
    return lax.dot_general(a, b, (((0,), (0,)), ((), ())), preferred_element_type=F32)


def _hi_lo(x, axis):
    hi = x.astype(BF16)
    lo = (x - hi.astype(F32)).astype(BF16)
    return jnp.concatenate([hi, lo], axis=axis)


def _split_dot(x, m2):
    return _dot(_hi_lo(x, 1), m2)


def _split_dot_left(m2, x):
    return _dot(m2, _hi_lo(x, 0))


def _sigmoid(x):
    return 1.0 / (1.0 + jnp.exp(-x))


def _silu(x):
    return x * _sigmoid(x)


def _log_sigmoid(x):
    return jnp.minimum(x, 0.0) - jnp.log(1.0 + jnp.exp(-jnp.abs(x)))


def _modulated_norm(x, g, scale, shift):
    ms = jnp.mean(x * x, axis=-1, keepdims=True)
    return (x * lax.rsqrt(ms + EPS) * g) * (1.0 + scale) + shift


def _adaln_kernel(c_ref, w_ref, b_ref, o_ref):
    ca = _silu(c_ref[...]).astype(BF16)
    o_ref[0] = _dot(ca, w_ref[0].astype(BF16)) + b_ref[0]


def _adaln(c, w_ada, b_ada):
    L, D, W = w_ada.shape
    B = c.shape[0]
    tn = ADALN_TILE
    assert W % tn == 0, (W, tn)
    return pl.pallas_call(
        _adaln_kernel,
        out_shape=jax.ShapeDtypeStruct((L, B, W), F32),
        grid=(L, W // tn),
        in_specs=[
            pl.BlockSpec((B, D), lambda l, j: (0, 0)),
            pl.BlockSpec((1, D, tn), lambda l, j: (l, 0, j)),
            pl.BlockSpec((1, 1, tn), lambda l, j: (l, 0, j)),
        ],
        out_specs=pl.BlockSpec((1, B, tn), lambda l, j: (l, 0, j)),
        compiler_params=pltpu.CompilerParams(
            dimension_semantics=("parallel", "parallel"), vmem_limit_bytes=VMEM_LIMIT),
        name="adaln",
    )(c, w_ada, b_ada.reshape(L, 1, W))


def _ffn_kernel(has_mix, regroups, *refs):
    n_round = len(regroups)
    refs = list(refs)
    x_ref, g_ref, sh_ref, sc_ref, gt_ref, win_ref, wout_ref = refs[:7]
    del refs[:7]
    if has_mix:
        a_ref, b_ref, c_ref, gt2_ref, wmix_ref = refs[:5]
        del refs[:5]
    slabs = refs[:n_round]
    o_ref = refs[n_round]
    rounded = refs[n_round + 1:2 * n_round + 1]
    acc_ref = refs[2 * n_round + 1]
    for src_ref, dst_ref, groups in zip(slabs, rounded, regroups):
        if groups is None:
            dst_ref[...] = src_ref[...].astype(BF16)
        else:
            dst_ref[...] = jnp.zeros_like(dst_ref)
            for src_col, width, dst_col in groups:
                dst_ref[:, dst_col:dst_col + width] = src_ref[:, src_col:src_col + width].astype(BF16)
    x = x_ref[...]
    if has_mix:
        abc = jnp.concatenate([a_ref[...], b_ref[...], c_ref[...]], axis=1)
        x = x + gt2_ref[0] * _dot(abc, wmix_ref[...])
    hb = _modulated_norm(x, g_ref[...], sc_ref[0], sh_ref[0]).astype(BF16)
    d_ff = wout_ref.shape[0]
    for j in range(d_ff // FF_CHUNK):
        cols = slice(j * FF_CHUNK, (j + 1) * FF_CHUNK)
        a = _dot(hb, win_ref[:, cols])
        b = _dot(hb, win_ref[:, d_ff + j * FF_CHUNK:d_ff + (j + 1) * FF_CHUNK])
        y = _dot((_silu(a) * b).astype(BF16), wout_ref[cols, :])
        if j == 0:
            acc_ref[...] = y
        else:
            acc_ref[...] += y
    o_ref[...] = x + (0.5 * gt_ref[0]) * acc_ref[...]


def _const_spec(shape):
    nd = len(shape)
    return pl.BlockSpec(shape, lambda *_: (0,) * nd, pipeline_mode=pl.Buffered(1))


def _row_spec(tm, w):
    return pl.BlockSpec((tm, w), lambda i: (i, 0))


def _batch_vec_spec(tiles_per_batch, w):
    return pl.BlockSpec((1, 1, w), lambda i: (i // tiles_per_batch, 0, 0))


def _ffn(x2, g, shift, scale, gate, win, wout, seq, mix=None, to_round=()):
    N, D = x2.shape
    tm = TOKEN_TILE
    tpb = seq // tm
    steps = N // tm
    in_specs = [
        _row_spec(tm, D),
        _const_spec((1, D)),
        _batch_vec_spec(tpb, D), _batch_vec_spec(tpb, D), _batch_vec_spec(tpb, D),
        _const_spec(win.shape), _const_spec(wout.shape),
    ]
    out_shape = [jax.ShapeDtypeStruct((N, D), F32)]
    out_specs = [_row_spec(tm, D)]
    args = [x2, g.reshape(1, D), shift, scale, gate, win, wout]
    if mix is not None:
        a, b, c, gate2, w_mix = mix
        in_specs += [_row_spec(tm, t.shape[1]) for t in (a, b, c)] + [_batch_vec_spec(tpb, D)]
        in_specs += [_const_spec(w_mix.shape)]
        args += [a, b, c, gate2, w_mix]
    for stack, layer, regroup in to_round:
        _, rows, cols = stack.shape
        out_cols = cols if regroup is None else regroup[1]
        if rows % steps == 0 and (rows // steps) % BF16_ROWS == 0:
            slab, last = rows // steps, steps - 1
        else:
            slab = LANES
            last = rows // slab - 1
            assert rows % slab == 0 and last < steps, (rows, steps)
        in_specs.append(pl.BlockSpec((None, slab, cols),
                                     lambda i, layer=layer, last=last: (layer, jnp.minimum(i, last), 0)))
        out_specs.append(pl.BlockSpec((slab, out_cols), lambda i, last=last: (jnp.minimum(i, last), 0)))
        out_shape.append(jax.ShapeDtypeStruct((rows, out_cols), BF16))
        args.append(stack)
    regroups = tuple(None if r is None else r[0] for _, _, r in to_round)
    outs = pl.pallas_call(
        functools.partial(_ffn_kernel, mix is not None, regroups),
        out_shape=out_shape,
        grid=(steps,),
        in_specs=in_specs,
        out_specs=out_specs,
        scratch_shapes=[pltpu.VMEM((tm, D), F32)],
        compiler_params=pltpu.CompilerParams(
            dimension_semantics=("arbitrary",), vmem_limit_bytes=VMEM_LIMIT),
        name="ffn",
    )(*args)
    return tuple(outs)


def _head_rms(x, seg2, gain, head_dim):
    ms = _split_dot(x * x, seg2) * (1.0 / head_dim)
    return x * lax.rsqrt(ms + EPS) * gain


def _conv_module(pad_ref, sh_ref, w_ref, b_ref, lg_ref, lb_ref, o_ref):
    tm = pad_ref.shape[0] - CONV_PAD
    S = SUBLANES
    for r in range(1, S):
        sh_ref[r - 1, S:, :] = pad_ref[S - r:tm + CONV_PAD - r, :]
    for i in range(tm // CONV_TILE):
        acc = jnp.zeros((CONV_TILE, CONV_CH), F32)
        for j in range(CONV_WIDTH):
            a, r = divmod(CONV_WIDTH - 1 - j, S)
            start = i * CONV_TILE + CONV_PAD - S * a
            src = pad_ref[start:start + CONV_TILE, :] if r == 0 else sh_ref[r - 1, start:start + CONV_TILE, :]
            acc = acc + w_ref[j:j + 1, :] * src
        acc = acc + b_ref[...]
        mu = jnp.mean(acc, axis=-1, keepdims=True)
        xc = acc - mu
        var = jnp.mean(xc * xc, axis=-1, keepdims=True)
        y = xc * lax.rsqrt(var + EPS) * lg_ref[...] + lb_ref[...]
        o_ref[i * CONV_TILE:(i + 1) * CONV_TILE, :] = _silu(y).astype(o_ref.dtype)


def _inproj_kernel(tiles_per_batch, x_ref, g_ref, sh_ref, sc_ref, w_ref, wup_ref, bg_ref, seg_ref, qn_ref, kn_ref,
                   cw_ref, cb_ref, lg_ref, lb_ref,
                   gq_ref, gk_ref, ga_ref, gv_ref, gg_ref, sq_ref, sk_ref, sv_ref, oc_ref, pad_ref, shift_ref):
    hb = _modulated_norm(x_ref[...], g_ref[...], sc_ref[0], sh_ref[0]).astype(BF16)
    pc = _dot(hb, w_ref[:, _OFF_CA:])
    tm = x_ref.shape[0]
    tail = pad_ref[tm:tm + CONV_PAD, :]
    pad_ref[0:CONV_PAD, :] = jnp.where(pl.program_id(0) % tiles_per_batch == 0, 0.0, tail)
    pad_ref[CONV_PAD:, :] = pc[:, :CONV_CH] * _sigmoid(pc[:, CONV_CH:])
    _conv_module(pad_ref, shift_ref, cw_ref, cb_ref, lg_ref, lb_ref, oc_ref)

    p = _dot(hb, w_ref[:, :_OFF_CA])
    gq_ref[...] = p[:, _OFF_GQ:_OFF_GQ + QK_PAD]
    gk_ref[...] = p[:, _OFF_GK:_OFF_GK + QK_PAD]
    gv_ref[...] = p[:, _OFF_GV:_OFF_GV + GLA_V]
    gg_ref[...] = p[:, _OFF_GG:_OFF_GG + GLA_V]
    r = p[:, _OFF_GR:_OFF_GR + RANK_PAD].astype(BF16)
    ga_ref[...] = _log_sigmoid(_dot(r, wup_ref[...]) + bg_ref[...]) * (1.0 / GLA_TAU)
    seg = seg_ref[...]
    for t in range(SB_W // LANES):
        lanes = slice(t * LANES, (t + 1) * LANES)
        sq = _head_rms(p[:, _OFF_SQ + t * LANES:_OFF_SQ + (t + 1) * LANES], seg, qn_ref[:, lanes], SB_DH)
        sq_ref[:, lanes] = (sq * (SB_DH ** -0.5)).astype(BF16)
        sk = _head_rms(p[:, _OFF_SK + t * LANES:_OFF_SK + (t + 1) * LANES], seg, kn_ref[:, lanes], SB_DH)
        sk_ref[:, lanes] = sk.astype(BF16)
    sv_ref[...] = p[:, _OFF_SV:_OFF_SV + SB_W].astype(BF16)


def _same_head(width, head_dim):
    h = np.arange(width) // head_dim
    m = (h[:, None] == h[None, :]).astype(np.float32)
    return jnp.asarray(np.concatenate([m, m], axis=0), BF16)


def _inproj(x2, g, shift, scale, w, w_up, b_gate, q_norm, k_norm, conv_w, conv_b, conv_ln_g, conv_ln_b, seq):
    N, D = x2.shape
    wup = jnp.pad(w_up, ((0, RANK_PAD - GLA_RANK), (0, QK_PAD - GLA_QK))).astype(BF16)
    bg = jnp.pad(b_gate, (0, QK_PAD - GLA_QK)).reshape(1, QK_PAD)
    tm = TOKEN_TILE
    tpb = seq // tm
    seg = _same_head(LANES, SB_DH)
    heads = lambda gn: jnp.tile(gn, SB_HEADS).reshape(1, SB_W)
    outs = ((QK_PAD, F32), (QK_PAD, F32), (QK_PAD, F32), (GLA_V, F32), (GLA_V, F32),
            (SB_W, BF16), (SB_W, BF16), (SB_W, BF16), (CONV_CH, BF16))
    chan = lambda a: a.reshape(1, CONV_CH)
    return pl.pallas_call(
        functools.partial(_inproj_kernel, tpb),
        out_shape=[jax.ShapeDtypeStruct((N, wd), dt) for wd, dt in outs],
        grid=(N // tm,),
        in_specs=[
            _row_spec(tm, D),
            _const_spec((1, D)),
            _batch_vec_spec(tpb, D), _batch_vec_spec(tpb, D),
            _const_spec(w.shape), _const_spec(wup.shape), _const_spec(bg.shape),
            _const_spec(seg.shape), _const_spec((1, SB_W)), _const_spec((1, SB_W)),
            _const_spec(conv_w.shape), _const_spec((1, CONV_CH)), _const_spec((1, CONV_CH)), _const_spec((1, CONV_CH)),
        ],
        out_specs=[_row_spec(tm, wd) for wd, _ in outs],
        scratch_shapes=[pltpu.VMEM((tm + CONV_PAD, CONV_CH), F32),
                        pltpu.VMEM((SUBLANES - 1, tm + CONV_PAD, CONV_CH), F32)],
        compiler_params=pltpu.CompilerParams(
            dimension_semantics=("arbitrary",), vmem_limit_bytes=VMEM_LIMIT),
        name="inproj",
    )(x2, g.reshape(1, D), shift, scale, w, wup, bg, seg, heads(q_norm), heads(k_norm),
      conv_w, chan(conv_b), chan(conv_ln_g), chan(conv_ln_b))


_GLA_LEVELS = 6


def _gla_constants():
    C = CHUNK
    t = np.arange(C)[:, None]
    j = np.arange(C)[None, :]
    mats = [j <= t, j > t]
    masks = [t == j]
    for l in range(1, _GLA_LEVELS + 1):
        n, m = 1 << l, 1 << (l - 1)
        ref = t // n * n + m - 1
        right = (t % n) >= m
        mats.append((right & (j > ref) & (j <= t)) | ((~right) & (j > t) & (j <= ref)))
        masks.append(((t // n) == (j // n)) & right & ((j % n) < m))
    prefix = np.concatenate(mats, axis=0).astype(np.float32)
    prefix = np.concatenate([prefix, prefix], axis=1)
    masks = np.stack([np.tile(mk, (1, GLA_HEADS)) for mk in masks]).astype(np.float32)
    kl = np.arange(QK_PAD)[None, :] // GLA_DK
    vl = np.arange(GLA_V)[:, None] // GLA_DV
    state_mask = (kl == vl).astype(np.float32)
    return prefix, masks, state_mask


def _gla_kernel(q_ref, k_ref, a_ref, v_ref, g_ref, pre_ref, msk_ref, smask_ref, seg_ref, on_ref,
                o_ref, st_ref):
    C = CHUNK

    @pl.when(pl.program_id(1) == 0)
    def _():
        st_ref[...] = jnp.zeros_like(st_ref)

    klane = lax.broadcasted_iota(jnp.int32, (1, QK_PAD), 1) // GLA_DK
    vlane = lax.broadcasted_iota(jnp.int32, (1, GLA_V), 1) // GLA_DV
    n_grp = GLA_GROUP

    def group(i, carry):
        base = i * (n_grp * C)
        rows = [pl.ds(pl.multiple_of(base + j * C, C), C) for j in range(n_grp)]
        qs = [q_ref[0, r, :] * (GLA_DK ** -0.5) for r in rows]
        ks = [k_ref[0, r, :] for r in rows]
        vs = [v_ref[0, r, :] for r in rows]

        es = []
        for r in rows:
            es.append(_split_dot_left(pre_ref[...], a_ref[0, r, :]))

        atts = []
        for q, k, e in zip(qs, ks, es):
            att = None
            for l in range(_GLA_LEVELS + 1):
                if l == 0:
                    qt, kt = q, k
                else:
                    f = jnp.exp(e[(l + 1) * C:(l + 2) * C])
                    qt, kt = q * f, k * f
                kst = jnp.concatenate(
                    [jnp.where(klane == h, kt, 0.0) for h in range(GLA_HEADS)], axis=0).astype(BF16)
                term = _dot_nt(qt.astype(BF16), kst) * msk_ref[l]
                att = term if att is None else att + term
            atts.append(att)

        intra, upds, qbs, decs = [], [], [], []
        for q, k, v, e, att in zip(qs, ks, vs, es, atts):
            vst = jnp.concatenate(
                [jnp.where(vlane == h, v, 0.0) for h in range(GLA_HEADS)], axis=0).astype(BF16)
            intra.append(_dot(att.astype(BF16), vst))
            kdec = (k * jnp.exp(e[C:2 * C])).astype(BF16)
            upds.append(_dot_tn(v.astype(BF16), kdec) * smask_ref[...])
            qbs.append((q * jnp.exp(e[0:C])).astype(BF16))
            decs.append(jnp.exp(e[C - 1:C]))

        st = st_ref[...]
        outs = []
        for o_intra, upd, qb, dec in zip(intra, upds, qbs, decs):
            outs.append(o_intra + _dot_nt(qb, st.astype(BF16)))
            st = st * dec + upd
        st_ref[...] = st

        o = jnp.concatenate(outs, axis=0)
        out_rows = pl.ds(pl.multiple_of(base, n_grp * C), n_grp * C)
        y = _head_rms(o, seg_ref[...], on_ref[...], GLA_DV)
        o_ref[0, out_rows, :] = (y * _silu(g_ref[0, out_rows, :])).astype(o_ref.dtype)
        return carry

    lax.fori_loop(0, q_ref.shape[1] // (n_grp * C), group, 0)


def _gla(gq, gk, ga, gv, gg, out_norm):
    B, T, _ = gq.shape
    prefix, masks, state_mask = _gla_constants()
    tt = GLA_TILE
    seq_spec = lambda w: pl.BlockSpec((1, tt, w), lambda b, i: (b, i, 0))
    consts = (jnp.asarray(prefix, BF16), jnp.asarray(masks), jnp.asarray(state_mask),
              _same_head(GLA_V, GLA_DV), jnp.tile(out_norm, GLA_HEADS).reshape(1, GLA_V))
    return pl.pallas_call(
        _gla_kernel,
        out_shape=jax.ShapeDtypeStruct((B, T, GLA_V), BF16),
        grid=(B, T // tt),
        in_specs=[seq_spec(QK_PAD), seq_spec(QK_PAD), seq_spec(QK_PAD), seq_spec(GLA_V), seq_spec(GLA_V)]
                 + [_const_spec(cst.shape) for cst in consts],
        out_specs=seq_spec(GLA_V),
        scratch_shapes=[pltpu.VMEM((GLA_V, QK_PAD), F32)],
        compiler_params=pltpu.CompilerParams(
            dimension_semantics=("parallel", "arbitrary"), vmem_limit_bytes=VMEM_LIMIT),
        name="gla",
    )(gq, gk, ga, gv, gg, *consts)


def _sb_kernel(q_ref, k_ref, v_ref, og_ref, seg_ref, causal_ref, suf_ref, o_ref,
               qs_ref, kn_ref, v0_ref, v1_ref, acc_ref, carry_ref, alive_ref):
    T = q_ref.shape[1]
    BLK, G = SB_BLOCK, SB_GROUP
    n_q = T // BLK
    n_groups = n_q // G
    first = lax.broadcasted_iota(jnp.int32, (1, 2 * SB_DH), 1) < SB_DH

    kn_ref[0:BLK, :] = jnp.zeros((BLK, 2 * SB_DH), BF16)
    kn_ref[BLK:, :] = k_ref[0]
    v0_ref[0:BLK, :] = jnp.zeros((BLK, SB_DH), BF16)
    v1_ref[0:BLK, :] = jnp.zeros((BLK, SB_DH), BF16)
    v0_ref[BLK:, :] = v_ref[0, :, 0:SB_DH]
    v1_ref[BLK:, :] = v_ref[0, :, SB_DH:]

    for qi in range(n_q):
        q2 = q_ref[0, qi * BLK:(qi + 1) * BLK, :]
        zq = jnp.zeros_like(q2)
        qs_ref[qi, 0:BLK, :] = jnp.where(first, q2, zq)
        qs_ref[qi, BLK:, :] = jnp.where(first, zq, q2)

    def group_step(grp, dist, causal):
        qis = [grp * G + g for g in range(G)]
        krows = [pl.ds(pl.multiple_of(jnp.maximum(qi - dist + 1, 0) * BLK, BLK), BLK) for qi in qis]
        zs = [_dot_nt(kn_ref[kr, :], qs_ref[qi]) for qi, kr in zip(qis, krows)]
        drops = []
        for z in zs:
            sp = jnp.maximum(z, 0.0) + jnp.log(1.0 + jnp.exp(-jnp.abs(z)))
            drops.append(sp * causal_ref[...] if causal else sp)
        sufs = [_split_dot_left(suf_ref[...], d) for d in drops]
        ws, alive = [], None
        for qi, z, suf in zip(qis, zs, sufs):
            carry = jnp.zeros((1, 2 * BLK), F32) if causal else carry_ref[qi]
            w = jnp.exp(z - suf + carry)
            ws.append((w * causal_ref[...] if causal else w).astype(BF16))
            carry = carry - suf[0:1, :]
            carry_ref[qi] = carry
            alive = carry if alive is None else jnp.maximum(alive, carry)
        for qi, kr, w in zip(qis, krows, ws):
            for h, v_ref_h in enumerate((v0_ref, v1_ref)):
                upd = _dot_tn(v_ref_h[kr, :], w[:, h * BLK:(h + 1) * BLK])
                out = (qi, slice(h * SB_DH, (h + 1) * SB_DH), slice(None))
                acc_ref[out] = upd if causal else acc_ref[out] + upd
        return jnp.max(alive) > -SB_SKIP

    def diagonal(grp, c0):
        alive_ref[grp] = group_step(grp, 0, True).astype(jnp.int32)
        return c0

    lax.fori_loop(0, n_groups, diagonal, 0)

    def any_alive():
        total = alive_ref[0]
        for g in range(1, n_groups):
            total = total + alive_ref[g]
        return total > 0

    def sweep(state):
        dist, _ = state

        def visit(grp, c0):
            run = jnp.logical_and(alive_ref[grp] > 0, grp * G + G - 1 >= dist)

            @pl.when(run)
            def _():
                alive_ref[grp] = group_step(grp, dist, False).astype(jnp.int32)

            @pl.when(jnp.logical_not(run))
            def _():
                alive_ref[grp] = 0

            return c0

        lax.fori_loop(0, n_groups, visit, 0)
        return dist + 1, any_alive()

    lax.while_loop(lambda s: jnp.logical_and(s[0] < n_q, s[1]), sweep, (1, any_alive()))

    def finish(grp, c0):
        cols = pl.ds(pl.multiple_of(grp * G * BLK, G * BLK), G * BLK)
        ot = jnp.concatenate([acc_ref[grp * G + g] for g in range(G)], axis=1)
        o_ref[0, cols, :] = _head_rms(ot.T, seg_ref[...], og_ref[...], SB_DH).astype(o_ref.dtype)
        return c0

    lax.fori_loop(0, n_groups, finish, 0)


def _sb_constants():
    BLK = SB_BLOCK
    s = np.arange(BLK)[:, None]
    j = np.arange(BLK)[None, :]
    suffix = (j >= s).astype(np.float32)
    suffix = np.concatenate([suffix, suffix], axis=1)
    causal =(s < np.tile(np.arange(BLK), 2)[None, :]).astype(np.float32)
    return causal, suffix


def _sb(sq, sk, sv, out_norm):
    B, T, _ = sq.shape
    causal, suffix = _sb_constants()
    consts = (jnp.tile(out_norm, 2).reshape(1, 2 * SB_DH), _same_head(2 * SB_DH, SB_DH),
              jnp.asarray(causal), jnp.asarray(suffix, BF16))
    spec = pl.BlockSpec((1, T, 2 * SB_DH), lambda b, p: (b, 0, p))
    assert T % (SB_GROUP * SB_BLOCK) == 0, (T, SB_GROUP, SB_BLOCK)
    n_q = T // SB_BLOCK
    return pl.pallas_call(
        _sb_kernel,
        out_shape=jax.ShapeDtypeStruct((B, T, SB_W), BF16),
        grid=(B, SB_HEADS // 2),
        in_specs=[spec, spec, spec] + [_const_spec(cst.shape) for cst in consts],
        out_specs=spec,
        scratch_shapes=[pltpu.VMEM((n_q, 2 * SB_BLOCK, 2 * SB_DH), BF16),
                        pltpu.VMEM((T + SB_BLOCK, 2 * SB_DH), BF16),
                        pltpu.VMEM((T + SB_BLOCK, SB_DH), BF16),
                        pltpu.VMEM((T + SB_BLOCK, SB_DH), BF16),
                        pltpu.VMEM((n_q, 2 * SB_DH, SB_BLOCK), F32),
                        pltpu.VMEM((n_q, 1, 2 * SB_BLOCK), F32),
                        pltpu.SMEM((n_q // SB_GROUP,), jnp.int32)],
        compiler_params=pltpu.CompilerParams(
            dimension_semantics=("parallel", "parallel"), vmem_limit_bytes=VMEM_LIMIT),
        name="stickbreak",
    )(sq, sk, sv, *consts)


def kernel(x, c, w_ada, b_ada, norm_ffn1, ffn1_w_in, ffn1_w_out, norm_mix, w_in, w_out, gla_w_gate_up, gla_b_gate, gla_out_norm, sb_q_norm, sb_k_norm, sb_out_norm, conv_w, conv_b, conv_ln_g, conv_ln_b, norm_ffn2, ffn2_w_in, ffn2_w_out):
    B, T, D = x.shape
    L = w_ada.shape[0]
    mod = _adaln(c, w_ada, b_ada).reshape(L, B, N_MOD, 1, D)
    x2 = x.reshape(B * T, D)
    ffn_w = (ffn1_w_in[0].astype(BF16), ffn1_w_out[0].astype(BF16))
    for l in range(L):
        sh1, sc1, gt1, sh2, sc2, gt2, sh3, sc3, gt3 = (mod[l, :, i] for i in range(N_MOD))
        x2, *ffn_w, w_mix, w_proj = _ffn(
            x2, norm_ffn1[l], sh1, sc1, gt1, *ffn_w, T,
            to_round=((ffn2_w_in, l, None), (ffn2_w_out, l, None), (w_out, l, None),
                      (w_in, l, (_PROJ_GROUPS, PROJ_W))))
        gq, gk, ga, gv, gg, sq, sk, sv, oc = _inproj(
            x2, norm_mix[l], sh2, sc2, w_proj, gla_w_gate_up[l], gla_b_gate[l], sb_q_norm[l], sb_k_norm[l],
            conv_w[l], conv_b[l], conv_ln_g[l], conv_ln_b[l], T)
        seq = lambda a: a.reshape(B, T, a.shape[-1])
        flat = lambda a: a.reshape(B * T, a.shape[-1])
        oa = _gla(seq(gq), seq(gk), seq(ga), seq(gv), seq(gg), gla_out_norm[l])
        ob = _sb(seq(sq), seq(sk), seq(sv), sb_out_norm[l])
        mix = (flat(oa), flat(ob), oc, gt2, w_mix)
        nxt = ((ffn1_w_in, l + 1, None), (ffn1_w_out, l + 1, None)) if l + 1 < L else ()
        x2, *ffn_w = _ffn(x2, norm_ffn2[l], sh3, sc3, gt3, *ffn_w, T, mix=mix, to_round=nxt)
    return x2.reshape(B, T, D)
```

```python
import functools

import numpy as np
import jax
import jax.numpy as jnp
from jax import lax
from jax.experimental import pallas as pl
from jax.experimental.pallas import tpu as pltpu

F32 = jnp.float32
BF16 = jnp.bfloat16

EPS = 1e-6
LOG2E = 1.4426950408889634
CHUNK = 64
GLA_HEADS, GLA_DK, GLA_DV, GLA_RANK, GLA_TAU = 4, 48, 96, 16, 16.0
SB_HEADS, SB_DH = 6, 64
CONV_CH, CONV_WIDTH = 256, 31
D_FF = 2816
N_MOD = 9
GLA_QK = GLA_HEADS * GLA_DK
GLA_V = GLA_HEADS * GLA_DV
SB_W = SB_HEADS * SB_DH

LANES = 128
SUBLANES = 8
BF16_ROWS = 16
QK_PAD = 256
RANK_PAD = 128
FF_CHUNK = 256
TOKEN_TILE = 512
SB_BLOCK = 128
SB_GROUP = 8
SB_SKIP = 104.0
GLA_TILE = 512
GLA_GROUP = 4
CONV_TILE = 256
ADALN_TILE = 1536
CONV_PAD = 32
VMEM_LIMIT = 52 * 1024 * 1024

_OFF_GQ, _OFF_GK, _OFF_GV, _OFF_GG, _OFF_GR = 0, 256, 512, 896, 1280
_OFF_SQ, _OFF_SK, _OFF_SV, _OFF_CA, _OFF_CG = 1408, 1792, 2176, 2560, 2816
PROJ_W = 3072
_PROJ_GROUPS = []
_src = 0
for _width, _dst in ((GLA_QK, _OFF_GQ), (GLA_QK, _OFF_GK), (GLA_V, _OFF_GV), (GLA_V, _OFF_GG), (GLA_RANK, _OFF_GR),
                     (SB_W, _OFF_SQ), (SB_W, _OFF_SK), (SB_W, _OFF_SV), (CONV_CH, _OFF_CA), (CONV_CH, _OFF_CG)):
    _PROJ_GROUPS.append((_src, _width, _dst))
    _src += _width
_PROJ_GROUPS = tuple(_PROJ_GROUPS)


def _dot(a, b):
    return jnp.dot(a, b, preferred_element_type=F32)


def _dot_nt(a, b):
    return lax.dot_general(a, b, (((1,), (1,)), ((), ())), preferred_element_type=F32)


def _dot_tn(a, b):
    return lax.dot_general(a, b, (((0,), (0,)), ((), ())), preferred_element_type=F32)


def _hi_lo(x, axis):
    hi = x.astype(BF16)
    lo = (x - hi.astype(F32)).astype(BF16)
    return jnp.concatenate([hi, lo], axis=axis)


def _split_dot(x, m2):
    return _dot(_hi_lo(x, 1), m2)


def _split_dot_left(m2, x):
    return _dot(m2, _hi_lo(x, 0))


def _sigmoid(x):
    return 1.0 / (1.0 + jnp.exp2(x * (-LOG2E)))


def _silu(x):
    return x * _sigmoid(x)


def _log_sigmoid(x):
    return jnp.minimum(x, 0.0) - jnp.log(1.0 + jnp.exp2(jnp.abs(x) * (-LOG2E)))


def _modulated_norm(x, g, scale, shift):
    ms = jnp.mean(x * x, axis=-1, keepdims=True)
    return (x * lax.rsqrt(ms + EPS) * g) * (1.0 + scale) + shift


def _adaln_kernel(c_ref, w_ref, b_ref, o_ref):
    ca = _silu(c_ref[...]).astype(BF16)
    o_ref[0] = _dot(ca, w_ref[0].astype(BF16)) + b_ref[0]


def _adaln(c, w_ada, b_ada):
    L, D, W = w_ada.shape
    B = c.shape[0]
    tn = ADALN_TILE
    assert W % tn == 0, (W, tn)
    return pl.pallas_call(
        _adaln_kernel,
        out_shape=jax.ShapeDtypeStruct((L, B, W), F32),
        grid=(L, W // tn),
        in_specs=[
            pl.BlockSpec((B, D), lambda l, j: (0, 0)),
            pl.BlockSpec((1, D, tn), lambda l, j: (l, 0, j)),
            pl.BlockSpec((1, 1, tn), lambda l, j: (l, 0, j)),
        ],
        out_specs=pl.BlockSpec((1, B, tn), lambda l, j: (l, 0, j)),
        compiler_params=pltpu.CompilerParams(
            dimension_semantics=("parallel", "parallel"), vmem_limit_bytes=VMEM_LIMIT),
        name="adaln",
    )(c, w_ada, b_ada.reshape(L, 1, W))


def _ffn_kernel(has_mix, regroups, *refs):
    n_round = len(regroups)
    refs = list(refs)
    x_ref, g_ref, sh_ref, sc_ref, gt_ref, win_ref, wout_ref = refs[:7]
    del refs[:7]
    if has_mix:
        a_ref, b_ref, c_ref, gt2_ref, wmix_ref = refs[:5]
        del refs[:5]
    slabs = refs[:n_round]
    o_ref = refs[n_round]
    rounded = refs[n_round + 1:2 * n_round + 1]
    acc_ref = refs[2 * n_round + 1]
    for src_ref, dst_ref, groups in zip(slabs, rounded, regroups):
        if groups is None:
            dst_ref[...] = src_ref[...].astype(BF16)
        else:
            dst_ref[...] = jnp.zeros_like(dst_ref)
            for src_col, width, dst_col in groups:
                dst_ref[:, dst_col:dst_col + width] = src_ref[:, src_col:src_col + width].astype(BF16)
    x = x_ref[...]
    if has_mix:
        abc = jnp.concatenate([a_ref[...], b_ref[...], c_ref[...]], axis=1)
        x = x + gt2_ref[0] * _dot(abc, wmix_ref[...])
    hb = _modulated_norm(x, g_ref[...], sc_ref[0], sh_ref[0]).astype(BF16)
    d_ff = wout_ref.shape[0]
    for j in range(d_ff // FF_CHUNK):
        cols = slice(j * FF_CHUNK, (j + 1) * FF_CHUNK)
        a = _dot(hb, win_ref[:, cols])
        b = _dot(hb, win_ref[:, d_ff + j * FF_CHUNK:d_ff + (j + 1) * FF_CHUNK])
        y = _dot((_silu(a) * b).astype(BF16), wout_ref[cols, :])
        if j == 0:
            acc_ref[...] = y
        else:
            acc_ref[...] += y
    o_ref[...] = x + (0.5 * gt_ref[0]) * acc_ref[...]


def _const_spec(shape):
    nd = len(shape)
    return pl.BlockSpec(shape, lambda *_: (0,) * nd, pipeline_mode=pl.Buffered(1))


def _row_spec(tm, w):
    return pl.BlockSpec((tm, w), lambda i: (i, 0))


def _batch_vec_spec(tiles_per_batch, w):
    return pl.BlockSpec((1, 1, w), lambda i: (i // tiles_per_batch, 0, 0))


def _ffn(x2, g, shift, scale, gate, win, wout, seq, mix=None, to_round=()):
    N, D = x2.shape
    tm = TOKEN_TILE
    tpb = seq // tm
    steps = N // tm
    in_specs = [
        _row_spec(tm, D),
        _const_spec((1, D)),
        _batch_vec_spec(tpb, D), _batch_vec_spec(tpb, D), _batch_vec_spec(tpb, D),
        _const_spec(win.shape), _const_spec(wout.shape),
    ]
    out_shape = [jax.ShapeDtypeStruct((N, D), F32)]
    out_specs = [_row_spec(tm, D)]
    args = [x2, g.reshape(1, D), shift, scale, gate, win, wout]
    if mix is not None:
        a, b, c, gate2, w_mix = mix
        in_specs += [_row_spec(tm, t.shape[1]) for t in (a, b, c)] + [_batch_vec_spec(tpb, D)]
        in_specs += [_const_spec(w_mix.shape)]
        args += [a, b, c, gate2, w_mix]
    for stack, layer, regroup in to_round:
        _, rows, cols = stack.shape
        out_cols = cols if regroup is None else regroup[1]
        if rows % steps == 0 and (rows // steps) % BF16_ROWS == 0:
            slab, last = rows // steps, steps - 1
        else:
            slab = LANES
            last = rows // slab - 1
            assert rows % slab == 0 and last < steps, (rows, steps)
        in_specs.append(pl.BlockSpec((None, slab, cols),
                                     lambda i, layer=layer, last=last: (layer, jnp.minimum(i, last), 0)))
        out_specs.append(pl.BlockSpec((slab, out_cols), lambda i, last=last: (jnp.minimum(i, last), 0)))
        out_shape.append(jax.ShapeDtypeStruct((rows, out_cols), BF16))
        args.append(stack)
    regroups = tuple(None if r is None else r[0] for _, _, r in to_round)
    outs = pl.pallas_call(
        functools.partial(_ffn_kernel, mix is not None, regroups),
        out_shape=out_shape,
        grid=(steps,),
        in_specs=in_specs,
        out_specs=out_specs,
        scratch_shapes=[pltpu.VMEM((tm, D), F32)],
        compiler_params=pltpu.CompilerParams(
            dimension_semantics=("arbitrary",), vmem_limit_bytes=VMEM_LIMIT),
        name="ffn",
    )(*args)
    return tuple(outs)


def _head_rms(x, seg2, gain, head_dim):
    ms = _split_dot(x * x, seg2) * (1.0 / head_dim)
    return x * lax.rsqrt(ms + EPS) * gain


def _conv_module(pad_ref, sh_ref, w_ref, b_ref, lg_ref, lb_ref, o_ref):
    tm = pad_ref.shape[0] - CONV_PAD
    S = SUBLANES
    for r in range(1, S):
        sh_ref[r - 1, S:, :] = pad_ref[S - r:tm + CONV_PAD - r, :]
    for i in range(tm // CONV_TILE):
        acc = jnp.zeros((CONV_TILE, CONV_CH), F32)
        for j in range(CONV_WIDTH):
            a, r = divmod(CONV_WIDTH - 1 - j, S)
            start = i * CONV_TILE + CONV_PAD - S * a
            src = pad_ref[start:start + CONV_TILE, :] if r == 0 else sh_ref[r - 1, start:start + CONV_TILE, :]
            acc = acc + w_ref[j:j + 1, :] * src
        acc = acc + b_ref[...]
        mu = jnp.mean(acc, axis=-1, keepdims=True)
        xc = acc - mu
        var = jnp.mean(xc * xc, axis=-1, keepdims=True)
        y = xc * lax.rsqrt(var + EPS) * lg_ref[...] + lb_ref[...]
        o_ref[i * CONV_TILE:(i + 1) * CONV_TILE, :] = _silu(y).astype(o_ref.dtype)


def _inproj_kernel(tiles_per_batch, x_ref, g_ref, sh_ref, sc_ref, w_ref, wup_ref, bg_ref, seg_ref, qn_ref, kn_ref,
                   cw_ref, cb_ref, lg_ref, lb_ref,
                   gq_ref, gk_ref, ga_ref, gv_ref, gg_ref, sq_ref, sk_ref, sv_ref, oc_ref, pad_ref, shift_ref):
    hb = _modulated_norm(x_ref[...], g_ref[...], sc_ref[0], sh_ref[0]).astype(BF16)
    pc = _dot(hb, w_ref[:, _OFF_CA:])
    tm = x_ref.shape[0]
    tail = pad_ref[tm:tm + CONV_PAD, :]
    pad_ref[0:CONV_PAD, :] = jnp.where(pl.program_id(0) % tiles_per_batch == 0, 0.0, tail)
    pad_ref[CONV_PAD:, :] = pc[:, :CONV_CH] * _sigmoid(pc[:, CONV_CH:])
    _conv_module(pad_ref, shift_ref, cw_ref, cb_ref, lg_ref, lb_ref, oc_ref)

    p = _dot(hb, w_ref[:, :_OFF_CA])
    gq_ref[...] = p[:, _OFF_GQ:_OFF_GQ + QK_PAD]
    gk_ref[...] = p[:, _OFF_GK:_OFF_GK + QK_PAD]
    gv_ref[...] = p[:, _OFF_GV:_OFF_GV + GLA_V]
    gg_ref[...] = p[:, _OFF_GG:_OFF_GG + GLA_V]
    r = p[:, _OFF_GR:_OFF_GR + RANK_PAD].astype(BF16)
    ga_ref[...] = _log_sigmoid(_dot(r, wup_ref[...]) + bg_ref[...]) * (1.0 / GLA_TAU)
    seg = seg_ref[...]
    for t in range(SB_W // LANES):
        lanes = slice(t * LANES, (t + 1) * LANES)
        sq = _head_rms(p[:, _OFF_SQ + t * LANES:_OFF_SQ + (t + 1) * LANES], seg, qn_ref[:, lanes], SB_DH)
        sq_ref[:, lanes] = (sq * (SB_DH ** -0.5)).astype(BF16)
        sk = _head_rms(p[:, _OFF_SK + t * LANES:_OFF_SK + (t + 1) * LANES], seg, kn_ref[:, lanes], SB_DH)
        sk_ref[:, lanes] = sk.astype(BF16)
    sv_ref[...] = p[:, _OFF_SV:_OFF_SV + SB_W].astype(BF16)


def _same_head(width, head_dim):
    h = np.arange(width) // head_dim
    m = (h[:, None] == h[None, :]).astype(np.float32)
    return jnp.asarray(np.concatenate([m, m], axis=0), BF16)


def _inproj(x2, g, shift, scale, w, w_up, b_gate, q_norm, k_norm, conv_w, conv_b, conv_ln_g, conv_ln_b, seq):
    N, D = x2.shape
    wup = jnp.pad(w_up, ((0, RANK_PAD - GLA_RANK), (0, QK_PAD - GLA_QK))).astype(BF16)
    bg = jnp.pad(b_gate, (0, QK_PAD - GLA_QK)).reshape(1, QK_PAD)
    tm = TOKEN_TILE
    tpb = seq // tm
    seg = _same_head(LANES, SB_DH)
    heads = lambda gn: jnp.tile(gn, SB_HEADS).reshape(1, SB_W)
    outs = ((QK_PAD, F32), (QK_PAD, F32), (QK_PAD, F32), (GLA_V, F32), (GLA_V, F32),
            (SB_W, BF16), (SB_W, BF16), (SB_W, BF16), (CONV_CH, BF16))
    chan = lambda a: a.reshape(1, CONV_CH)
    return pl.pallas_call(
        functools.partial(_inproj_kernel, tpb),
        out_shape=[jax.ShapeDtypeStruct((N, wd), dt) for wd, dt in outs],
        grid=(N // tm,),
        in_specs=[
            _row_spec(tm, D),
            _const_spec((1, D)),
            _batch_vec_spec(tpb, D), _batch_vec_spec(tpb, D),
            _const_spec(w.shape), _const_spec(wup.shape), _const_spec(bg.shape),
            _const_spec(seg.shape), _const_spec((1, SB_W)), _const_spec((1, SB_W)),
            _const_spec(conv_w.shape), _const_spec((1, CONV_CH)), _const_spec((1, CONV_CH)), _const_spec((1, CONV_CH)),
        ],
        out_specs=[_row_spec(tm, wd) for wd, _ in outs],
        scratch_shapes=[pltpu.VMEM((tm + CONV_PAD, CONV_CH), F32),
                        pltpu.VMEM((SUBLANES - 1, tm + CONV_PAD, CONV_CH), F32)],
        compiler_params=pltpu.CompilerParams(
            dimension_semantics=("arbitrary",), vmem_limit_bytes=VMEM_LIMIT),
        name="inproj",
    )(x2, g.reshape(1, D), shift, scale, w, wup, bg, seg, heads(q_norm), heads(k_norm),
      conv_w, chan(conv_b), chan(conv_ln_g), chan(conv_ln_b))


_GLA_LEVELS = 6


def _gla_constants():
    C = CHUNK
    t = np.arange(C)[:, None]
    j = np.arange(C)[None, :]
    mats = [j <= t, j > t]
    masks = [t == j]
    for l in range(1, _GLA_LEVELS + 1):
        n, m = 1 << l, 1 << (l - 1)
        ref = t // n * n + m - 1
        right = (t % n) >= m
        mats.append((right & (j > ref) & (j <= t)) | ((~right) & (j > t) & (j <= ref)))
        masks.append(((t // n) == (j // n)) & right & ((j % n) < m))
    prefix = np.concatenate(mats, axis=0).astype(np.float32)
    prefix = np.concatenate([prefix, prefix], axis=1)
    masks = np.stack([np.tile(mk, (1, GLA_HEADS)) for mk in masks]).astype(np.float32)
    kl = np.arange(QK_PAD)[None, :] // GLA_DK
    vl = np.arange(GLA_V)[:, None] // GLA_DV
    state_mask = (kl == vl).astype(np.float32)
    return prefix, masks, state_mask


def _gla_kernel(q_ref, k_ref, a_ref, v_ref, g_ref, pre_ref, msk_ref, smask_ref, seg_ref, on_ref,
                o_ref, st_ref):
    C = CHUNK

    @pl.when(pl.program_id(1) == 0)
    def _():
        st_ref[...] = jnp.zeros_like(st_ref)

    klane = lax.broadcasted_iota(jnp.int32, (1, QK_PAD), 1) // GLA_DK
    vlane = lax.broadcasted_iota(jnp.int32, (1, GLA_V), 1) // GLA_DV
    n_grp = GLA_GROUP

    def group(i, carry):
        base = i * (n_grp * C)
        rows = [pl.ds(pl.multiple_of(base + j * C, C), C) for j in range(n_grp)]
        qs = [q_ref[0, r, :] * (GLA_DK ** -0.5) for r in rows]
        ks = [k_ref[0, r, :] for r in rows]
        vs = [v_ref[0, r, :] for r in rows]

        es = []
        for r in rows:
            es.append(_split_dot_left(pre_ref[...], a_ref[0, r, :]))

        atts = []
        for q, k, e in zip(qs, ks, es):
            att = None
            for l in range(_GLA_LEVELS + 1):
                if l == 0:
                    qt, kt = q, k
                else:
                    f = jnp.exp(e[(l + 1) * C:(l + 2) * C])
                    qt, kt = q * f, k * f
                kst = jnp.concatenate(
                    [jnp.where(klane == h, kt, 0.0) for h in range(GLA_HEADS)], axis=0).astype(BF16)
                term = _dot_nt(qt.astype(BF16), kst) * msk_ref[l]
                att = term if att is None else att + term
            atts.append(att)

        intra, upds, qbs, decs = [], [], [], []
        for q, k, v, e, att in zip(qs, ks, vs, es, atts):
            vst = jnp.concatenate(
                [jnp.where(vlane == h, v, 0.0) for h in range(GLA_HEADS)], axis=0).astype(BF16)
            intra.append(_dot(att.astype(BF16), vst))
            kdec = (k * jnp.exp(e[C:2 * C])).astype(BF16)
            upds.append(_dot_tn(v.astype(BF16), kdec) * smask_ref[...])
            qbs.append((q * jnp.exp(e[0:C])).astype(BF16))
            decs.append(jnp.exp(e[C - 1:C]))

        st = st_ref[...]
        outs = []
        for o_intra, upd, qb, dec in zip(intra, upds, qbs, decs):
            outs.append(o_intra + _dot_nt(qb, st.astype(BF16)))
            st = st * dec + upd
        st_ref[...] = st

        o = jnp.concatenate(outs, axis=0)
        out_rows = pl.ds(pl.multiple_of(base, n_grp * C), n_grp * C)
        y = _head_rms(o, seg_ref[...], on_ref[...], GLA_DV)
        o_ref[0, out_rows, :] = (y * _silu(g_ref[0, out_rows, :])).astype(o_ref.dtype)
        return carry

    lax.fori_loop(0, q_ref.shape[1] // (n_grp * C), group, 0)


def _gla(gq, gk, ga, gv, gg, out_norm):
    B, T, _ = gq.shape
    prefix, masks, state_mask = _gla_constants()
    tt = GLA_TILE
    seq_spec = lambda w: pl.BlockSpec((1, tt, w), lambda b, i: (b, i, 0))
    consts = (jnp.asarray(prefix, BF16), jnp.asarray(masks), jnp.asarray(state_mask),
              _same_head(GLA_V, GLA_DV), jnp.tile(out_norm, GLA_HEADS).reshape(1, GLA_V))
    return pl.pallas_call(
        _gla_kernel,
        out_shape=jax.ShapeDtypeStruct((B, T, GLA_V), BF16),
        grid=(B, T // tt),
        in_specs=[seq_spec(QK_PAD), seq_spec(QK_PAD), seq_spec(QK_PAD), seq_spec(GLA_V), seq_spec(GLA_V)]
                 + [_const_spec(cst.shape) for cst in consts],
        out_specs=seq_spec(GLA_V),
        scratch_shapes=[pltpu.VMEM((GLA_V, QK_PAD), F32)],
        compiler_params=pltpu.CompilerParams(
            dimension_semantics=("parallel", "arbitrary"), vmem_limit_bytes=VMEM_LIMIT),
        name="gla",
    )(gq, gk, ga, gv, gg, *consts)


def _sb_kernel(q_ref, k_ref, v_ref, og_ref, seg_ref, causal_ref, suf_ref, o_ref,
               qs_ref, kn_ref, v0_ref, v1_ref, acc_ref, carry_ref, alive_ref):
    T = q_ref.shape[1]
    BLK, G = SB_BLOCK, SB_GROUP
    n_q = T // BLK
    n_groups = n_q // G
    first = lax.broadcasted_iota(jnp.int32, (1, 2 * SB_DH), 1) < SB_DH

    kn_ref[0:BLK, :] = jnp.zeros((BLK, 2 * SB_DH), BF16)
    kn_ref[BLK:, :] = k_ref[0]
    v0_ref[0:BLK, :] = jnp.zeros((BLK, SB_DH), BF16)
    v1_ref[0:BLK, :] = jnp.zeros((BLK, SB_DH), BF16)
    v0_ref[BLK:, :] = v_ref[0, :, 0:SB_DH]
    v1_ref[BLK:, :] = v_ref[0, :, SB_DH:]

    for qi in range(n_q):
        q2 = q_ref[0, qi * BLK:(qi + 1) * BLK, :]
        zq = jnp.zeros_like(q2)
        qs_ref[qi, 0:BLK, :] = jnp.where(first, q2, zq)
        qs_ref[qi, BLK:, :] = jnp.where(first, zq, q2)

    def group_step(grp, dist, causal):
        qis = [grp * G + g for g in range(G)]
        krows = [pl.ds(pl.multiple_of(jnp.maximum(qi - dist + 1, 0) * BLK, BLK), BLK) for qi in qis]
        zs = [_dot_nt(kn_ref[kr, :], qs_ref[qi]) for qi, kr in zip(qis, krows)]
        drops = []
        for z in zs:
            sp = jnp.maximum(z, 0.0) + jnp.log(1.0 + jnp.exp2(jnp.abs(z) * (-LOG2E)))
            drops.append(sp * causal_ref[...] if causal else sp)
        sufs = [_split_dot_left(suf_ref[...], d) for d in drops]
        ws, alive = [], None
        for qi, z, suf in zip(qis, zs, sufs):
            carry = jnp.zeros((1, 2 * BLK), F32) if causal else carry_ref[qi]
            w = jnp.exp(z - suf + carry)
            ws.append((w * causal_ref[...] if causal else w).astype(BF16))
            carry = carry - suf[0:1, :]
            carry_ref[qi] = carry
            alive = carry if alive is None else jnp.maximum(alive, carry)
        for qi, kr, w in zip(qis, krows, ws):
            for h, v_ref_h in enumerate((v0_ref, v1_ref)):
                upd = _dot_tn(v_ref_h[kr, :], w[:, h * BLK:(h + 1) * BLK])
                out = (qi, slice(h * SB_DH, (h + 1) * SB_DH), slice(None))
                acc_ref[out] = upd if causal else acc_ref[out] + upd
        return jnp.max(alive) > -SB_SKIP

    def diagonal(grp, c0):
        alive_ref[grp] = group_step(grp, 0, True).astype(jnp.int32)
        return c0

    lax.fori_loop(0, n_groups, diagonal, 0)

    def any_alive():
        total = alive_ref[0]
        for g in range(1, n_groups):
            total = total + alive_ref[g]
        return total > 0

    def sweep(state):
        dist, _ = state

        def visit(grp, c0):
            run = jnp.logical_and(alive_ref[grp] > 0, grp * G + G - 1 >= dist)

            @pl.when(run)
            def _():
                alive_ref[grp] = group_step(grp, dist, False).astype(jnp.int32)

            @pl.when(jnp.logical_not(run))
            def _():
                alive_ref[grp] = 0

            return c0

        lax.fori_loop(0, n_groups, visit, 0)
        return dist + 1, any_alive()

    lax.while_loop(lambda s: jnp.logical_and(s[0] < n_q, s[1]), sweep, (1, any_alive()))

    def finish(grp, c0):
        cols = pl.ds(pl.multiple_of(grp * G * BLK, G * BLK), G * BLK)
        ot = jnp.concatenate([acc_ref[grp * G + g] for g in range(G)], axis=1)
        o_ref[0, cols, :] = _head_rms(ot.T, seg_ref[...], og_ref[...], SB_DH).astype(o_ref.dtype)
        return c0

    lax.fori_loop(0, n_groups, finish, 0)


def _sb_constants():
    BLK = SB_BLOCK
    s = np.arange(BLK)[:, None]
    j = np.arange(BLK)[None, :]
    suffix = (j >= s).astype(np.float32)
    suffix = np.concatenate([suffix, suffix], axis=1)
    causal =(s < np.tile(np.arange(BLK), 2)[None, :]).astype(np.float32)
    return causal, suffix


def _sb(sq, sk, sv, out_norm):
    B, T, _ = sq.shape
    causal, suffix = _sb_constants()
    consts = (jnp.tile(out_norm, 2).reshape(1, 2 * SB_DH), _same_head(2 * SB_DH, SB_DH),
              jnp.asarray(causal), jnp.asarray(suffix, BF16))
    spec = pl.BlockSpec((1, T, 2 * SB_DH), lambda b, p: (b, 0, p))
    assert T % (SB_GROUP * SB_BLOCK) == 0, (T, SB_GROUP, SB_BLOCK)
    n_q = T // SB_BLOCK
    return pl.pallas_call(
        _sb_kernel,
        out_shape=jax.ShapeDtypeStruct((B, T, SB_W), BF16),
        grid=(B, SB_HEADS // 2),
        in_specs=[spec, spec, spec] + [_const_spec(cst.shape) for cst in consts],
        out_specs=spec,
        scratch_shapes=[pltpu.VMEM((n_q, 2 * SB_BLOCK, 2 * SB_DH), BF16),
                        pltpu.VMEM((T + SB_BLOCK, 2 * SB_DH), BF16),
                        pltpu.VMEM((T + SB_BLOCK, SB_DH), BF16),
                        pltpu.VMEM((T + SB_BLOCK, SB_DH), BF16),
                        pltpu.VMEM((n_q, 2 * SB_DH, SB_BLOCK), F32),
                        pltpu.VMEM((n_q, 1, 2 * SB_BLOCK), F32),
                        pltpu.SMEM((n_q // SB_GROUP,), jnp.int32)],
        compiler_params=pltpu.CompilerParams(
            dimension_semantics=("parallel", "parallel"), vmem_limit_bytes=VMEM_LIMIT),
        name="stickbreak",
    )(sq, sk, sv, *consts)


def kernel(x, c, w_ada, b_ada, norm_ffn1, ffn1_w_in, ffn1_w_out, norm_mix, w_in, w_out, gla_w_gate_up, gla_b_gate, gla_out_norm, sb_q_norm, sb_k_norm, sb_out_norm, conv_w, conv_b, conv_ln_g, conv_ln_b, norm_ffn2, ffn2_w_in, ffn2_w_out):
    B, T, D = x.shape
    L = w_ada.shape[0]
    mod = _adaln(c, w_ada, b_ada).reshape(L, B, N_MOD, 1, D)
    x2 = x.reshape(B * T, D)
    ffn_w = (ffn1_w_in[0].astype(BF16), ffn1_w_out[0].astype(BF16))
    for l in range(L):
        sh1, sc1, gt1, sh2, sc2, gt2, sh3, sc3, gt3 = (mod[l, :, i] for i in range(N_MOD))
        x2, *ffn_w, w_mix, w_proj = _ffn(
            x2, norm_ffn1[l], sh1, sc1, gt1, *ffn_w, T,
            to_round=((ffn2_w_in, l, None), (ffn2_w_out, l, None), (w_out, l, None),
                      (w_in, l, (_PROJ_GROUPS, PROJ_W))))
        gq, gk, ga, gv, gg, sq, sk, sv, oc = _inproj(
            x2, norm_mix[l], sh2, sc2, w_proj, gla_w_gate_up[l], gla_b_gate[l], sb_q_norm[l], sb_k_norm[l],
            conv_w[l], conv_b[l], conv_ln_g[l], conv_ln_b[l], T)
        seq = lambda a: a.reshape(B, T, a.shape[-1])
        flat = lambda a: a.reshape(B * T, a.shape[-1])
        oa = _gla(seq(gq), seq(gk), seq(ga), seq(gv), seq(gg), gla_out_norm[l])
        ob = _sb(seq(sq), seq(sk), seq(sv), sb_out_norm[l])
        mix = (flat(oa), flat(ob), oc, gt2, w_mix)
        nxt = ((ffn1_w_in, l + 1, None), (ffn1_w_out, l + 1, None)) if l + 1 < L else ()
        x2, *ffn_w = _ffn(x2, norm_ffn2[l], sh3, sc3, gt3, *ffn_w, T, mix=mix, to_round=nxt)
    return x2.reshape(B, T, D)
```

```python
import functools

import numpy as np
import jax
import jax.numpy as jnp
from jax import lax
from jax.experimental import pallas as pl
from jax.experimental.pallas import tpu as pltpu

F32 = jnp.float32
BF16 = jnp.bfloat16

EPS = 1e-6
LOG2E = 1.4426950408889634
CHUNK = 64
GLA_HEADS, GLA_DK, GLA_DV, GLA_RANK, GLA_TAU = 4, 48, 96, 16, 16.0
SB_HEADS, SB_DH = 6, 64
CONV_CH, CONV_WIDTH = 256, 31
D_FF = 2816
N_MOD = 9
GLA_QK = GLA_HEADS * GLA_DK
GLA_V = GLA_HEADS * GLA_DV
SB_W = SB_HEADS * SB_DH

LANES = 128
SUBLANES = 8
BF16_ROWS = 16
QK_PAD = 256
RANK_PAD = 128
FF_CHUNK = 256
TOKEN_TILE = 512
SB_BLOCK = 128
SB_GROUP = 8
SB_SKIP = 104.0
GLA_TILE = 512
GLA_GROUP = 4
CONV_TILE = 256
ADALN_TILE = 1536
CONV_PAD = 32
VMEM_LIMIT = 52 * 1024 * 1024
FFN_TILE = 1024
FFN_VMEM_LIMIT = 58 * 1024 * 1024

_OFF_GQ, _OFF_GK, _OFF_GV, _OFF_GG, _OFF_GR = 0, 256, 512, 896, 1280
_OFF_SQ, _OFF_SK, _OFF_SV, _OFF_CA, _OFF_CG = 1408, 1792, 2176, 2560, 2816
PROJ_W = 3072
_PROJ_GROUPS = []
_src = 0
for _width, _dst in ((GLA_QK, _OFF_GQ), (GLA_QK, _OFF_GK), (GLA_V, _OFF_GV), (GLA_V, _OFF_GG), (GLA_RANK, _OFF_GR),
                     (SB_W, _OFF_SQ), (SB_W, _OFF_SK), (SB_W, _OFF_SV), (CONV_CH, _OFF_CA), (CONV_CH, _OFF_CG)):
    _PROJ_GROUPS.append((_src, _width, _dst))
    _src += _width
_PROJ_GROUPS = tuple(_PROJ_GROUPS)


def _dot(a, b):
    return jnp.dot(a, b, preferred_element_type=F32)


def _dot_nt(a, b):
    return lax.dot_general(a, b, (((1,), (1,)), ((), ())), preferred_element_type=F32)


def _dot_tn(a, b):
    return lax.dot_general(a, b, (((0,), (0,)), ((), ())), preferred_element_type=F32)


def _hi_lo(x, axis):
    hi = x.astype(BF16)
    lo = (x - hi.astype(F32)).astype(BF16)
    return jnp.concatenate([hi, lo], axis=axis)


def _split_dot(x, m2):
    return _dot(_hi_lo(x, 1), m2)


def _split_dot_left(m2, x):
    return _dot(m2, _hi_lo(x, 0))


def _sigmoid(x):
    return 1.0 / (1.0 + jnp.exp2(x * (-LOG2E)))


def _silu(x):
    return x * _sigmoid(x)


def _log_sigmoid(x):
    return jnp.minimum(x, 0.0) - jnp.log(1.0 + jnp.exp2(jnp.abs(x) * (-LOG2E)))


def _modulated_norm(x, g, scale, shift):
    ms = jnp.mean(x * x, axis=-1, keepdims=True)
    return (x * lax.rsqrt(ms + EPS) * g) * (1.0 + scale) + shift


def _adaln_kernel(c_ref, w_ref, b_ref, o_ref):
    ca = _silu(c_ref[...]).astype(BF16)
    o_ref[0] = _dot(ca, w_ref[0].astype(BF16)) + b_ref[0]


def _adaln(c, w_ada, b_ada):
    L, D, W = w_ada.shape
    B = c.shape[0]
    tn = ADALN_TILE
    assert W % tn == 0, (W, tn)
    return pl.pallas_call(
        _adaln_kernel,
        out_shape=jax.ShapeDtypeStruct((L, B, W), F32),
        grid=(L, W // tn),
        in_specs=[
            pl.BlockSpec((B, D), lambda l, j: (0, 0)),
            pl.BlockSpec((1, D, tn), lambda l, j: (l, 0, j)),
            pl.BlockSpec((1, 1, tn), lambda l, j: (l, 0, j)),
        ],
        out_specs=pl.BlockSpec((1, B, tn), lambda l, j: (l, 0, j)),
        compiler_params=pltpu.CompilerParams(
            dimension_semantics=("parallel", "parallel"), vmem_limit_bytes=VMEM_LIMIT),
        name="adaln",
    )(c, w_ada, b_ada.reshape(L, 1, W))


def _ffn_kernel(has_mix, regroups, *refs):
    n_round = len(regroups)
    refs = list(refs)
    x_ref, g_ref, sh_ref, sc_ref, gt_ref, win_ref, wout_ref = refs[:7]
    del refs[:7]
    if has_mix:
        a_ref, b_ref, c_ref, gt2_ref, wmix_ref = refs[:5]
        del refs[:5]
    slabs = refs[:n_round]
    o_ref = refs[n_round]
    rounded = refs[n_round + 1:2 * n_round + 1]
    acc_ref = refs[2 * n_round + 1]
    for src_ref, dst_ref, groups in zip(slabs, rounded, regroups):
        if groups is None:
            dst_ref[...] = src_ref[...].astype(BF16)
        else:
            dst_ref[...] = jnp.zeros_like(dst_ref)
            for src_col, width, dst_col in groups:
                dst_ref[:, dst_col:dst_col + width] = src_ref[:, src_col:src_col + width].astype(BF16)
    x = x_ref[...]
    if has_mix:
        abc = jnp.concatenate([a_ref[...], b_ref[...], c_ref[...]], axis=1)
        x = x + gt2_ref[0] * _dot(abc, wmix_ref[...])
    hb = _modulated_norm(x, g_ref[...], sc_ref[0], sh_ref[0]).astype(BF16)
    d_ff = wout_ref.shape[0]
    for j in range(d_ff // FF_CHUNK):
        cols = slice(j * FF_CHUNK, (j + 1) * FF_CHUNK)
        a = _dot(hb, win_ref[:, cols])
        b = _dot(hb, win_ref[:, d_ff + j * FF_CHUNK:d_ff + (j + 1) * FF_CHUNK])
        y = _dot((_silu(a) * b).astype(BF16), wout_ref[cols, :])
        if j == 0:
            acc_ref[...] = y
        else:
            acc_ref[...] += y
    o_ref[...] = x + (0.5 * gt_ref[0]) * acc_ref[...]


def _const_spec(shape):
    nd = len(shape)
    return pl.BlockSpec(shape, lambda *_: (0,) * nd, pipeline_mode=pl.Buffered(1))


def _row_spec(tm, w):
    return pl.BlockSpec((tm, w), lambda i: (i, 0))


def _batch_vec_spec(tiles_per_batch, w):
    return pl.BlockSpec((1, 1, w), lambda i: (i // tiles_per_batch, 0, 0))


def _ffn(x2, g, shift, scale, gate, win, wout, seq, mix=None, to_round=()):
    N, D = x2.shape
    tm = FFN_TILE
    tpb = seq // tm
    steps = N // tm
    in_specs = [
        _row_spec(tm, D),
        _const_spec((1, D)),
        _batch_vec_spec(tpb, D), _batch_vec_spec(tpb, D), _batch_vec_spec(tpb, D),
        _const_spec(win.shape), _const_spec(wout.shape),
    ]
    out_shape = [jax.ShapeDtypeStruct((N, D), F32)]
    out_specs = [_row_spec(tm, D)]
    args = [x2, g.reshape(1, D), shift, scale, gate, win, wout]
    if mix is not None:
        a, b, c, gate2, w_mix = mix
        in_specs += [_row_spec(tm, t.shape[1]) for t in (a, b, c)] + [_batch_vec_spec(tpb, D)]
        in_specs += [_const_spec(w_mix.shape)]
        args += [a, b, c, gate2, w_mix]
    for stack, layer, regroup in to_round:
        _, rows, cols = stack.shape
        out_cols = cols if regroup is None else regroup[1]
        if rows % steps == 0 and (rows // steps) % BF16_ROWS == 0:
            slab, last = rows // steps, steps - 1
        else:
            slab = 2 * LANES
            last = rows // slab - 1
            assert rows % slab == 0 and last < steps, (rows, steps)
        in_specs.append(pl.BlockSpec((None, slab, cols),
                                     lambda i, layer=layer, last=last: (layer, jnp.minimum(i, last), 0)))
        out_specs.append(pl.BlockSpec((slab, out_cols), lambda i, last=last: (jnp.minimum(i, last), 0)))
        out_shape.append(jax.ShapeDtypeStruct((rows, out_cols), BF16))
        args.append(stack)
    regroups = tuple(None if r is None else r[0] for _, _, r in to_round)
    outs = pl.pallas_call(
        functools.partial(_ffn_kernel, mix is not None, regroups),
        out_shape=out_shape,
        grid=(steps,),
        in_specs=in_specs,
        out_specs=out_specs,
        scratch_shapes=[pltpu.VMEM((tm, D), F32)],
        compiler_params=pltpu.CompilerParams(
            dimension_semantics=("arbitrary",), vmem_limit_bytes=FFN_VMEM_LIMIT),
        name="ffn",
    )(*args)
    return tuple(outs)


def _head_rms(x, seg2, gain, head_dim):
    ms = _split_dot(x * x, seg2) * (1.0 / head_dim)
    return x * lax.rsqrt(ms + EPS) * gain


def _conv_module(pad_ref, sh_ref, w_ref, b_ref, lg_ref, lb_ref, o_ref):
    tm = pad_ref.shape[0] - CONV_PAD
    S = SUBLANES
    for r in range(1, S):
        sh_ref[r - 1, S:, :] = pad_ref[S - r:tm + CONV_PAD - r, :]
    for i in range(tm // CONV_TILE):
        acc = jnp.zeros((CONV_TILE, CONV_CH), F32)
        for j in range(CONV_WIDTH):
            a, r = divmod(CONV_WIDTH - 1 - j, S)
            start = i * CONV_TILE + CONV_PAD - S * a
            src = pad_ref[start:start + CONV_TILE, :] if r == 0 else sh_ref[r - 1, start:start + CONV_TILE, :]
            acc = acc + w_ref[j:j + 1, :] * src
        acc = acc + b_ref[...]
        mu = jnp.mean(acc, axis=-1, keepdims=True)
        xc = acc - mu
        var = jnp.mean(xc * xc, axis=-1, keepdims=True)
        y = xc * lax.rsqrt(var + EPS) * lg_ref[...] + lb_ref[...]
        o_ref[i * CONV_TILE:(i + 1) * CONV_TILE, :] = _silu(y).astype(o_ref.dtype)


def _inproj_kernel(tiles_per_batch, x_ref, g_ref, sh_ref, sc_ref, w_ref, wup_ref, bg_ref, seg_ref, qn_ref, kn_ref,
                   cw_ref, cb_ref, lg_ref, lb_ref,
                   gq_ref, gk_ref, ga_ref, gv_ref, gg_ref, sq_ref, sk_ref, sv_ref, oc_ref, pad_ref, shift_ref):
    hb = _modulated_norm(x_ref[...], g_ref[...], sc_ref[0], sh_ref[0]).astype(BF16)
    pc = _dot(hb, w_ref[:, _OFF_CA:])
    tm = x_ref.shape[0]
    tail = pad_ref[tm:tm + CONV_PAD, :]
    pad_ref[0:CONV_PAD, :] = jnp.where(pl.program_id(0) % tiles_per_batch == 0, 0.0, tail)
    pad_ref[CONV_PAD:, :] = pc[:, :CONV_CH] * _sigmoid(pc[:, CONV_CH:])
    _conv_module(pad_ref, shift_ref, cw_ref, cb_ref, lg_ref, lb_ref, oc_ref)

    p = _dot(hb, w_ref[:, :_OFF_CA])
    gq_ref[...] = p[:, _OFF_GQ:_OFF_GQ + QK_PAD]
    gk_ref[...] = p[:, _OFF_GK:_OFF_GK + QK_PAD]
    gv_ref[...] = p[:, _OFF_GV:_OFF_GV + GLA_V]
    gg_ref[...] = p[:, _OFF_GG:_OFF_GG + GLA_V]
    r = p[:, _OFF_GR:_OFF_GR + RANK_PAD].astype(BF16)
    ga_ref[...] = _log_sigmoid(_dot(r, wup_ref[...]) + bg_ref[...]) * (1.0 / GLA_TAU)
    seg = seg_ref[...]
    for t in range(SB_W // LANES):
        lanes = slice(t * LANES, (t + 1) * LANES)
        sq = _head_rms(p[:, _OFF_SQ + t * LANES:_OFF_SQ + (t + 1) * LANES], seg, qn_ref[:, lanes], SB_DH)
        sq_ref[:, lanes] = (sq * (SB_DH ** -0.5)).astype(BF16)
        sk = _head_rms(p[:, _OFF_SK + t * LANES:_OFF_SK + (t + 1) * LANES], seg, kn_ref[:, lanes], SB_DH)
        sk_ref[:, lanes] = sk.astype(BF16)
    sv_ref[...] = p[:, _OFF_SV:_OFF_SV + SB_W].astype(BF16)


def _same_head(width, head_dim):
    h = np.arange(width) // head_dim
    m = (h[:, None] == h[None, :]).astype(np.float32)
    return jnp.asarray(np.concatenate([m, m], axis=0), BF16)


def _inproj(x2, g, shift, scale, w, w_up, b_gate, q_norm, k_norm, conv_w, conv_b, conv_ln_g, conv_ln_b, seq):
    N, D = x2.shape
    wup = jnp.pad(w_up, ((0, RANK_PAD - GLA_RANK), (0, QK_PAD - GLA_QK))).astype(BF16)
    bg = jnp.pad(b_gate, (0, QK_PAD - GLA_QK)).reshape(1, QK_PAD)
    tm = TOKEN_TILE
    tpb = seq // tm
    seg = _same_head(LANES, SB_DH)
    heads = lambda gn: jnp.tile(gn, SB_HEADS).reshape(1, SB_W)
    outs = ((QK_PAD, F32), (QK_PAD, F32), (QK_PAD, F32), (GLA_V, F32), (GLA_V, F32),
            (SB_W, BF16), (SB_W, BF16), (SB_W, BF16), (CONV_CH, BF16))
    chan = lambda a: a.reshape(1, CONV_CH)
    return pl.pallas_call(
        functools.partial(_inproj_kernel, tpb),
        out_shape=[jax.ShapeDtypeStruct((N, wd), dt) for wd, dt in outs],
        grid=(N // tm,),
        in_specs=[
            _row_spec(tm, D),
            _const_spec((1, D)),
            _batch_vec_spec(tpb, D), _batch_vec_spec(tpb, D),
            _const_spec(w.shape), _const_spec(wup.shape), _const_spec(bg.shape),
            _const_spec(seg.shape), _const_spec((1, SB_W)), _const_spec((1, SB_W)),
            _const_spec(conv_w.shape), _const_spec((1, CONV_CH)), _const_spec((1, CONV_CH)), _const_spec((1, CONV_CH)),
        ],
        out_specs=[_row_spec(tm, wd) for wd, _ in outs],
        scratch_shapes=[pltpu.VMEM((tm + CONV_PAD, CONV_CH), F32),
                        pltpu.VMEM((SUBLANES - 1, tm + CONV_PAD, CONV_CH), F32)],
        compiler_params=pltpu.CompilerParams(
            dimension_semantics=("arbitrary",), vmem_limit_bytes=VMEM_LIMIT),
        name="inproj",
    )(x2, g.reshape(1, D), shift, scale, w, wup, bg, seg, heads(q_norm), heads(k_norm),
      conv_w, chan(conv_b), chan(conv_ln_g), chan(conv_ln_b))


_GLA_LEVELS = 6


def _gla_constants():
    C = CHUNK
    t = np.arange(C)[:, None]
    j = np.arange(C)[None, :]
    mats = [j <= t, j > t]
    masks = [t == j]
    for l in range(1, _GLA_LEVELS + 1):
        n, m = 1 << l, 1 << (l - 1)
        ref = t // n * n + m - 1
        right = (t % n) >= m
        mats.append((right & (j > ref) & (j <= t)) | ((~right) & (j > t) & (j <= ref)))
        masks.append(((t // n) == (j // n)) & right & ((j % n) < m))
    prefix = np.concatenate(mats, axis=0).astype(np.float32)
    prefix = np.concatenate([prefix, prefix], axis=1)
    masks = np.stack([np.tile(mk, (1, GLA_HEADS)) for mk in masks]).astype(np.float32)
    kl = np.arange(QK_PAD)[None, :] // GLA_DK
    vl = np.arange(GLA_V)[:, None] // GLA_DV
    state_mask = (kl == vl).astype(np.float32)
    return prefix, masks, state_mask


def _gla_kernel(q_ref, k_ref, a_ref, v_ref, g_ref, pre_ref, msk_ref, smask_ref, seg_ref, on_ref,
                o_ref, st_ref):
    C = CHUNK

    @pl.when(pl.program_id(1) == 0)
    def _():
        st_ref[...] = jnp.zeros_like(st_ref)

    klane = lax.broadcasted_iota(jnp.int32, (1, QK_PAD), 1) // GLA_DK
    vlane = lax.broadcasted_iota(jnp.int32, (1, GLA_V), 1) // GLA_DV
    n_grp = GLA_GROUP

    def group(i, carry):
        base = i * (n_grp * C)
        rows = [pl.ds(pl.multiple_of(base + j * C, C), C) for j in range(n_grp)]
        qs = [q_ref[0, r, :] * (GLA_DK ** -0.5) for r in rows]
        ks = [k_ref[0, r, :] for r in rows]
        vs = [v_ref[0, r, :] for r in rows]

        es = []
        for r in rows:
            es.append(_split_dot_left(pre_ref[...], a_ref[0, r, :]))

        atts = []
        for q, k, e in zip(qs, ks, es):
            att = None
            for l in range(_GLA_LEVELS + 1):
                if l == 0:
                    qt, kt = q, k
                else:
                    f = jnp.exp(e[(l + 1) * C:(l + 2) * C])
                    qt, kt = q * f, k * f
                kst = jnp.concatenate(
                    [jnp.where(klane == h, kt, 0.0) for h in range(GLA_HEADS)], axis=0).astype(BF16)
                term = _dot_nt(qt.astype(BF16), kst) * msk_ref[l]
                att = term if att is None else att + term
            atts.append(att)

        intra, upds, qbs, decs = [], [], [], []
        for q, k, v, e, att in zip(qs, ks, vs, es, atts):
            vst = jnp.concatenate(
                [jnp.where(vlane == h, v, 0.0) for h in range(GLA_HEADS)], axis=0).astype(BF16)
            intra.append(_dot(att.astype(BF16), vst))
            kdec = (k * jnp.exp(e[C:2 * C])).astype(BF16)
            upds.append(_dot_tn(v.astype(BF16), kdec) * smask_ref[...])
            qbs.append((q * jnp.exp(e[0:C])).astype(BF16))
            decs.append(jnp.exp(e[C - 1:C]))

        st = st_ref[...]
        outs = []
        for o_intra, upd, qb, dec in zip(intra, upds, qbs, decs):
            outs.append(o_intra + _dot_nt(qb, st.astype(BF16)))
            st = st * dec + upd
        st_ref[...] = st

        o = jnp.concatenate(outs, axis=0)
        out_rows = pl.ds(pl.multiple_of(base, n_grp * C), n_grp * C)
        y = _head_rms(o, seg_ref[...], on_ref[...], GLA_DV)
        o_ref[0, out_rows, :] = (y * _silu(g_ref[0, out_rows, :])).astype(o_ref.dtype)
        return carry

    lax.fori_loop(0, q_ref.shape[1] // (n_grp * C), group, 0)


def _gla(gq, gk, ga, gv, gg, out_norm):
    B, T, _ = gq.shape
    prefix, masks, state_mask = _gla_constants()
    tt = GLA_TILE
    seq_spec = lambda w: pl.BlockSpec((1, tt, w), lambda b, i: (b, i, 0))
    consts = (jnp.asarray(prefix, BF16), jnp.asarray(masks), jnp.asarray(state_mask),
              _same_head(GLA_V, GLA_DV), jnp.tile(out_norm, GLA_HEADS).reshape(1, GLA_V))
    return pl.pallas_call(
        _gla_kernel,
        out_shape=jax.ShapeDtypeStruct((B, T, GLA_V), BF16),
        grid=(B, T // tt),
        in_specs=[seq_spec(QK_PAD), seq_spec(QK_PAD), seq_spec(QK_PAD), seq_spec(GLA_V), seq_spec(GLA_V)]
                 + [_const_spec(cst.shape) for cst in consts],
        out_specs=seq_spec(GLA_V),
        scratch_shapes=[pltpu.VMEM((GLA_V, QK_PAD), F32)],
        compiler_params=pltpu.CompilerParams(
            dimension_semantics=("parallel", "arbitrary"), vmem_limit_bytes=VMEM_LIMIT),
        name="gla",
    )(gq, gk, ga, gv, gg, *consts)


def _sb_kernel(q_ref, k_ref, v_ref, og_ref, seg_ref, causal_ref, suf_ref, o_ref,
               qs_ref, kn_ref, v0_ref, v1_ref, acc_ref, carry_ref, alive_ref):
    T = q_ref.shape[1]
    BLK, G = SB_BLOCK, SB_GROUP
    n_q = T // BLK
    n_groups = n_q // G
    first = lax.broadcasted_iota(jnp.int32, (1, 2 * SB_DH), 1) < SB_DH

    kn_ref[0:BLK, :] = jnp.zeros((BLK, 2 * SB_DH), BF16)
    kn_ref[BLK:, :] = k_ref[0]
    v0_ref[0:BLK, :] = jnp.zeros((BLK, SB_DH), BF16)
    v1_ref[0:BLK, :] = jnp.zeros((BLK, SB_DH), BF16)
    v0_ref[BLK:, :] = v_ref[0, :, 0:SB_DH]
    v1_ref[BLK:, :] = v_ref[0, :, SB_DH:]

    for qi in range(n_q):
        q2 = q_ref[0, qi * BLK:(qi + 1) * BLK, :]
        zq = jnp.zeros_like(q2)
        qs_ref[qi, 0:BLK, :] = jnp.where(first, q2, zq)
        qs_ref[qi, BLK:, :] = jnp.where(first, zq, q2)

    def group_step(grp, dist, causal):
        qis = [grp * G + g for g in range(G)]
        krows = [pl.ds(pl.multiple_of(jnp.maximum(qi - dist + 1, 0) * BLK, BLK), BLK) for qi in qis]
        zs = [_dot_nt(kn_ref[kr, :], qs_ref[qi]) for qi, kr in zip(qis, krows)]
        drops = []
        for z in zs:
            sp = jnp.maximum(z, 0.0) + jnp.log(1.0 + jnp.exp2(jnp.abs(z) * (-LOG2E)))
            drops.append(sp * causal_ref[...] if causal else sp)
        sufs = [_split_dot_left(suf_ref[...], d) for d in drops]
        ws, alive = [], None
        for qi, z, suf in zip(qis, zs, sufs):
            carry = jnp.zeros((1, 2 * BLK), F32) if causal else carry_ref[qi]
            w = jnp.exp(z - suf + carry)
            ws.append((w * causal_ref[...] if causal else w).astype(BF16))
            carry = carry - suf[0:1, :]
            carry_ref[qi] = carry
            alive = carry if alive is None else jnp.maximum(alive, carry)
        for qi, kr, w in zip(qis, krows, ws):
            for h, v_ref_h in enumerate((v0_ref, v1_ref)):
                upd = _dot_tn(v_ref_h[kr, :], w[:, h * BLK:(h + 1) * BLK])
                out = (qi, slice(h * SB_DH, (h + 1) * SB_DH), slice(None))
                acc_ref[out] = upd if causal else acc_ref[out] + upd
        return jnp.max(alive) > -SB_SKIP

    def diagonal(grp, c0):
        alive_ref[grp] = group_step(grp, 0, True).astype(jnp.int32)
        return c0

    lax.fori_loop(0, n_groups, diagonal, 0)

    def any_alive():
        total = alive_ref[0]
        for g in range(1, n_groups):
            total = total + alive_ref[g]
        return total > 0

    def sweep(state):
        dist, _ = state

        def visit(grp, c0):
            run = jnp.logical_and(alive_ref[grp] > 0, grp * G + G - 1 >= dist)

            @pl.when(run)
            def _():
                alive_ref[grp] = group_step(grp, dist, False).astype(jnp.int32)

            @pl.when(jnp.logical_not(run))
            def _():
                alive_ref[grp] = 0

            return c0

        lax.fori_loop(0, n_groups, visit, 0)
        return dist + 1, any_alive()

    lax.while_loop(lambda s: jnp.logical_and(s[0] < n_q, s[1]), sweep, (1, any_alive()))

    def finish(grp, c0):
        cols = pl.ds(pl.multiple_of(grp * G * BLK, G * BLK), G * BLK)
        ot = jnp.concatenate([acc_ref[grp * G + g] for g in range(G)], axis=1)
        o_ref[0, cols, :] = _head_rms(ot.T, seg_ref[...], og_ref[...], SB_DH).astype(o_ref.dtype)
        return c0

    lax.fori_loop(0, n_groups, finish, 0)


def _sb_constants():
    BLK = SB_BLOCK
    s = np.arange(BLK)[:, None]
    j = np.arange(BLK)[None, :]
    suffix = (j >= s).astype(np.float32)
    suffix = np.concatenate([suffix, suffix], axis=1)
    causal =(s < np.tile(np.arange(BLK), 2)[None, :]).astype(np.float32)
    return causal, suffix


def _sb(sq, sk, sv, out_norm):
    B, T, _ = sq.shape
    causal, suffix = _sb_constants()
    consts = (jnp.tile(out_norm, 2).reshape(1, 2 * SB_DH), _same_head(2 * SB_DH, SB_DH),
              jnp.asarray(causal), jnp.asarray(suffix, BF16))
    spec = pl.BlockSpec((1, T, 2 * SB_DH), lambda b, p: (b, 0, p))
    assert T % (SB_GROUP * SB_BLOCK) == 0, (T, SB_GROUP, SB_BLOCK)
    n_q = T // SB_BLOCK
    return pl.pallas_call(
        _sb_kernel,
        out_shape=jax.ShapeDtypeStruct((B, T, SB_W), BF16),
        grid=(B, SB_HEADS // 2),
        in_specs=[spec, spec, spec] + [_const_spec(cst.shape) for cst in consts],
        out_specs=spec,
        scratch_shapes=[pltpu.VMEM((n_q, 2 * SB_BLOCK, 2 * SB_DH), BF16),
                        pltpu.VMEM((T + SB_BLOCK, 2 * SB_DH), BF16),
                        pltpu.VMEM((T + SB_BLOCK, SB_DH), BF16),
                        pltpu.VMEM((T + SB_BLOCK, SB_DH), BF16),
                        pltpu.VMEM((n_q, 2 * SB_DH, SB_BLOCK), F32),
                        pltpu.VMEM((n_q, 1, 2 * SB_BLOCK), F32),
                        pltpu.SMEM((n_q // SB_GROUP,), jnp.int32)],
        compiler_params=pltpu.CompilerParams(
            dimension_semantics=("parallel", "parallel"), vmem_limit_bytes=VMEM_LIMIT),
        name="stickbreak",
    )(sq, sk, sv, *consts)


def kernel(x, c, w_ada, b_ada, norm_ffn1, ffn1_w_in, ffn1_w_out, norm_mix, w_in, w_out, gla_w_gate_up, gla_b_gate, gla_out_norm, sb_q_norm, sb_k_norm, sb_out_norm, conv_w, conv_b, conv_ln_g, conv_ln_b, norm_ffn2, ffn2_w_in, ffn2_w_out):
    B, T, D = x.shape
    L = w_ada.shape[0]
    mod = _adaln(c, w_ada, b_ada).reshape(L, B, N_MOD, 1, D)
    x2 = x.reshape(B * T, D)
    ffn_w = (ffn1_w_in[0].astype(BF16), ffn1_w_out[0].astype(BF16))
    for l in range(L):
        sh1, sc1, gt1, sh2, sc2, gt2, sh3, sc3, gt3 = (mod[l, :, i] for i in range(N_MOD))
        x2, *ffn_w, w_mix, w_proj = _ffn(
            x2, norm_ffn1[l], sh1, sc1, gt1, *ffn_w, T,
            to_round=((ffn2_w_in, l, None), (ffn2_w_out, l, None), (w_out, l, None),
                      (w_in, l, (_PROJ_GROUPS, PROJ_W))))
        gq, gk, ga, gv, gg, sq, sk, sv, oc = _inproj(
            x2, norm_mix[l], sh2, sc2, w_proj, gla_w_gate_up[l], gla_b_gate[l], sb_q_norm[l], sb_k_norm[l],
            conv_w[l], conv_b[l], conv_ln_g[l], conv_ln_b[l], T)
        seq = lambda a: a.reshape(B, T, a.shape[-1])
        flat = lambda a: a.reshape(B * T, a.shape[-1])
        oa = _gla(seq(gq), seq(gk), seq(ga), seq(gv), seq(gg), gla_out_norm[l])
        ob = _sb(seq(sq), seq(sk), seq(sv), sb_out_norm[l])
        mix = (flat(oa), flat(ob), oc, gt2, w_mix)
        nxt = ((ffn1_w_in, l + 1, None), (ffn1_w_out, l + 1, None)) if l + 1 < L else ()
        x2, *ffn_w = _ffn(x2, norm_ffn2[l], sh3, sc3, gt3, *ffn_w, T, mix=mix, to_round=nxt)
    return x2.reshape(B, T, D)
```

```python
import functools

import numpy as np
import jax
import jax.numpy as jnp
from jax import lax
from jax.experimental import pallas as pl
from jax.experimental.pallas import tpu as pltpu

F32 = jnp.float32
BF16 = jnp.bfloat16

EPS = 1e-6
LOG2E = 1.4426950408889634
CHUNK = 64
GLA_HEADS, GLA_DK, GLA_DV, GLA_RANK, GLA_TAU = 4, 48, 96, 16, 16.0
SB_HEADS, SB_DH = 6, 64
CONV_CH, CONV_WIDTH = 256, 31
D_FF = 2816
N_MOD = 9
GLA_QK = GLA_HEADS * GLA_DK
GLA_V = GLA_HEADS * GLA_DV
SB_W = SB_HEADS * SB_DH

LANES = 128
SUBLANES = 8
BF16_ROWS = 16
QK_PAD = 256
RANK_PAD = 128
FF_CHUNK = 256
TOKEN_TILE = 512
SB_BLOCK = 128
SB_GROUP = 8
SB_SKIP = 104.0
SB_GONE = -1e30
GLA_TILE = 512
GLA_GROUP = 4
CONV_TILE = 256
ADALN_TILE = 1536
CONV_PAD = 32
VMEM_LIMIT = 52 * 1024 * 1024

_OFF_GQ, _OFF_GK, _OFF_GV, _OFF_GG, _OFF_GR = 0, 256, 512, 896, 1280
_OFF_SQ, _OFF_SK, _OFF_SV, _OFF_CA, _OFF_CG = 1408, 1792, 2176, 2560, 2816
PROJ_W = 3072
_PROJ_GROUPS = []
_src = 0
for _width, _dst in ((GLA_QK, _OFF_GQ), (GLA_QK, _OFF_GK), (GLA_V, _OFF_GV), (GLA_V, _OFF_GG), (GLA_RANK, _OFF_GR),
                     (SB_W, _OFF_SQ), (SB_W, _OFF_SK), (SB_W, _OFF_SV), (CONV_CH, _OFF_CA), (CONV_CH, _OFF_CG)):
    _PROJ_GROUPS.append((_src, _width, _dst))
    _src += _width
_PROJ_GROUPS = tuple(_PROJ_GROUPS)


def _dot(a, b):
    return jnp.dot(a, b, preferred_element_type=F32)


def _dot_nt(a, b):
    return lax.dot_general(a, b, (((1,), (1,)), ((), ())), preferred_element_type=F32)


def _dot_tn(a, b):
    return lax.dot_general(a, b, (((0,), (0,)), ((), ())), preferred_element_type=F32)


def _hi_lo(x, axis):
    hi = x.astype(BF16)
    lo = (x - hi.astype(F32)).astype(BF16)
    return jnp.concatenate([hi, lo], axis=axis)


def _split_dot(x, m2):
    return _dot(_hi_lo(x, 1), m2)


def _split_dot_left(m2, x):
    return _dot(m2, _hi_lo(x, 0))


def _sigmoid(x):
    return 1.0 / (1.0 + jnp.exp2(x * (-LOG2E)))


def _silu(x):
    return x * _sigmoid(x)


def _log_sigmoid(x):
    return jnp.minimum(x, 0.0) - jnp.log(1.0 + jnp.exp2(jnp.abs(x) * (-LOG2E)))


def _modulated_norm(x, g, scale, shift):
    ms = jnp.mean(x * x, axis=-1, keepdims=True)
    return (x * lax.rsqrt(ms + EPS) * g) * (1.0 + scale) + shift


def _adaln_kernel(c_ref, w_ref, b_ref, o_ref):
    ca = _silu(c_ref[...]).astype(BF16)
    o_ref[0] = _dot(ca, w_ref[0].astype(BF16)) + b_ref[0]


def _adaln(c, w_ada, b_ada):
    L, D, W = w_ada.shape
    B = c.shape[0]
    tn = ADALN_TILE
    assert W % tn == 0, (W, tn)
    return pl.pallas_call(
        _adaln_kernel,
        out_shape=jax.ShapeDtypeStruct((L, B, W), F32),
        grid=(L, W // tn),
        in_specs=[
            pl.BlockSpec((B, D), lambda l, j: (0, 0)),
            pl.BlockSpec((1, D, tn), lambda l, j: (l, 0, j)),
            pl.BlockSpec((1, 1, tn), lambda l, j: (l, 0, j)),
        ],
        out_specs=pl.BlockSpec((1, B, tn), lambda l, j: (l, 0, j)),
        compiler_params=pltpu.CompilerParams(
            dimension_semantics=("parallel", "parallel"), vmem_limit_bytes=VMEM_LIMIT),
        name="adaln",
    )(c, w_ada, b_ada.reshape(L, 1, W))


def _ffn_kernel(has_mix, regroups, *refs):
    n_round = len(regroups)
    refs = list(refs)
    x_ref, g_ref, sh_ref, sc_ref, gt_ref, win_ref, wout_ref = refs[:7]
    del refs[:7]
    if has_mix:
        a_ref, b_ref, c_ref, gt2_ref, wmix_ref = refs[:5]
        del refs[:5]
    slabs = refs[:n_round]
    o_ref = refs[n_round]
    rounded = refs[n_round + 1:2 * n_round + 1]
    acc_ref = refs[2 * n_round + 1]
    for src_ref, dst_ref, groups in zip(slabs, rounded, regroups):
        if groups is None:
            dst_ref[...] = src_ref[...].astype(BF16)
        else:
            dst_ref[...] = jnp.zeros_like(dst_ref)
            for src_col, width, dst_col in groups:
                dst_ref[:, dst_col:dst_col + width] = src_ref[:, src_col:src_col + width].astype(BF16)
    x = x_ref[...]
    if has_mix:
        abc = jnp.concatenate([a_ref[...], b_ref[...], c_ref[...]], axis=1)
        x = x + gt2_ref[0] * _dot(abc, wmix_ref[...])
    hb = _modulated_norm(x, g_ref[...], sc_ref[0], sh_ref[0]).astype(BF16)
    d_ff = wout_ref.shape[0]
    for j in range(d_ff // FF_CHUNK):
        cols = slice(j * FF_CHUNK, (j + 1) * FF_CHUNK)
        a = _dot(hb, win_ref[:, cols])
        b = _dot(hb, win_ref[:, d_ff + j * FF_CHUNK:d_ff + (j + 1) * FF_CHUNK])
        y = _dot((_silu(a) * b).astype(BF16), wout_ref[cols, :])
        if j == 0:
            acc_ref[...] = y
        else:
            acc_ref[...] += y
    o_ref[...] = x + (0.5 * gt_ref[0]) * acc_ref[...]


def _const_spec(shape):
    nd = len(shape)
    return pl.BlockSpec(shape, lambda *_: (0,) * nd, pipeline_mode=pl.Buffered(1))


def _row_spec(tm, w):
    return pl.BlockSpec((tm, w), lambda i: (i, 0))


def _batch_vec_spec(tiles_per_batch, w):
    return pl.BlockSpec((1, 1, w), lambda i: (i // tiles_per_batch, 0, 0))


def _ffn(x2, g, shift, scale, gate, win, wout, seq, mix=None, to_round=()):
    N, D = x2.shape
    tm = TOKEN_TILE
    tpb = seq // tm
    steps = N // tm
    in_specs = [
        _row_spec(tm, D),
        _const_spec((1, D)),
        _batch_vec_spec(tpb, D), _batch_vec_spec(tpb, D), _batch_vec_spec(tpb, D),
        _const_spec(win.shape), _const_spec(wout.shape),
    ]
    out_shape = [jax.ShapeDtypeStruct((N, D), F32)]
    out_specs = [_row_spec(tm, D)]
    args = [x2, g.reshape(1, D), shift, scale, gate, win, wout]
    if mix is not None:
        a, b, c, gate2, w_mix = mix
        in_specs += [_row_spec(tm, t.shape[1]) for t in (a, b, c)] + [_batch_vec_spec(tpb, D)]
        in_specs += [_const_spec(w_mix.shape)]
        args += [a, b, c, gate2, w_mix]
    for stack, layer, regroup in to_round:
        _, rows, cols = stack.shape
        out_cols = cols if regroup is None else regroup[1]
        if rows % steps == 0 and (rows // steps) % BF16_ROWS == 0:
            slab, last = rows // steps, steps - 1
        else:
            slab = LANES
            last = rows // slab - 1
            assert rows % slab == 0 and last < steps, (rows, steps)
        in_specs.append(pl.BlockSpec((None, slab, cols),
                                     lambda i, layer=layer, last=last: (layer, jnp.minimum(i, last), 0)))
        out_specs.append(pl.BlockSpec((slab, out_cols), lambda i, last=last: (jnp.minimum(i, last), 0)))
        out_shape.append(jax.ShapeDtypeStruct((rows, out_cols), BF16))
        args.append(stack)
    regroups = tuple(None if r is None else r[0] for _, _, r in to_round)
    outs = pl.pallas_call(
        functools.partial(_ffn_kernel, mix is not None, regroups),
        out_shape=out_shape,
        grid=(steps,),
        in_specs=in_specs,
        out_specs=out_specs,
        scratch_shapes=[pltpu.VMEM((tm, D), F32)],
        compiler_params=pltpu.CompilerParams(
            dimension_semantics=("arbitrary",), vmem_limit_bytes=VMEM_LIMIT),
        name="ffn",
    )(*args)
    return tuple(outs)


def _head_rms(x, seg2, gain, head_dim):
    ms = _split_dot(x * x, seg2) * (1.0 / head_dim)
    return x * lax.rsqrt(ms + EPS) * gain


def _conv_module(pad_ref, sh_ref, w_ref, b_ref, lg_ref, lb_ref, o_ref):
    tm = pad_ref.shape[0] - CONV_PAD
    S = SUBLANES
    for r in range(1, S):
        sh_ref[r - 1, S:, :] = pad_ref[S - r:tm + CONV_PAD - r, :]
    for i in range(tm // CONV_TILE):
        acc = jnp.zeros((CONV_TILE, CONV_CH), F32)
        for j in range(CONV_WIDTH):
            a, r = divmod(CONV_WIDTH - 1 - j, S)
            start = i * CONV_TILE + CONV_PAD - S * a
            src = pad_ref[start:start + CONV_TILE, :] if r == 0 else sh_ref[r - 1, start:start + CONV_TILE, :]
            acc = acc + w_ref[j:j + 1, :] * src
        acc = acc + b_ref[...]
        mu = jnp.mean(acc, axis=-1, keepdims=True)
        xc = acc - mu
        var = jnp.mean(xc * xc, axis=-1, keepdims=True)
        y = xc * lax.rsqrt(var + EPS) * lg_ref[...] + lb_ref[...]
        o_ref[i * CONV_TILE:(i + 1) * CONV_TILE, :] = _silu(y).astype(o_ref.dtype)


def _inproj_kernel(tiles_per_batch, x_ref, g_ref, sh_ref, sc_ref, w_ref, wup_ref, bg_ref, seg_ref, qn_ref, kn_ref,
                   cw_ref, cb_ref, lg_ref, lb_ref,
                   gq_ref, gk_ref, ga_ref, gv_ref, gg_ref, sq_ref, sk_ref, sv_ref, oc_ref, pad_ref, shift_ref):
    hb = _modulated_norm(x_ref[...], g_ref[...], sc_ref[0], sh_ref[0]).astype(BF16)
    pc = _dot(hb, w_ref[:, _OFF_CA:])
    tm = x_ref.shape[0]
    tail = pad_ref[tm:tm + CONV_PAD, :]
    pad_ref[0:CONV_PAD, :] = jnp.where(pl.program_id(0) % tiles_per_batch == 0, 0.0, tail)
    pad_ref[CONV_PAD:, :] = pc[:, :CONV_CH] * _sigmoid(pc[:, CONV_CH:])
    _conv_module(pad_ref, shift_ref, cw_ref, cb_ref, lg_ref, lb_ref, oc_ref)

    p = _dot(hb, w_ref[:, :_OFF_CA])
    gq_ref[...] = p[:, _OFF_GQ:_OFF_GQ + QK_PAD]
    gk_ref[...] = p[:, _OFF_GK:_OFF_GK + QK_PAD]
    gv_ref[...] = p[:, _OFF_GV:_OFF_GV + GLA_V]
    gg_ref[...] = p[:, _OFF_GG:_OFF_GG + GLA_V]
    r = p[:, _OFF_GR:_OFF_GR + RANK_PAD].astype(BF16)
    ga_ref[...] = _log_sigmoid(_dot(r, wup_ref[...]) + bg_ref[...]) * (1.0 / GLA_TAU)
    seg = seg_ref[...]
    for t in range(SB_W // LANES):
        lanes = slice(t * LANES, (t + 1) * LANES)
        sq = _head_rms(p[:, _OFF_SQ + t * LANES:_OFF_SQ + (t + 1) * LANES], seg, qn_ref[:, lanes], SB_DH)
        sq_ref[:, lanes] = (sq * (SB_DH ** -0.5)).astype(BF16)
        sk = _head_rms(p[:, _OFF_SK + t * LANES:_OFF_SK + (t + 1) * LANES], seg, kn_ref[:, lanes], SB_DH)
        sk_ref[:, lanes] = sk.astype(BF16)
    sv_ref[...] = p[:, _OFF_SV:_OFF_SV + SB_W].astype(BF16)


def _same_head(width, head_dim):
    h = np.arange(width) // head_dim
    m = (h[:, None] == h[None, :]).astype(np.float32)
    return jnp.asarray(np.concatenate([m, m], axis=0), BF16)


def _inproj(x2, g, shift, scale, w, w_up, b_gate, q_norm, k_norm, conv_w, conv_b, conv_ln_g, conv_ln_b, seq):
    N, D = x2.shape
    wup = jnp.pad(w_up, ((0, RANK_PAD - GLA_RANK), (0, QK_PAD - GLA_QK))).astype(BF16)
    bg = jnp.pad(b_gate, (0, QK_PAD - GLA_QK)).reshape(1, QK_PAD)
    tm = TOKEN_TILE
    tpb = seq // tm
    seg = _same_head(LANES, SB_DH)
    heads = lambda gn: jnp.tile(gn, SB_HEADS).reshape(1, SB_W)
    outs = ((QK_PAD, F32), (QK_PAD, F32), (QK_PAD, F32), (GLA_V, F32), (GLA_V, F32),
            (SB_W, BF16), (SB_W, BF16), (SB_W, BF16), (CONV_CH, BF16))
    chan = lambda a: a.reshape(1, CONV_CH)
    return pl.pallas_call(
        functools.partial(_inproj_kernel, tpb),
        out_shape=[jax.ShapeDtypeStruct((N, wd), dt) for wd, dt in outs],
        grid=(N // tm,),
        in_specs=[
            _row_spec(tm, D),
            _const_spec((1, D)),
            _batch_vec_spec(tpb, D), _batch_vec_spec(tpb, D),
            _const_spec(w.shape), _const_spec(wup.shape), _const_spec(bg.shape),
            _const_spec(seg.shape), _const_spec((1, SB_W)), _const_spec((1, SB_W)),
            _const_spec(conv_w.shape), _const_spec((1, CONV_CH)), _const_spec((1, CONV_CH)), _const_spec((1, CONV_CH)),
        ],
        out_specs=[_row_spec(tm, wd) for wd, _ in outs],
        scratch_shapes=[pltpu.VMEM((tm + CONV_PAD, CONV_CH), F32),
                        pltpu.VMEM((SUBLANES - 1, tm + CONV_PAD, CONV_CH), F32)],
        compiler_params=pltpu.CompilerParams(
            dimension_semantics=("arbitrary",), vmem_limit_bytes=VMEM_LIMIT),
        name="inproj",
    )(x2, g.reshape(1, D), shift, scale, w, wup, bg, seg, heads(q_norm), heads(k_norm),
      conv_w, chan(conv_b), chan(conv_ln_g), chan(conv_ln_b))


_GLA_LEVELS = 6


def _gla_constants():
    C = CHUNK
    t = np.arange(C)[:, None]
    j = np.arange(C)[None, :]
    mats = [j <= t, j > t]
    masks = [t == j]
    for l in range(1, _GLA_LEVELS + 1):
        n, m = 1 << l, 1 << (l - 1)
        ref = t // n * n + m - 1
        right = (t % n) >= m
        mats.append((right & (j > ref) & (j <= t)) | ((~right) & (j > t) & (j <= ref)))
        masks.append(((t // n) == (j // n)) & right & ((j % n) < m))
    prefix = np.concatenate(mats, axis=0).astype(np.float32)
    prefix = np.concatenate([prefix, prefix], axis=1)
    masks = np.stack([np.tile(mk, (1, GLA_HEADS)) for mk in masks]).astype(np.float32)
    kl = np.arange(QK_PAD)[None, :] // GLA_DK
    vl = np.arange(GLA_V)[:, None] // GLA_DV
    state_mask = (kl == vl).astype(np.float32)
    return prefix, masks, state_mask


def _gla_kernel(q_ref, k_ref, a_ref, v_ref, g_ref, pre_ref, msk_ref, smask_ref, seg_ref, on_ref,
                o_ref, st_ref):
    C = CHUNK

    @pl.when(pl.program_id(1) == 0)
    def _():
        st_ref[...] = jnp.zeros_like(st_ref)

    klane = lax.broadcasted_iota(jnp.int32, (1, QK_PAD), 1) // GLA_DK
    vlane = lax.broadcasted_iota(jnp.int32, (1, GLA_V), 1) // GLA_DV
    n_grp = GLA_GROUP

    def group(i, carry):
        base = i * (n_grp * C)
        rows = [pl.ds(pl.multiple_of(base + j * C, C), C) for j in range(n_grp)]
        qs = [q_ref[0, r, :] * (GLA_DK ** -0.5) for r in rows]
        ks = [k_ref[0, r, :] for r in rows]
        vs = [v_ref[0, r, :] for r in rows]

        es = []
        for r in rows:
            es.append(_split_dot_left(pre_ref[...], a_ref[0, r, :]))

        atts = []
        for q, k, e in zip(qs, ks, es):
            att = None
            for l in range(_GLA_LEVELS + 1):
                if l == 0:
                    qt, kt = q, k
                else:
                    f = jnp.exp(e[(l + 1) * C:(l + 2) * C])
                    qt, kt = q * f, k * f
                kst = jnp.concatenate(
                    [jnp.where(klane == h, kt, 0.0) for h in range(GLA_HEADS)], axis=0).astype(BF16)
                term = _dot_nt(qt.astype(BF16), kst) * msk_ref[l]
                att = term if att is None else att + term
            atts.append(att)

        intra, upds, qbs, decs = [], [], [], []
        for q, k, v, e, att in zip(qs, ks, vs, es, atts):
            vst = jnp.concatenate(
                [jnp.where(vlane == h, v, 0.0) for h in range(GLA_HEADS)], axis=0).astype(BF16)
            intra.append(_dot(att.astype(BF16), vst))
            kdec = (k * jnp.exp(e[C:2 * C])).astype(BF16)
            upds.append(_dot_tn(v.astype(BF16), kdec) * smask_ref[...])
            qbs.append((q * jnp.exp(e[0:C])).astype(BF16))
            decs.append(jnp.exp(e[C - 1:C]))

        st = st_ref[...]
        outs = []
        for o_intra, upd, qb, dec in zip(intra, upds, qbs, decs):
            outs.append(o_intra + _dot_nt(qb, st.astype(BF16)))
            st = st * dec + upd
        st_ref[...] = st

        o = jnp.concatenate(outs, axis=0)
        out_rows = pl.ds(pl.multiple_of(base, n_grp * C), n_grp * C)
        y = _head_rms(o, seg_ref[...], on_ref[...], GLA_DV)
        o_ref[0, out_rows, :] = (y * _silu(g_ref[0, out_rows, :])).astype(o_ref.dtype)
        return carry

    lax.fori_loop(0, q_ref.shape[1] // (n_grp * C), group, 0)


def _gla(gq, gk, ga, gv, gg, out_norm):
    B, T, _ = gq.shape
    prefix, masks, state_mask = _gla_constants()
    tt = GLA_TILE
    seq_spec = lambda w: pl.BlockSpec((1, tt, w), lambda b, i: (b, i, 0))
    consts = (jnp.asarray(prefix, BF16), jnp.asarray(masks), jnp.asarray(state_mask),
              _same_head(GLA_V, GLA_DV), jnp.tile(out_norm, GLA_HEADS).reshape(1, GLA_V))
    return pl.pallas_call(
        _gla_kernel,
        out_shape=jax.ShapeDtypeStruct((B, T, GLA_V), BF16),
        grid=(B, T // tt),
        in_specs=[seq_spec(QK_PAD), seq_spec(QK_PAD), seq_spec(QK_PAD), seq_spec(GLA_V), seq_spec(GLA_V)]
                 + [_const_spec(cst.shape) for cst in consts],
        out_specs=seq_spec(GLA_V),
        scratch_shapes=[pltpu.VMEM((GLA_V, QK_PAD), F32)],
        compiler_params=pltpu.CompilerParams(
            dimension_semantics=("parallel", "arbitrary"), vmem_limit_bytes=VMEM_LIMIT),
        name="gla",
    )(gq, gk, ga, gv, gg, *consts)


def _sb_kernel(q_ref, k_ref, v_ref, og_ref, seg_ref, causal_ref, suf_ref, o_ref,
               qs_ref, v0_ref, v1_ref, acc_ref, carry_ref, alive_ref):
    T = q_ref.shape[1]
    BLK, G = SB_BLOCK, SB_GROUP
    n_q = T // BLK
    n_groups = n_q // G
    first = lax.broadcasted_iota(jnp.int32, (1, 2 * SB_DH), 1) < SB_DH

    v0_ref[...] = v_ref[0, :, 0:SB_DH]
    v1_ref[...] = v_ref[0, :, SB_DH:]

    for qi in range(n_q):
        q2 = q_ref[0, qi * BLK:(qi + 1) * BLK, :]
        zq = jnp.zeros_like(q2)
        qs_ref[qi, 0:BLK, :] = jnp.where(first, q2, zq)
        qs_ref[qi, BLK:, :] = jnp.where(first, zq, q2)

    def group_step(grp, dist, causal):
        qis = [grp * G + g for g in range(G)]
        krows = [pl.ds(pl.multiple_of(jnp.maximum(qi - dist, 0) * BLK, BLK), BLK) for qi in qis]
        zs = [_dot_nt(k_ref[0, kr, :], qs_ref[qi]) for qi, kr in zip(qis, krows)]
        drops = []
        for z in zs:
            sp = jnp.maximum(z, 0.0) + jnp.log(1.0 + jnp.exp2(jnp.abs(z) * (-LOG2E)))
            drops.append(sp * causal_ref[...] if causal else sp)
        sufs = [_split_dot_left(suf_ref[...], d) for d in drops]
        ws, alive = [], None
        for qi, z, suf in zip(qis, zs, sufs):
            carry = jnp.zeros((1, 2 * BLK), F32) if causal else jnp.where(qi >= dist, carry_ref[qi], SB_GONE)
            w = jnp.exp(z - suf + carry)
            ws.append((w * causal_ref[...] if causal else w).astype(BF16))
            carry = carry - suf[0:1, :]
            carry_ref[qi] = carry
            alive = carry if alive is None else jnp.maximum(alive, carry)
        for qi, kr, w in zip(qis, krows, ws):
            for h, v_ref_h in enumerate((v0_ref, v1_ref)):
                upd = _dot_tn(v_ref_h[kr, :], w[:, h * BLK:(h + 1) * BLK])
                out = (qi, slice(h * SB_DH, (h + 1) * SB_DH), slice(None))
                acc_ref[out] = upd if causal else acc_ref[out] + upd
        return jnp.max(alive) > -SB_SKIP

    def diagonal(grp, c0):
        alive_ref[grp] = group_step(grp, 0, True).astype(jnp.int32)
        return c0

    lax.fori_loop(0, n_groups, diagonal, 0)

    def any_alive():
        total = alive_ref[0]
        for g in range(1, n_groups):
            total = total + alive_ref[g]
        return total > 0

    def sweep(state):
        dist, _ = state

        def visit(grp, c0):
            run = jnp.logical_and(alive_ref[grp] > 0, grp * G + G - 1 >= dist)

            @pl.when(run)
            def _():
                alive_ref[grp] = group_step(grp, dist, False).astype(jnp.int32)

            @pl.when(jnp.logical_not(run))
            def _():
                alive_ref[grp] = 0

            return c0

        lax.fori_loop(0, n_groups, visit, 0)
        return dist + 1, any_alive()

    lax.while_loop(lambda s: jnp.logical_and(s[0] < n_q, s[1]), sweep, (1, any_alive()))

    def finish(grp, c0):
        cols = pl.ds(pl.multiple_of(grp * G * BLK, G * BLK), G * BLK)
        ot = jnp.concatenate([acc_ref[grp * G + g] for g in range(G)], axis=1)
        o_ref[0, cols, :] = _head_rms(ot.T, seg_ref[...], og_ref[...], SB_DH).astype(o_ref.dtype)
        return c0

    lax.fori_loop(0, n_groups, finish, 0)


def _sb_constants():
    BLK = SB_BLOCK
    s = np.arange(BLK)[:, None]
    j = np.arange(BLK)[None, :]
    suffix = (j >= s).astype(np.float32)
    suffix = np.concatenate([suffix, suffix], axis=1)
    causal =(s < np.tile(np.arange(BLK), 2)[None, :]).astype(np.float32)
    return causal, suffix


def _sb(sq, sk, sv, out_norm):
    B, T, _ = sq.shape
    causal, suffix = _sb_constants()
    consts = (jnp.tile(out_norm, 2).reshape(1, 2 * SB_DH), _same_head(2 * SB_DH, SB_DH),
              jnp.asarray(causal), jnp.asarray(suffix, BF16))
    spec = pl.BlockSpec((1, T, 2 * SB_DH), lambda b, p: (b, 0, p))
    assert T % (SB_GROUP * SB_BLOCK) == 0, (T, SB_GROUP, SB_BLOCK)
    n_q = T // SB_BLOCK
    return pl.pallas_call(
        _sb_kernel,
        out_shape=jax.ShapeDtypeStruct((B, T, SB_W), BF16),
        grid=(B, SB_HEADS // 2),
        in_specs=[spec, spec, spec] + [_const_spec(cst.shape) for cst in consts],
        out_specs=spec,
        scratch_shapes=[pltpu.VMEM((n_q, 2 * SB_BLOCK, 2 * SB_DH), BF16),
                        pltpu.VMEM((T, SB_DH), BF16),
                        pltpu.VMEM((T, SB_DH), BF16),
                        pltpu.VMEM((n_q, 2 * SB_DH, SB_BLOCK), F32),
                        pltpu.VMEM((n_q, 1, 2 * SB_BLOCK), F32),
                        pltpu.SMEM((n_q // SB_GROUP,), jnp.int32)],
        compiler_params=pltpu.CompilerParams(
            dimension_semantics=("parallel", "parallel"), vmem_limit_bytes=VMEM_LIMIT),
        name="stickbreak",
    )(sq, sk, sv, *consts)


def kernel(x, c, w_ada, b_ada, norm_ffn1, ffn1_w_in, ffn1_w_out, norm_mix, w_in, w_out, gla_w_gate_up, gla_b_gate, gla_out_norm, sb_q_norm, sb_k_norm, sb_out_norm, conv_w, conv_b, conv_ln_g, conv_ln_b, norm_ffn2, ffn2_w_in, ffn2_w_out):
    B, T, D = x.shape
    L = w_ada.shape[0]
    mod = _adaln(c, w_ada, b_ada).reshape(L, B, N_MOD, 1, D)
    x2 = x.reshape(B * T, D)
    ffn_w = (ffn1_w_in[0].astype(BF16), ffn1_w_out[0].astype(BF16))
    for l in range(L):
        sh1, sc1, gt1, sh2, sc2, gt2, sh3, sc3, gt3 = (mod[l, :, i] for i in range(N_MOD))
        x2, *ffn_w, w_mix, w_proj = _ffn(
            x2, norm_ffn1[l], sh1, sc1, gt1, *ffn_w, T,
            to_round=((ffn2_w_in, l, None), (ffn2_w_out, l, None), (w_out, l, None),
                      (w_in, l, (_PROJ_GROUPS, PROJ_W))))
        gq, gk, ga, gv, gg, sq, sk, sv, oc = _inproj(
            x2, norm_mix[l], sh2, sc2, w_proj, gla_w_gate_up[l], gla_b_gate[l], sb_q_norm[l], sb_k_norm[l],
            conv_w[l], conv_b[l], conv_ln_g[l], conv_ln_b[l], T)
        seq = lambda a: a.reshape(B, T, a.shape[-1])
        flat = lambda a: a.reshape(B * T, a.shape[-1])
        oa = _gla(seq(gq), seq(gk), seq(ga), seq(gv), seq(gg), gla_out_norm[l])
        ob = _sb(seq(sq), seq(sk), seq(sv), sb_out_norm[l])
        mix = (flat(oa), flat(ob), oc, gt2, w_mix)
        nxt = ((ffn1_w_in, l + 1, None), (ffn1_w_out, l + 1, None)) if l + 1 < L else ()
        x2, *ffn_w = _ffn(x2, norm_ffn2[l], sh3, sc3, gt3, *ffn_w, T, mix=mix, to_round=nxt)
    return x2.reshape(B, T, D)
```
